```python
import numpy as np
import jax
import jax.numpy as jnp
from jax import lax

D_MODEL = 1024
BATCH = 16
SEQ = 2048
DEPTH = 2

CTX_LEN = 256
GRID_W = 64

NA_HEADS = 4
NA_HEAD_DIM = 64
NA_WIN_ROWS = 8
NA_WIN_COLS = 16
NA_QBLOCK_COLS = 16
NA_KBLOCK_COLS = 32
MLA_HEADS = 4
MLA_NOPE_DIM = 64
MLA_ROPE_DIM = 32
MLA_V_DIM = 64
MLA_Q_RANK = 256
MLA_KV_RANK = 128
MLA_QBLOCK = 128
CM_GROUPS = 4
CM_GROUP_DIM = 64
CM_CHUNK = 128
SC_WIDTH = 256
SC_TAPS = 3

NA_WIDTH = NA_HEADS * NA_HEAD_DIM
MLA_WIDTH = MLA_HEADS * MLA_V_DIM
CM_WIDTH = CM_GROUPS * CM_GROUP_DIM
D_MIX = NA_WIDTH + MLA_WIDTH + CM_WIDTH + SC_WIDTH

IN_SIZES = (NA_WIDTH, NA_WIDTH, NA_WIDTH, NA_WIDTH,
            MLA_Q_RANK, MLA_KV_RANK, MLA_ROPE_DIM, MLA_WIDTH,
            CM_WIDTH, CM_WIDTH, CM_WIDTH,
            SC_WIDTH, SC_WIDTH, SC_WIDTH, SC_WIDTH)
D_IN = sum(IN_SIZES)
IN_SPLIT_POINTS = tuple(int(s) for s in np.cumsum(IN_SIZES)[:-1])

ROPE_BASE = 10000.0
NORM_EPS = 1e-6
NEG_INF = -1e9

kernel_name = "hybrid_headgroup_diffusion_block"


def rms_norm(x, g):
    xf = x.astype(jnp.float32)
    y = xf * lax.rsqrt(jnp.mean(xf * xf, axis=-1, keepdims=True) + NORM_EPS)
    return (y * g.astype(jnp.float32)).astype(x.dtype)


def layer_norm(x, g):
    xf = x.astype(jnp.float32)
    mu = jnp.mean(xf, axis=-1, keepdims=True)
    var = jnp.mean(jnp.square(xf - mu), axis=-1, keepdims=True)
    return ((xf - mu) * lax.rsqrt(var + NORM_EPS) * g.astype(jnp.float32)).astype(x.dtype)


def modulate(h, shift, scale):
    return h * (1 + scale) + shift


def heads(t, n_heads):
    return t.reshape(*t.shape[:-1], n_heads, t.shape[-1] // n_heads)


def attend_dense(q, k, v):
    s = jnp.einsum('bqhd,bkhd->bhqk', q, k).astype(jnp.float32) * (q.shape[-1] ** -0.5)
    p = jax.nn.softmax(s, axis=-1).astype(v.dtype)
    return jnp.einsum('bhqk,bkhd->bqhd', p, v)


def axial_rope(x):
    length = x.shape[1]
    nf = x.shape[-1] // 4
    t = jnp.arange(length)
    row = (t // GRID_W).astype(jnp.float32)
    col = (t % GRID_W).astype(jnp.float32)
    inv = ROPE_BASE ** (-jnp.arange(nf, dtype=jnp.float32) / nf)
    ang = jnp.stack([row[:, None] * inv, col[:, None] * inv], axis=1)
    cos = jnp.cos(ang)[:, None].astype(x.dtype)
    sin = jnp.sin(ang)[:, None].astype(x.dtype)
    xa = x.reshape(*x.shape[:-1], 2, 2, nf)
    x1, x2 = xa[..., 0, :], xa[..., 1, :]
    out = jnp.stack([x1 * cos - x2 * sin, x2 * cos + x1 * sin], axis=-2)
    return out.reshape(x.shape)


def na_tables(rpb, rows, kr):
    ncb = GRID_W // NA_QBLOCK_COLS
    r = np.arange(rows)
    rs = np.clip(r - kr // 2, 0, rows - kr)
    roff = rs[:, None] + np.arange(kr)[None, :] - r[:, None]
    qcol = np.arange(ncb)[:, None] * NA_QBLOCK_COLS + np.arange(NA_QBLOCK_COLS)[None, :]
    kcs = np.clip(np.arange(ncb) * NA_QBLOCK_COLS - NA_WIN_COLS // 2, 0, GRID_W - NA_KBLOCK_COLS)
    kcol = kcs[:, None] + np.arange(NA_KBLOCK_COLS)[None, :]
    cs = np.clip(qcol - NA_WIN_COLS // 2, 0, GRID_W - NA_WIN_COLS)
    kc3 = kcol[:, None, :]
    inwin = (kc3 >= cs[..., None]) & (kc3 < cs[..., None] + NA_WIN_COLS)
    coff = np.clip(kc3 - qcol[..., None], -(NA_WIN_COLS - 1), NA_WIN_COLS - 1)
    ri = (roff + NA_WIN_ROWS - 1)[:, None, None, :, None]
    ci = (coff + NA_WIN_COLS - 1)[None, :, :, None, :]
    bias = rpb[:, ri, ci].astype(jnp.float32)
    bias = jnp.where(inwin[None, None, :, :, None, :], bias, NEG_INF)
    return jnp.moveaxis(bias, 1, 0), kcol


def neighbourhood_attention(q, k, v, kc, vc, rpb):
    b_, length, n_h, d = q.shape
    rows = length // GRID_W
    kr = min(NA_WIN_ROWS, rows)
    ncb = GRID_W // NA_QBLOCK_COLS
    nloc = kr * NA_KBLOCK_COLS
    scale = d ** -0.5
    bias, kcol = na_tables(rpb, rows, kr)
    kg = k.reshape(b_, rows, GRID_W, n_h, d)
    vg = v.reshape(b_, rows, GRID_W, n_h, d)
    qr = jnp.moveaxis(q.reshape(b_, rows, ncb, NA_QBLOCK_COLS, n_h, d), 1, 0)

    def row_fn(args):
        r, q_r, b_r = args
        rs = jnp.clip(r - kr // 2, 0, rows - kr)
        k_blk = lax.dynamic_slice_in_dim(kg, rs, kr, axis=1)[:, :, kcol]
        v_blk = lax.dynamic_slice_in_dim(vg, rs, kr, axis=1)[:, :, kcol]
        s_loc = jnp.einsum('bjmhd,bijnhd->bhjmin', q_r, k_blk).astype(jnp.float32) * scale + b_r[None]
        s_ctx = jnp.einsum('bjmhd,bchd->bhjmc', q_r, kc).astype(jnp.float32) * scale
        s = jnp.concatenate([s_loc.reshape(*s_loc.shape[:4], nloc), s_ctx], axis=-1)
        p = jax.nn.softmax(s, axis=-1).astype(v.dtype)
        p_loc = p[..., :nloc].reshape(s_loc.shape)
        p_ctx = p[..., nloc:]
        return (jnp.einsum('bhjmin,bijnhd->bjmhd', p_loc, v_blk)
                + jnp.einsum('bhjmc,bchd->bjmhd', p_ctx, vc))

    out = lax.map(row_fn, (jnp.arange(rows), qr, bias))
    return jnp.moveaxis(out, 0, 1).reshape(b_, length, n_h * d)


def mla_queries(c_q, qn_g, w_uq):
    q = rms_norm(c_q, qn_g) @ w_uq
    return heads(q, MLA_HEADS)


def mla_rope_query(q):
    return jnp.concatenate([q[..., :MLA_NOPE_DIM], axial_rope(q[..., MLA_NOPE_DIM:])], axis=-1)


def mla_keys_values(c_kv, k_rope, kvn_g, w_ukv):
    kv = heads(rms_norm(c_kv, kvn_g) @ w_ukv, MLA_HEADS)
    return kv[..., :MLA_NOPE_DIM], k_rope[:, :, None, :], kv[..., MLA_NOPE_DIM:]


def mla_join_key(k_nope, k_pe):
    k_pe = jnp.broadcast_to(k_pe, k_nope.shape[:-1] + (MLA_ROPE_DIM,))
    return jnp.concatenate([k_nope, k_pe], axis=-1)


def mla_block_attention(q, k, v, kc, vc):
    k_all = jnp.concatenate([kc, k], axis=1)
    v_all = jnp.concatenate([vc, v], axis=1)
    b_, length, n_h, dq = q.shape
    nb = length // MLA_QBLOCK
    qb = jnp.moveaxis(q.reshape(b_, nb, MLA_QBLOCK, n_h, dq), 1, 0)
    out = lax.map(lambda q_blk: attend_dense(q_blk, k_all, v_all), qb)
    return jnp.moveaxis(out, 0, 1).reshape(b_, length, n_h * MLA_V_DIM)


def spatial_gating(u, v, ln_g, w_s, b_s):
    b_, length, _ = u.shape
    nc = length // CM_CHUNK
    vg = v.reshape(b_, nc, CM_CHUNK, CM_GROUPS, CM_GROUP_DIM)
    vn = layer_norm(vg, ln_g.reshape(CM_GROUPS, CM_GROUP_DIM))
    s = jnp.einsum('gij,bnjgc->bnigc', w_s, vn) + b_s.T[:, :, None]
    return u * s.reshape(b_, length, CM_WIDTH)


def short_conv(gb, gc, h, w):
    length = h.shape[1]
    pad = SC_TAPS // 2
    zp = jnp.pad(gc * h, ((0, 0), (pad, pad), (0, 0)))
    y = sum(zp[:, i:i + length] * w[i] for i in range(SC_TAPS))
    return gb * y


def merge_branches(outs, gates, w_out):
    return jnp.concatenate([o * jax.nn.silu(g) for o, g in zip(outs, gates)], axis=-1) @ w_out


def hybrid_layer(x, ctx, c, c_ctx, norm_g, w_mod, b_mod, w_in, na_rpb, mla_qn_g, mla_w_uq,
                 mla_kvn_g, mla_w_ukv, cm_ln_g, cm_w_s, cm_b_s, sc_w, w_out, update_ctx):
    mod_x = jax.nn.silu(c) @ w_mod + b_mod
    mod_c = jax.nn.silu(c_ctx) @ w_mod + b_mod
    sh_x, sc_x, g_x = jnp.split(mod_x[:, None, :], 3, axis=-1)
    sh_c, sc_c, g_c = jnp.split(mod_c, 3, axis=-1)
    hx = modulate(rms_norm(x, norm_g), sh_x, sc_x)
    hc = modulate(rms_norm(ctx, norm_g), sh_c, sc_c)
    (a_q, a_k, a_v, a_g, b_cq, b_ckv, b_kr, b_g,
     c_u, c_v, c_g, d_b, d_c, d_h, d_g) = jnp.split(hx @ w_in, IN_SPLIT_POINTS, axis=-1)
    (a_qc, a_kc, a_vc, a_gc, b_cqc, b_ckvc, b_krc, b_gc,
     c_uc, c_vc, c_gc, d_bc, d_cc, d_hc, d_gc) = jnp.split(hc @ w_in, IN_SPLIT_POINTS, axis=-1)

    a_kc_h = heads(a_kc, NA_HEADS)
    a_vc_h = heads(a_vc, NA_HEADS)
    b_kn_c, b_kpe_c, b_v_c = mla_keys_values(b_ckvc, b_krc, mla_kvn_g, mla_w_ukv)
    b_k_c = mla_join_key(b_kn_c, b_kpe_c)

    out_a = neighbourhood_attention(heads(a_q, NA_HEADS), heads(a_k, NA_HEADS), heads(a_v, NA_HEADS),
                                    a_kc_h, a_vc_h, na_rpb)
    b_q = mla_rope_query(mla_queries(b_cq, mla_qn_g, mla_w_uq))
    b_kn, b_kpe, b_v = mla_keys_values(b_ckv, b_kr, mla_kvn_g, mla_w_ukv)
    b_k = mla_join_key(b_kn, axial_rope(b_kpe))
    out_b = mla_block_attention(b_q, b_k, b_v, b_k_c, b_v_c)
    out_c = spatial_gating(jax.nn.gelu(c_u, approximate=False), jax.nn.gelu(c_v, approximate=False),
                           cm_ln_g, cm_w_s, cm_b_s)
    out_d = short_conv(d_b, d_c, d_h, sc_w)
    x_new = x + g_x * merge_branches((out_a, out_b, out_c, out_d), (a_g, b_g, c_g, d_g), w_out)

    if update_ctx:
        bc_, lc_, _ = ctx.shape
        ctx_a = attend_dense(heads(a_qc, NA_HEADS), a_kc_h, a_vc_h).reshape(bc_, lc_, NA_WIDTH)
        ctx_b = attend_dense(mla_queries(b_cqc, mla_qn_g, mla_w_uq), b_k_c, b_v_c).reshape(bc_, lc_, MLA_WIDTH)
        ctx_c = spatial_gating(jax.nn.gelu(c_uc, approximate=False), jax.nn.gelu(c_vc, approximate=False),
                               cm_ln_g, cm_w_s, cm_b_s)
        ctx_d = short_conv(d_bc, d_cc, d_hc, sc_w)
        ctx = ctx + g_c * merge_branches((ctx_a, ctx_b, ctx_c, ctx_d), (a_gc, b_gc, c_gc, d_gc), w_out)
    return x_new, ctx


def setup_inputs(seed: int = 0) -> dict:
    key = jax.random.key(seed)
    ks = jax.random.split(key, 24)
    f32 = jnp.float32

    def nrm(k, shape, scale):
        return jax.random.normal(k, shape, f32) * scale

    return {
        "x": nrm(ks[0], (BATCH, SEQ, D_MODEL), 1.0),
        "c": nrm(ks[1], (BATCH, D_MODEL), 1.0),
        "ctx": nrm(ks[2], (BATCH, CTX_LEN, D_MODEL), 1.0),
        "c_ctx": nrm(ks[3], (D_MODEL,), 1.0),
        "norm_g": 1.0 + nrm(ks[4], (DEPTH, D_MODEL), 0.02),
        "w_mod": nrm(ks[5], (DEPTH, D_MODEL, 3 * D_MODEL), 0.5 * D_MODEL ** -0.5),
        "b_mod": nrm(ks[6], (DEPTH, 3 * D_MODEL), 0.01),
        "w_in": nrm(ks[7], (DEPTH, D_MODEL, D_IN), D_MODEL ** -0.5),
        "na_rpb": nrm(ks[8], (DEPTH, NA_HEADS, 2 * NA_WIN_ROWS - 1, 2 * NA_WIN_COLS - 1), 0.1),
        "mla_qn_g": 1.0 + nrm(ks[9], (DEPTH, MLA_Q_RANK), 0.02),
        "mla_w_uq": nrm(ks[10], (DEPTH, MLA_Q_RANK, MLA_HEADS * (MLA_NOPE_DIM + MLA_ROPE_DIM)), MLA_Q_RANK ** -0.5),
        "mla_kvn_g": 1.0 + nrm(ks[11], (DEPTH, MLA_KV_RANK), 0.02),
        "mla_w_ukv": nrm(ks[12], (DEPTH, MLA_KV_RANK, MLA_HEADS * (MLA_NOPE_DIM + MLA_V_DIM)), MLA_KV_RANK ** -0.5),
        "cm_ln_g": 1.0 + nrm(ks[13], (DEPTH, CM_WIDTH), 0.02),
        "cm_w_s": nrm(ks[14], (DEPTH, CM_GROUPS, CM_CHUNK, CM_CHUNK), CM_CHUNK ** -0.5),
        "cm_b_s": 1.0 + nrm(ks[15], (DEPTH, CM_GROUPS, CM_CHUNK), 0.02),
        "sc_w": nrm(ks[16], (DEPTH, SC_TAPS, SC_WIDTH), SC_TAPS ** -0.5),
        "w_out": nrm(ks[17], (DEPTH, D_MIX, D_MODEL), D_MIX ** -0.5),
        "final_g": 1.0 + nrm(ks[18], (D_MODEL,), 0.02),
    }


def reference(x, c, ctx, c_ctx, norm_g, w_mod, b_mod, w_in, na_rpb, mla_qn_g, mla_w_uq, mla_kvn_g,
              mla_w_ukv, cm_ln_g, cm_w_s, cm_b_s, sc_w, w_out, final_g):
    for l in range(DEPTH):
        x, ctx = hybrid_layer(x, ctx, c, c_ctx, norm_g[l], w_mod[l], b_mod[l], w_in[l], na_rpb[l],
                              mla_qn_g[l], mla_w_uq[l], mla_kvn_g[l], mla_w_ukv[l], cm_ln_g[l],
                              cm_w_s[l], cm_b_s[l], sc_w[l], w_out[l], update_ctx=(l < DEPTH - 1))
    return rms_norm(x, final_g)
```

```python
import functools

import numpy as np
import jax
import jax.numpy as jnp
from jax import lax
from jax.experimental import pallas as pl
from jax.experimental.pallas import tpu as pltpu

F32 = jnp.float32
BF16 = jnp.bfloat16

GRID_W = 64
NA_HEADS = 4
NA_HEAD_DIM = 64
NA_WIN_ROWS = 8
NA_WIN_COLS = 16
MLA_HEADS = 4
MLA_NOPE_DIM = 64
MLA_ROPE_DIM = 32
MLA_V_DIM = 64
MLA_Q_RANK = 256
MLA_KV_RANK = 128
CM_GROUPS = 4
CM_GROUP_DIM = 64
CM_CHUNK = 128
SC_TAPS = 3
ROPE_BASE = 10000.0
NORM_EPS = 1e-6
NEG_INF = -1e9

GW = 256
LANES = 128
NA_QROWS = 8
NA_KROWS = 16
NA_KCHUNK_ROWS = 4
HALO = 16
VMEM_LIMIT = 56 * 1024 * 1024

C_AQ, C_AK, C_AV, C_AG = 0, 256, 512, 768
C_CQ, C_CKV, C_KR, C_BG = 1024, 1280, 1408, 1536
C_CU, C_CV, C_CG = 1792, 2048, 2304
C_DB, C_DC, C_DH, C_DG = 2560, 2816, 3072, 3328
D_IN_P = 3584


def _silu(x):
    return x / (1.0 + jnp.exp(-x))


def _gelu(x):
    return 0.5 * x * (1.0 + lax.erf(x * 0.7071067811865476))


def _rms(x):
    return x * lax.rsqrt(jnp.mean(x * x, axis=-1, keepdims=True) + NORM_EPS)


def _dot(a, b):
    return jnp.dot(a, b, preferred_element_type=F32)


def _dot_nt(a, b):
    return lax.dot_general(a, b, (((1,), (1,)), ((), ())), preferred_element_type=F32)


def _cparams(n_grid):
    return pltpu.CompilerParams(dimension_semantics=("parallel",) * n_grid,
                                vmem_limit_bytes=VMEM_LIMIT)


def _mod_kernel(c_ref, w_ref, b_ref, o_ref):
    s = _silu(c_ref[...]).astype(BF16)
    o_ref[...] = _dot(s, w_ref[...].astype(BF16)) + b_ref[...]


def _modulation(cc, w_mod, b_mod):
    depth, d, n = w_mod.shape
    r = cc.shape[0]
    tn = 768
    return pl.pallas_call(
        _mod_kernel,
        grid=(depth, n // tn),
        in_specs=[pl.BlockSpec((r, d), lambda l, j: (0, 0)),
                  pl.BlockSpec((None, d, tn), lambda l, j: (l, 0, j)),
                  pl.BlockSpec((None, 1, tn), lambda l, j: (l, 0, j))],
        out_specs=pl.BlockSpec((None, r, tn), lambda l, j: (l, 0, j)),
        out_shape=jax.ShapeDtypeStruct((depth, r, n), F32),
        compiler_params=_cparams(2),
        name="modulation",
    )(cc, w_mod, b_mod.reshape(depth, 1, n))


def _rope128(x, cos, sina, sinb):
    return x * cos + pltpu.roll(x, LANES - 8, 1) * sina + pltpu.roll(x, 8, 1) * sinb


def _inproj_kernel(x_ref, sh_ref, sc_ref, ng_ref, w_ref, qng_ref, wuq_ref, kvng_ref, wukv_ref,
                   cos_ref, sina_ref, sinb_ref, lng_ref, gavg_ref, ws_ref, bs_ref,
                   aq_ref, ak_ref, av_ref, ga_ref, bq_ref, bk_ref, bv_ref, gb_ref, mc_ref, p_ref,
                   bgd_ref):
    tm = x_ref.shape[0]
    h = _rms(x_ref[...]) * ng_ref[...]
    hb = (h * (1.0 + sc_ref[...]) + sh_ref[...]).astype(BF16)

    def proj(off, width=GW):
        return _dot(hb, w_ref[:, off:off + width])

    aq_ref[...] = (proj(C_AQ) * (NA_HEAD_DIM ** -0.5)).astype(BF16)
    ak_ref[...] = proj(C_AK).astype(BF16)
    av_ref[...] = proj(C_AV).astype(BF16)
    ga_ref[...] = _silu(proj(C_AG)).astype(BF16)

    cos, sina, sinb = cos_ref[...], sina_ref[...], sinb_ref[...]
    qscale = (MLA_NOPE_DIM + MLA_ROPE_DIM) ** -0.5
    qn =(_rms(proj(C_CQ)) * qng_ref[...]).astype(BF16)
    qu = _dot(qn, wuq_ref[...])
    for p in range(MLA_HEADS // 2):
        o = p * GW
        bq_ref[:, o:o + LANES] = (_rope128(qu[:, o:o + LANES], cos, sina, sinb) * qscale).astype(BF16)
        bq_ref[:, o + LANES:o + GW] = (qu[:, o + LANES:o + GW] * qscale).astype(BF16)
    kvn = (_rms(proj(C_CKV, MLA_KV_RANK)) * kvng_ref[...]).astype(BF16)
    kvu = _dot(kvn, wukv_ref[...])
    kr = _rope128(proj(C_KR, LANES), cos, sina, sinb).astype(BF16)
    for p in range(MLA_HEADS // 2):
        o = p * GW
        bk_ref[:, o:o + LANES] = kr
        bk_ref[:, o + LANES:o + GW] = kvu[:, p * LANES:(p + 1) * LANES].astype(BF16)
    bv_ref[...] = kvu[:, 2 * LANES:].astype(BF16)
    gb_ref[...] = _silu(proj(C_BG)).astype(BF16)

    u = _gelu(proj(C_CU))
    v = _gelu(proj(C_CV))
    gavg = gavg_ref[...]

    def group_mean(t):
        hi = t.astype(BF16)
        lo = (t - hi.astype(F32)).astype(BF16)
        return _dot(hi, gavg) + _dot(lo, gavg)

    cen = v - group_mean(v)
    vn = (cen * lax.rsqrt(group_mean(cen * cen) + NORM_EPS) * lng_ref[...]).astype(BF16)
    lane = lax.broadcasted_iota(jnp.int32, (1, GW), 1)
    gate_c = _silu(proj(C_CG))
    for n in range(tm // CM_CHUNK):
        rows = slice(n * CM_CHUNK, (n + 1) * CM_CHUNK)
        vc = vn[rows]
        s = bs_ref[...]
        for g in range(CM_GROUPS):
            in_g = (lane >= g * CM_GROUP_DIM) & (lane < (g + 1) * CM_GROUP_DIM)
            s = s + _dot(ws_ref[:, g * CM_CHUNK:(g + 1) * CM_CHUNK], jnp.where(in_g, vc, jnp.zeros_like(vc)))
        mc_ref[rows, :] = (u[rows] * s * gate_c[rows]).astype(BF16)

    p_ref[...] = (proj(C_DC) * proj(C_DH)).astype(BF16)
    bgd_ref[...] = (proj(C_DB) * _silu(proj(C_DG))).astype(BF16)


def _inproj(x, shift, scale, lw, tabs, tm):
    b, l, d = x.shape
    cos, sina, sinb = tabs
    full = lambda a: pl.BlockSpec(a.shape, lambda bi, i: (0,) * a.ndim)
    tok = lambda width: pl.BlockSpec((None, tm, width), lambda bi, i: (bi, i, 0))
    per_b = pl.BlockSpec((None, 1, d), lambda bi, i: (bi, 0, 0))
    pos = pl.BlockSpec((tm, LANES), lambda bi, i: (i, 0))
    widths = (GW, GW, GW, GW, 2 * GW, 2 * GW, GW, GW, GW, GW, GW)
    consts = (lw["norm_g"], lw["w_in"], lw["qn_g"], lw["w_uq"], lw["kvn_g"], lw["w_ukv"])
    consts2 = (lw["ln_g"], lw["gavg"], lw["ws"], lw["bs"])
    return pl.pallas_call(
        _inproj_kernel,
        grid=(b, l // tm),
        in_specs=[tok(d), per_b, per_b] + [full(a) for a in consts] + [pos, pos, pos]
                 + [full(a) for a in consts2],
        out_specs=[tok(w) for w in widths],
        out_shape=[jax.ShapeDtypeStruct((b, l, w), BF16) for w in widths],
        compiler_params=_cparams(2),
        name="inproj",
    )(x, shift, scale, *consts, cos, sina, sinb, *consts2)


def _lane_mask(ranges, width):
    lane = lax.broadcasted_iota(jnp.int32, (1, width), 1)
    m = None
    for lo, hi in ranges:
        t = (lane >= lo) & (lane < hi)
        m = t if m is None else (m | t)
    return m


def _softmax_pv(scores, values):
    m = scores[0].max(axis=-1, keepdims=True)
    for s in scores[1:]:
        m = jnp.maximum(m, s.max(axis=-1, keepdims=True))
    l = None
    o = None
    for s, v in zip(scores, values):
        e = jnp.exp(s - m)
        es = e.sum(axis=-1, keepdims=True)
        l = es if l is None else l + es
        pv = _dot(e.astype(BF16), v)
        o = pv if o is None else o + pv
    return o * (1.0 / l)


def _attn_kernel(*refs, head_specs, n_src):
    q_ref = refs[0]
    k_refs = refs[1:1 + n_src]
    v_refs = refs[1 + n_src:1 + 2 * n_src]
    gate_ref, o_ref = refs[1 + 2 * n_src], refs[2 + 2 * n_src]
    acc = jnp.zeros(o_ref.shape, F32)
    for hd, (grp, ranges) in enumerate(head_specs):
        q = q_ref[:, grp * GW:(grp + 1) * GW]
        qm = jnp.where(_lane_mask(ranges, GW), q, jnp.zeros_like(q))
        scores = [_dot_nt(qm, k[:, grp * GW:(grp + 1) * GW]) for k in k_refs]
        o = _softmax_pv(scores, [v[...] for v in v_refs])
        acc = jnp.where(_lane_mask([(hd * 64, hd * 64 + 64)], GW), o, acc)
    o_ref[...] = (acc * gate_ref[...].astype(F32)).astype(BF16)


def _dense_attention(q, ks, vs, gate, head_specs, tq):
    b, lq, qw = q.shape
    n_src = len(ks)
    kv_spec = lambda a: pl.BlockSpec((None,) + a.shape[1:], lambda bi, i: (bi, 0, 0))
    return pl.pallas_call(
        functools.partial(_attn_kernel, head_specs=head_specs, n_src=n_src),
        grid=(b, lq // tq),
        in_specs=[pl.BlockSpec((None, tq, qw), lambda bi, i: (bi, i, 0))]
                 + [kv_spec(a) for a in ks] + [kv_spec(a) for a in vs]
                 + [pl.BlockSpec((None, tq, GW), lambda bi, i: (bi, i, 0))],
        out_specs=pl.BlockSpec((None, tq, GW), lambda bi, i: (bi, i, 0)),
        out_shape=jax.ShapeDtypeStruct((b, lq, GW), BF16),
        compiler_params=_cparams(2),
        name="dense_attention",
    )(q, *ks, *vs, gate)


NA_HEAD_SPECS = tuple((0, ((h * 64, h * 64 + 64),)) for h in range(NA_HEADS))
MLA_HEAD_SPECS = tuple((h // 2, ((32 * (h % 2), 32 * (h % 2) + 32),
                                 (LANES + 64 * (h % 2), LANES + 64 * (h % 2) + 64)))
                       for h in range(MLA_HEADS))


def _na_kernel(q_ref, k0, k1, k2, k3, v0, v1, v2, v3, kc_ref, vc_ref, strip_ref, rowm_ref, gate_ref,
               o_ref):
    rb = pl.program_id(0)
    tq = q_ref.shape[0]
    kw = NA_KCHUNK_ROWS * GRID_W
    t_base = 12 + jnp.where(rb == 0, 0, jnp.where(rb == pl.num_programs(0) - 1, 8, 4))
    q = q_ref[...]
    acc = jnp.zeros(o_ref.shape, F32)
    for hd in range(NA_HEADS):
        in_h = _lane_mask([(hd * 64, hd * 64 + 64)], GW)
        qm = jnp.where(in_h, q, jnp.zeros_like(q))
        scores = []
        for j, k in enumerate((k0, k1, k2, k3)):
            start = pl.multiple_of((t_base - NA_KCHUNK_ROWS * j) * GRID_W, GRID_W)
            bias = strip_ref[hd, pl.ds(start, tq), :] + rowm_ref[:, j * kw:(j + 1) * kw]
            scores.append(_dot_nt(qm, k[...]) + bias)
        scores.append(_dot_nt(qm, kc_ref[...]))
        o = _softmax_pv(scores, [v0[...], v1[...], v2[...], v3[...], vc_ref[...]])
        acc = jnp.where(in_h, o, acc)
    o_ref[...] = (acc * gate_ref[...].astype(F32)).astype(BF16)


def _na_tables(rpb, rows):
    n_h = rpb.shape[0]
    qc = np.arange(GRID_W)[:, None]
    kc = np.arange(GRID_W)[None, :]
    cs = np.clip(qc - NA_WIN_COLS // 2, 0, GRID_W - NA_WIN_COLS)
    col_ok = (kc >= cs) & (kc < cs + NA_WIN_COLS)
    ci = np.clip(kc - qc, -(NA_WIN_COLS - 1), NA_WIN_COLS - 1) + NA_WIN_COLS - 1
    onehot = (ci[None] == np.arange(2 * NA_WIN_COLS - 1)[:, None, None]).astype(np.float32)
    mc = jnp.einsum("hac,cqk->haqk", rpb.astype(F32), jnp.asarray(onehot),
                    precision=lax.Precision.HIGHEST)
    mc = jnp.where(jnp.asarray(col_ok)[None, None], mc, NEG_INF)
    n_ext = 31
    pad = (n_ext - (2 * NA_WIN_ROWS - 1)) // 2
    neg = jnp.full((n_h, pad, GRID_W, GRID_W), NEG_INF, F32)
    mc_ext = jnp.concatenate([neg, mc, neg], axis=1)
    n_strips = 28
    rev = mc_ext[:, ::-1]
    parts = [rev[:, 3 - i:3 - i + n_strips] for i in range(NA_KCHUNK_ROWS)]
    strips = jnp.stack(parts, axis=3)
    strips = strips.reshape(n_h, n_strips * GRID_W, NA_KCHUNK_ROWS * GRID_W)

    kr_win = min(NA_WIN_ROWS, rows)
    n_rb = rows // NA_QROWS
    qr = np.arange(rows)
    rs = np.clip(qr - kr_win // 2, 0, rows - kr_win)
    ks = np.clip(np.arange(n_rb) * NA_QROWS - NA_KROWS // 4, 0, rows - NA_KROWS)
    kr = ks[:, None] + np.arange(NA_KROWS)[None, :]
    rs_b = rs.reshape(n_rb, NA_QROWS)
    ok = (kr[:, None, :] >= rs_b[:, :, None]) & (kr[:, None, :] < rs_b[:, :, None] + kr_win)
    rowm = np.where(ok, 0.0, NEG_INF).astype(np.float32)
    rowm = np.broadcast_to(rowm[:, :, None, :, None], (n_rb, NA_QROWS, GRID_W, NA_KROWS, GRID_W))
    rowm = rowm.reshape(n_rb, NA_QROWS * GRID_W, NA_KROWS * GRID_W)
    return strips, jnp.asarray(rowm)


def _neighbourhood_attention(q, k, v, kc, vc, strips, rowm, gate):
    b, l, _ = q.shape
    tq = NA_QROWS * GRID_W
    kw = NA_KCHUNK_ROWS * GRID_W
    n_rb = l // tq
    n_kblk = l // kw

    def kv_spec(j):
        return pl.BlockSpec((None, kw, GW),
                            lambda rb, bi: (bi, jnp.clip(2 * rb - 1, 0, n_kblk - 4) + j, 0))

    tok = pl.BlockSpec((None, tq, GW), lambda rb, bi: (bi, rb, 0))
    ctx_spec = pl.BlockSpec((None,) + kc.shape[1:], lambda rb, bi: (bi, 0, 0))
    return pl.pallas_call(
        _na_kernel,
        grid=(n_rb, b),
        in_specs=[tok] + [kv_spec(j) for j in range(4)] * 2 + [ctx_spec, ctx_spec,
                  pl.BlockSpec(strips.shape, lambda rb, bi: (0, 0, 0)),
                  pl.BlockSpec((None,) + rowm.shape[1:], lambda rb, bi: (rb, 0, 0)),
                  tok],
        out_specs=tok,
        out_shape=jax.ShapeDtypeStruct((b, l, GW), BF16),
        compiler_params=_cparams(2),
        name="neighbourhood_attention",
    )(q, k, k, k, k, v, v, v, v, kc, vc, strips, rowm, gate)


def _outproj_kernel(oa_ref, ob_ref, mc_ref, p_ref, pprev_ref, pnext_ref, bgd_ref, scw_ref, w_ref,
                    g_ref, x_ref, fg_ref, o_ref, *, final_norm):
    i = pl.program_id(1)
    tm = x_ref.shape[0]
    p = p_ref[...].astype(F32)
    row = lax.broadcasted_iota(jnp.int32, (tm, 1), 0)
    prev_row = jnp.where(i == 0, 0.0, pprev_ref[HALO - 1:HALO, :].astype(F32))
    next_row = jnp.where(i == pl.num_programs(1) - 1, 0.0, pnext_ref[0:1, :].astype(F32))
    p_dn = jnp.where(row == 0, prev_row, pltpu.roll(p, 1, 0))
    p_up = jnp.where(row == tm - 1, next_row, pltpu.roll(p, tm - 1, 0))
    y = p_dn * scw_ref[0:1, :] + p * scw_ref[1:2, :] + p_up * scw_ref[2:3, :]
    md = (bgd_ref[...].astype(F32) * y).astype(BF16)
    acc = (_dot(oa_ref[...], w_ref[0:GW, :]) + _dot(ob_ref[...], w_ref[GW:2 * GW, :])
           + _dot(mc_ref[...], w_ref[2 * GW:3 * GW, :]) + _dot(md, w_ref[3 * GW:4 * GW, :]))
    xn = x_ref[...] + g_ref[...] * acc
    if final_norm:
        xn = _rms(xn) * fg_ref[...]
    o_ref[...] = xn


def _outproj(oa, ob, mc, p, bgd, sc_w, w_out, g, x, final_g, tm, final_norm):
    b, l, d = x.shape
    n_halo = l // HALO
    per_tile = tm // HALO
    tok = lambda width: pl.BlockSpec((None, tm, width), lambda bi, i: (bi, i, 0))
    full = lambda a: pl.BlockSpec(a.shape, lambda bi, i: (0,) * a.ndim)
    prev = pl.BlockSpec((None, HALO, GW), lambda bi, i: (bi, jnp.maximum(i * per_tile - 1, 0), 0))
    nxt = pl.BlockSpec((None, HALO, GW),
                       lambda bi, i: (bi, jnp.minimum((i + 1) * per_tile, n_halo - 1), 0))
    return pl.pallas_call(
        functools.partial(_outproj_kernel, final_norm=final_norm),
        grid=(b, l // tm),
        in_specs=[tok(GW), tok(GW), tok(GW), tok(GW), prev, nxt, tok(GW), full(sc_w), full(w_out),
                  pl.BlockSpec((None, 1, d), lambda bi, i: (bi, 0, 0)), tok(d), full(final_g)],
        out_specs=tok(d),
        out_shape=jax.ShapeDtypeStruct((b, l, d), F32),
        compiler_params=_cparams(2),
        name="outproj",
    )(oa, ob, mc, p, p, p, bgd, sc_w, w_out, g, x, final_g)


def _prep_w_in(w):
    kr = w[:, 1408:1440]
    z = jnp.zeros((w.shape[0], 64), w.dtype)
    return jnp.concatenate([w[:, :1440], kr, z, w[:, 1440:]], axis=1).astype(BF16)


def _prep_w_uq(w):
    hd = MLA_NOPE_DIM + MLA_ROPE_DIM
    z = jnp.zeros((w.shape[0], 64), w.dtype)
    cols = []
    for p in range(MLA_HEADS // 2):
        h0, h1 = 2 * p, 2 * p + 1
        cols += [w[:, hd * h0 + MLA_NOPE_DIM:hd * (h0 + 1)], w[:, hd * h1 + MLA_NOPE_DIM:hd * (h1 + 1)], z,
                 w[:, hd * h0:hd * h0 + MLA_NOPE_DIM], w[:, hd * h1:hd * h1 + MLA_NOPE_DIM]]
    return jnp.concatenate(cols, axis=1).astype(BF16)


def _prep_w_ukv(w):
    hd = MLA_NOPE_DIM + MLA_V_DIM
    kn = [w[:, hd * h:hd * h + MLA_NOPE_DIM] for h in range(MLA_HEADS)]
    vv = [w[:, hd * h + MLA_NOPE_DIM:hd * (h + 1)] for h in range(MLA_HEADS)]
    return jnp.concatenate(kn + vv, axis=1).astype(BF16)


def _rope_tables(length):
    nf = MLA_ROPE_DIM // 4
    t = jnp.arange(length)
    row = (t // GRID_W).astype(F32)
    col = (t % GRID_W).astype(F32)
    inv = ROPE_BASE ** (-jnp.arange(nf, dtype=F32) / nf)
    ang = jnp.stack([row[:, None] * inv, col[:, None] * inv], axis=1)
    cos = jnp.cos(ang)
    sin = jnp.sin(ang)
    zero = jnp.zeros_like(sin)
    cos32 = jnp.stack([cos, cos], axis=2).reshape(length, MLA_ROPE_DIM)
    sina32 = jnp.stack([-sin, zero], axis=2).reshape(length, MLA_ROPE_DIM)
    sinb32 = jnp.stack([zero, sin], axis=2).reshape(length, MLA_ROPE_DIM)
    pad1 = jnp.ones((length, 64), F32)
    pad0 = jnp.zeros((length, 64), F32)
    return (jnp.concatenate([cos32, cos32, pad1], axis=1),
            jnp.concatenate([sina32, sina32, pad0], axis=1),
            jnp.concatenate([sinb32, sinb32, pad0], axis=1))


def _identity_rope_tables(length):
    return (jnp.ones((length, LANES), F32), jnp.zeros((length, LANES), F32),
            jnp.zeros((length, LANES), F32))


def kernel(x, c, ctx, c_ctx, norm_g, w_mod, b_mod, w_in, na_rpb, mla_qn_g, mla_w_uq, mla_kvn_g,
           mla_w_ukv, cm_ln_g, cm_w_s, cm_b_s, sc_w, w_out, final_g):
    b, l, d = x.shape
    lc = ctx.shape[1]
    depth = w_in.shape[0]
    rows = l // GRID_W
    assert l % (NA_QROWS * GRID_W) == 0 and rows >= NA_KROWS and lc % CM_CHUNK == 0
    assert w_in.shape[2] == 3488 and d == 4 * GW

    n_mod = -(-(b + 1) // 8) * 8
    cc = jnp.zeros((n_mod, d), F32).at[:b].set(c).at[b].set(c_ctx)
    mod = _modulation(cc, w_mod, b_mod)

    rope_x = _rope_tables(l)
    rope_c = _identity_rope_tables(lc)
    gavg = jnp.asarray(np.kron(np.eye(CM_GROUPS), np.full((CM_GROUP_DIM, CM_GROUP_DIM),
                                                          1.0 / CM_GROUP_DIM)), BF16)
    fg =final_g.reshape(1, d)

    for layer in range(depth):
        last = layer == depth - 1
        lw = {
            "norm_g": norm_g[layer].reshape(1, d),
            "w_in": _prep_w_in(w_in[layer]),
            "qn_g": mla_qn_g[layer].reshape(1, MLA_Q_RANK),
            "w_uq": _prep_w_uq(mla_w_uq[layer]),
            "kvn_g": mla_kvn_g[layer].reshape(1, MLA_KV_RANK),
            "w_ukv": _prep_w_ukv(mla_w_ukv[layer]),
            "ln_g": cm_ln_g[layer].reshape(1, GW),
            "gavg": gavg,
            "ws": jnp.transpose(cm_w_s[layer], (1, 0, 2)).reshape(CM_CHUNK, CM_GROUPS * CM_CHUNK).astype(BF16),
            "bs": jnp.repeat(cm_b_s[layer].T, CM_GROUP_DIM, axis=1),
        }
        w_out_b = w_out[layer].astype(BF16)
        m = mod[layer]
        sh_x, sc_x, g_x = (m[:b, i * d:(i + 1) * d].reshape(b, 1, d) for i in range(3))
        sh_c, sc_c, g_c = (jnp.broadcast_to(m[b, i * d:(i + 1) * d], (b, 1, d)) for i in range(3))

        (aq, ak, av, ga, bq, bk, bv, gb, mc, pp, bgd) = _inproj(x, sh_x, sc_x, lw, rope_x, tm=512)
        (aqc, akc, avc, gac, bqc, bkc, bvc, gbc, mcc, ppc, bgdc) = _inproj(ctx, sh_c, sc_c, lw, rope_c, tm=lc)

        strips, rowm = _na_tables(na_rpb[layer], rows)
        out_a = _neighbourhood_attention(aq, ak, av, akc, avc, strips, rowm, ga)
        out_b = _dense_attention(bq, [bk, bkc], [bv, bvc], gb, MLA_HEAD_SPECS, tq=256)
        x_new = _outproj(out_a, out_b, mc, pp, bgd, sc_w[layer], w_out_b, g_x, x, fg, tm=512,
                         final_norm=last)
        if not last:
            ctx_a = _dense_attention(aqc, [akc], [avc], gac, NA_HEAD_SPECS, tq=lc)
            ctx_b = _dense_attention(bqc, [bkc], [bvc], gbc, MLA_HEAD_SPECS, tq=lc)
            ctx = _outproj(ctx_a, ctx_b, mcc, ppc, bgdc, sc_w[layer], w_out_b, g_c, ctx, fg, tm=lc,
                           final_norm=False)
        x = x_new
    return x
```

```python
import functools

import numpy as np
import jax
import jax.numpy as jnp
from jax import lax
from jax.experimental import pallas as pl
from jax.experimental.pallas import tpu as pltpu

F32 = jnp.float32
BF16 = jnp.bfloat16

GRID_W = 64
NA_HEADS = 4
NA_HEAD_DIM = 64
NA_WIN_ROWS = 8
NA_WIN_COLS = 16
MLA_HEADS = 4
MLA_NOPE_DIM = 64
MLA_ROPE_DIM = 32
MLA_V_DIM = 64
MLA_Q_RANK = 256
MLA_KV_RANK = 128
CM_GROUPS = 4
CM_GROUP_DIM = 64
CM_CHUNK = 128
SC_TAPS = 3
ROPE_BASE = 10000.0
NORM_EPS = 1e-6
NEG_INF = -1e9
LOG2E = 1.4426950408889634

GW = 256
LANES = 128
NA_QROWS = 8
NA_KROWS = 16
NA_KCHUNK_ROWS = 4
NA_BATCH_PER_STEP = 4
HALO = 16
VMEM_LIMIT = 56 * 1024 * 1024

C_AQ, C_AK, C_AV, C_AG = 0, 256, 512, 768
C_CQ, C_CKV, C_KR, C_BG = 1024, 1280, 1408, 1536
C_CU, C_CV, C_CG = 1792, 2048, 2304
C_DB, C_DC, C_DH, C_DG = 2560, 2816, 3072, 3328
D_IN_P = 3584


def _silu(x):
    return x / (1.0 + jnp.exp(-x))


def _gelu(x):
    return 0.5 * x * (1.0 + lax.erf(x * 0.7071067811865476))


def _rms(x):
    return x * lax.rsqrt(jnp.mean(x * x, axis=-1, keepdims=True) + NORM_EPS)


def _dot(a, b):
    return jnp.dot(a, b, preferred_element_type=F32)


def _dot_nt(a, b):
    return lax.dot_general(a, b, (((1,), (1,)), ((), ())), preferred_element_type=F32)


def _cparams(n_grid):
    return pltpu.CompilerParams(dimension_semantics=("parallel",) * n_grid,
                                vmem_limit_bytes=VMEM_LIMIT)


def _mod_kernel(c_ref, w_ref, b_ref, o_ref):
    s = _silu(c_ref[...]).astype(BF16)
    o_ref[...] = _dot(s, w_ref[...].astype(BF16)) + b_ref[...]


def _modulation(cc, w_mod, b_mod):
    depth, d, n = w_mod.shape
    r = cc.shape[0]
    tn = 768
    return pl.pallas_call(
        _mod_kernel,
        grid=(depth, n // tn),
        in_specs=[pl.BlockSpec((r, d), lambda l, j: (0, 0)),
                  pl.BlockSpec((None, d, tn), lambda l, j: (l, 0, j)),
                  pl.BlockSpec((None, 1, tn), lambda l, j: (l, 0, j))],
        out_specs=pl.BlockSpec((None, r, tn), lambda l, j: (l, 0, j)),
        out_shape=jax.ShapeDtypeStruct((depth, r, n), F32),
        compiler_params=_cparams(2),
        name="modulation",
    )(cc, w_mod, b_mod.reshape(depth, 1, n))


def _rope128(x, cos, sina, sinb):
    return x * cos + pltpu.roll(x, LANES - 8, 1) * sina + pltpu.roll(x, 8, 1) * sinb


def _inproj_kernel(x_ref, sh_ref, sc_ref, ng_ref, w_ref, qng_ref, wuq_ref, kvng_ref, wukv_ref,
                   cos_ref, sina_ref, sinb_ref, lng_ref, gavg_ref, ws_ref, bs_ref,
                   aq_ref, ak_ref, av_ref, ga_ref, bq_ref, bk_ref, bv_ref, gb_ref, mc_ref, p_ref,
                   bgd_ref):
    tm = x_ref.shape[0]
    h = _rms(x_ref[...]) * ng_ref[...]
    hb = (h * (1.0 + sc_ref[...]) + sh_ref[...]).astype(BF16)

    def proj(off, width=GW):
        return _dot(hb, w_ref[:, off:off + width])

    aq_ref[...] = (proj(C_AQ) * (NA_HEAD_DIM ** -0.5 * LOG2E)).astype(BF16)
    ak_ref[...] = proj(C_AK).astype(BF16)
    av_ref[...] = proj(C_AV).astype(BF16)
    ga_ref[...] = _silu(proj(C_AG)).astype(BF16)

    cos, sina, sinb = cos_ref[...], sina_ref[...], sinb_ref[...]
    qscale = (MLA_NOPE_DIM + MLA_ROPE_DIM) ** -0.5 * LOG2E
    qn = (_rms(proj(C_CQ)) * qng_ref[...]).astype(BF16)
    qu = _dot(qn, wuq_ref[...])
    for p in range(MLA_HEADS // 2):
        o = p * GW
        bq_ref[:, o:o + LANES] = (_rope128(qu[:, o:o + LANES], cos, sina, sinb) * qscale).astype(BF16)
        bq_ref[:, o + LANES:o + GW] = (qu[:, o + LANES:o + GW] * qscale).astype(BF16)
    kvn = (_rms(proj(C_CKV, MLA_KV_RANK)) * kvng_ref[...]).astype(BF16)
    kvu = _dot(kvn, wukv_ref[...])
    kr = _rope128(proj(C_KR, LANES), cos, sina, sinb).astype(BF16)
    for p in range(MLA_HEADS // 2):
        o = p * GW
        bk_ref[:, o:o + LANES] = kr
        bk_ref[:, o + LANES:o + GW] = kvu[:, p * LANES:(p + 1) * LANES].astype(BF16)
    bv_ref[...] = kvu[:, 2 * LANES:].astype(BF16)
    gb_ref[...] = _silu(proj(C_BG)).astype(BF16)

    u = _gelu(proj(C_CU))
    v = _gelu(proj(C_CV))
    gavg = gavg_ref[...]

    def group_mean(t):
        hi = t.astype(BF16)
        lo = (t - hi.astype(F32)).astype(BF16)
        return _dot(hi, gavg) + _dot(lo, gavg)

    cen = v - group_mean(v)
    vn = (cen * lax.rsqrt(group_mean(cen * cen) + NORM_EPS) * lng_ref[...]).astype(BF16)
    lane = lax.broadcasted_iota(jnp.int32, (1, GW), 1)
    gate_c = _silu(proj(C_CG))
    for n in range(tm // CM_CHUNK):
        rows = slice(n * CM_CHUNK, (n + 1) * CM_CHUNK)
        vc = vn[rows]
        s = bs_ref[...]
        for g in range(CM_GROUPS):
            in_g = (lane >= g * CM_GROUP_DIM) & (lane < (g + 1) * CM_GROUP_DIM)
            s = s + _dot(ws_ref[:, g * CM_CHUNK:(g + 1) * CM_CHUNK], jnp.where(in_g, vc, jnp.zeros_like(vc)))
        mc_ref[rows, :] = (u[rows] * s * gate_c[rows]).astype(BF16)

    p_ref[...] = (proj(C_DC) * proj(C_DH)).astype(BF16)
    bgd_ref[...] = (proj(C_DB) * _silu(proj(C_DG))).astype(BF16)


def _inproj(x, shift, scale, lw, tabs, tm):
    b, l, d = x.shape
    cos, sina, sinb = tabs
    full = lambda a: pl.BlockSpec(a.shape, lambda bi, i: (0,) * a.ndim)
    tok = lambda width: pl.BlockSpec((None, tm, width), lambda bi, i: (bi, i, 0))
    per_b = pl.BlockSpec((None, 1, d), lambda bi, i: (bi, 0, 0))
    pos = pl.BlockSpec((tm, LANES), lambda bi, i: (i, 0))
    widths = (GW, GW, GW, GW, 2 * GW, 2 * GW, GW, GW, GW, GW, GW)
    consts = (lw["norm_g"], lw["w_in"], lw["qn_g"], lw["w_uq"], lw["kvn_g"], lw["w_ukv"])
    consts2 = (lw["ln_g"], lw["gavg"], lw["ws"], lw["bs"])
    return pl.pallas_call(
        _inproj_kernel,
        grid=(b, l // tm),
        in_specs=[tok(d), per_b, per_b] + [full(a) for a in consts] + [pos, pos, pos]
                 + [full(a) for a in consts2],
        out_specs=[tok(w) for w in widths],
        out_shape=[jax.ShapeDtypeStruct((b, l, w), BF16) for w in widths],
        compiler_params=_cparams(2),
        name="inproj",
    )(x, shift, scale, *consts, cos, sina, sinb, *consts2)


def _lane_mask(ranges, width):
    lane = lax.broadcasted_iota(jnp.int32, (1, width), 1)
    m = None
    for lo, hi in ranges:
        t = (lane >= lo) & (lane < hi)
        m = t if m is None else (m | t)
    return m


def _softmax_probs(scores):
    m = scores[0].max(axis=-1, keepdims=True)
    for s in scores[1:]:
        m = jnp.maximum(m, s.max(axis=-1, keepdims=True))
    probs, l = [], None
    for s in scores:
        e = jnp.exp2(s - m)
        es = e.sum(axis=-1, keepdims=True)
        l = es if l is None else l + es
        probs.append(e.astype(BF16))
    return probs, l


def _weighted_values(probs, l, values):
    o = None
    for p, v in zip(probs, values):
        pv = _dot(p, v)
        o = pv if o is None else o + pv
    return o * (1.0 / l)


def _attend_blocks(n_blk, n_heads, scores_fn, values_fn, emit_fn, s_scr):
    def attend(scores, values):
        probs, l = _softmax_probs(scores)
        return _weighted_values(probs, l, values)

    if n_blk == 1:
        emit_fn(0, [attend(scores_fn(0, h), values_fn(0)) for h in range(n_heads)])
        return

    def stash(i):
        for ref, s in zip(s_scr, scores_fn(i, 0)):
            ref[...] = s

    def block(i):
        values = values_fn(i)
        outs = [attend([ref[...] for ref in s_scr], values)]
        outs += [attend(scores_fn(i, h), values) for h in range(1, n_heads)]
        emit_fn(i, outs)

    def body(i, carry):
        block(i)
        stash(i + 1)
        return carry

    stash(0)
    lax.fori_loop(0, n_blk - 1, body, 0)
    block(n_blk - 1)


def _pipeline_scratch(tq, piece_widths):
    return [pltpu.VMEM((tq, w), F32) for w in piece_widths]


def _merge_heads(outs):
    acc = outs[0]
    for hd in range(1, len(outs)):
        acc = jnp.where(_lane_mask([(hd * 64, hd * 64 + 64)], GW), outs[hd], acc)
    return acc


def _head_scores(q_ref, rows, k_refs, head_spec):
    grp, ranges = head_spec
    q = q_ref[rows, grp * GW:(grp + 1) * GW]
    qm = jnp.where(_lane_mask(ranges, GW), q, jnp.zeros_like(q))
    return [_dot_nt(qm, k[:, grp * GW:(grp + 1) * GW]) for k in k_refs]


def _attn_kernel(*refs, head_specs, n_src, tr):
    q_ref = refs[0]
    k_refs = refs[1:1 + n_src]
    v_refs = refs[1 + n_src:1 + 2 * n_src]
    gate_ref, o_ref = refs[1 + 2 * n_src], refs[2 + 2 * n_src]
    scr = refs[3 + 2 * n_src:]
    n_blk = q_ref.shape[0] // tr

    def rows(i):
        start = i * tr
        return pl.ds(start if isinstance(start, int) else pl.multiple_of(start, tr), tr)

    def emit(i, outs):
        o_ref[rows(i), :] = (_merge_heads(outs) * gate_ref[rows(i), :].astype(F32)).astype(BF16)

    _attend_blocks(n_blk, len(head_specs),
                   lambda i, h: _head_scores(q_ref, rows(i), k_refs, head_specs[h]),
                   lambda i: [v[...] for v in v_refs], emit, scr)


def _dense_attention(q, ks, vs, gate, head_specs, tq, tr):
    b, lq, qw = q.shape
    n_src = len(ks)
    kv_spec = lambda a: pl.BlockSpec((None,) + a.shape[1:], lambda bi, i: (bi, 0, 0))
    scratch = _pipeline_scratch(tr, [a.shape[1] for a in ks]) if tq > tr else []
    return pl.pallas_call(
        functools.partial(_attn_kernel, head_specs=head_specs, n_src=n_src, tr=tr),
        grid=(b, lq // tq),
        in_specs=[pl.BlockSpec((None, tq, qw), lambda bi, i: (bi, i, 0))]
                 + [kv_spec(a) for a in ks] + [kv_spec(a) for a in vs]
                 + [pl.BlockSpec((None, tq, GW), lambda bi, i: (bi, i, 0))],
        out_specs=pl.BlockSpec((None, tq, GW), lambda bi, i: (bi, i, 0)),
        out_shape=jax.ShapeDtypeStruct((b, lq, GW), BF16),
        scratch_shapes=scratch,
        compiler_params=_cparams(2),
        name="dense_attention",
    )(q, *ks, *vs, gate)


NA_HEAD_SPECS = tuple((0, ((h * 64, h * 64 + 64),)) for h in range(NA_HEADS))
MLA_HEAD_SPECS = tuple((h // 2, ((32 * (h % 2), 32 * (h % 2) + 32),
                                 (LANES + 64 * (h % 2), LANES + 64 * (h % 2) + 64)))
                       for h in range(MLA_HEADS))


def _na_kernel(q_ref, k0, k1, k2, k3, v0, v1, v2, v3, kc_ref, vc_ref, strip_ref, rowm_ref, gate_ref,
               o_ref, *scr):
    rb = pl.program_id(0)
    n_b, tq, _ = q_ref.shape
    kw = NA_KCHUNK_ROWS * GRID_W
    t_base = 12 + jnp.where(rb == 0, 0, jnp.where(rb == pl.num_programs(0) - 1, 8, 4))

    def scores_fn(i, hd):
        q = q_ref[i]
        qm = jnp.where(_lane_mask([(hd * 64, hd * 64 + 64)], GW), q, jnp.zeros_like(q))
        scores = []
        for j, k in enumerate((k0, k1, k2, k3)):
            start = pl.multiple_of((t_base - NA_KCHUNK_ROWS * j) * GRID_W, GRID_W)
            bias = strip_ref[hd, pl.ds(start, tq), :] + rowm_ref[:, j * kw:(j + 1) * kw]
            scores.append(_dot_nt(qm, k[i]) + bias)
        scores.append(_dot_nt(qm, kc_ref[i]))
        return scores

    def emit(i, outs):
        o_ref[i] = (_merge_heads(outs) * gate_ref[i].astype(F32)).astype(BF16)

    _attend_blocks(n_b, NA_HEADS, scores_fn,
                   lambda i: [v0[i], v1[i], v2[i], v3[i], vc_ref[i]], emit, scr)


def _na_tables(rpb, rows):
    n_h = rpb.shape[0]
    qc = np.arange(GRID_W)[:, None]
    kc = np.arange(GRID_W)[None, :]
    cs = np.clip(qc - NA_WIN_COLS // 2, 0, GRID_W - NA_WIN_COLS)
    col_ok = (kc >= cs) & (kc < cs + NA_WIN_COLS)
    ci = np.clip(kc - qc, -(NA_WIN_COLS - 1), NA_WIN_COLS - 1) + NA_WIN_COLS - 1
    onehot = (ci[None] == np.arange(2 * NA_WIN_COLS - 1)[:, None, None]).astype(np.float32)
    mc = jnp.einsum("hac,cqk->haqk", rpb.astype(F32), jnp.asarray(onehot),
                    precision=lax.Precision.HIGHEST)
    mc = jnp.where(jnp.asarray(col_ok)[None, None], mc * LOG2E, NEG_INF)
    n_ext = 31
    pad = (n_ext - (2 * NA_WIN_ROWS - 1)) // 2
    neg = jnp.full((n_h, pad, GRID_W, GRID_W), NEG_INF, F32)
    mc_ext = jnp.concatenate([neg, mc, neg], axis=1)
    n_strips = 28
    rev = mc_ext[:, ::-1]
    parts = [rev[:, 3 - i:3 - i + n_strips] for i in range(NA_KCHUNK_ROWS)]
    strips = jnp.stack(parts, axis=3)
    strips = strips.reshape(n_h, n_strips * GRID_W, NA_KCHUNK_ROWS * GRID_W)

    kr_win = min(NA_WIN_ROWS, rows)
    n_rb = rows // NA_QROWS
    qr = np.arange(rows)
    rs = np.clip(qr - kr_win // 2, 0, rows - kr_win)
    ks = np.clip(np.arange(n_rb) * NA_QROWS - NA_KROWS // 4, 0, rows - NA_KROWS)
    kr = ks[:, None] + np.arange(NA_KROWS)[None, :]
    rs_b = rs.reshape(n_rb, NA_QROWS)
    ok = (kr[:, None, :] >= rs_b[:, :, None]) & (kr[:, None, :] < rs_b[:, :, None] + kr_win)
    rowm = np.where(ok, 0.0, NEG_INF).astype(np.float32)
    rowm = np.broadcast_to(rowm[:, :, None, :, None], (n_rb, NA_QROWS, GRID_W, NA_KROWS, GRID_W))
    rowm = rowm.reshape(n_rb, NA_QROWS * GRID_W, NA_KROWS * GRID_W)
    return strips, jnp.asarray(rowm)


def _neighbourhood_attention(q, k, v, kc, vc, strips, rowm, gate):
    b, l, _ = q.shape
    tq = NA_QROWS * GRID_W
    kw = NA_KCHUNK_ROWS * GRID_W
    n_rb = l // tq
    n_kblk = l // kw

    n_b = NA_BATCH_PER_STEP
    assert b % n_b == 0

    def kv_spec(j):
        return pl.BlockSpec((n_b, kw, GW),
                            lambda rb, bi: (bi, jnp.clip(2 * rb - 1, 0, n_kblk - 4) + j, 0))

    tok = pl.BlockSpec((n_b, tq, GW), lambda rb, bi: (bi, rb, 0))
    ctx_spec = pl.BlockSpec((n_b,) + kc.shape[1:], lambda rb, bi: (bi, 0, 0))
    return pl.pallas_call(
        _na_kernel,
        grid=(n_rb, b // n_b),
        in_specs=[tok] + [kv_spec(j) for j in range(4)] * 2 + [ctx_spec, ctx_spec,
                  pl.BlockSpec(strips.shape, lambda rb, bi: (0, 0, 0)),
                  pl.BlockSpec((None,) + rowm.shape[1:], lambda rb, bi: (rb, 0, 0)),
                  tok],
        out_specs=tok,
        out_shape=jax.ShapeDtypeStruct((b, l, GW), BF16),
        scratch_shapes=_pipeline_scratch(tq, [kw] * (NA_KROWS // NA_KCHUNK_ROWS) + [kc.shape[1]]),
        compiler_params=_cparams(2),
        name="neighbourhood_attention",
    )(q, k, k, k, k, v, v, v, v, kc, vc, strips, rowm, gate)


def _outproj_kernel(oa_ref, ob_ref, mc_ref, p_ref, pprev_ref, pnext_ref, bgd_ref, scw_ref, w_ref,
                    g_ref, x_ref, fg_ref, o_ref, *, final_norm):
    i = pl.program_id(1)
    tm = x_ref.shape[0]
    p = p_ref[...].astype(F32)
    row = lax.broadcasted_iota(jnp.int32, (tm, 1), 0)
    prev_row = jnp.where(i == 0, 0.0, pprev_ref[HALO - 1:HALO, :].astype(F32))
    next_row = jnp.where(i == pl.num_programs(1) - 1, 0.0, pnext_ref[0:1, :].astype(F32))
    p_dn = jnp.where(row == 0, prev_row, pltpu.roll(p, 1, 0))
    p_up = jnp.where(row == tm - 1, next_row, pltpu.roll(p, tm - 1, 0))
    y = p_dn * scw_ref[0:1, :] + p * scw_ref[1:2, :] + p_up * scw_ref[2:3, :]
    md = (bgd_ref[...].astype(F32) * y).astype(BF16)
    acc = (_dot(oa_ref[...], w_ref[0:GW, :]) + _dot(ob_ref[...], w_ref[GW:2 * GW, :])
           + _dot(mc_ref[...], w_ref[2 * GW:3 * GW, :]) + _dot(md, w_ref[3 * GW:4 * GW, :]))
    xn = x_ref[...] + g_ref[...] * acc
    if final_norm:
        xn = _rms(xn) * fg_ref[...]
    o_ref[...] = xn


def _outproj(oa, ob, mc, p, bgd, sc_w, w_out, g, x, final_g, tm, final_norm):
    b, l, d = x.shape
    n_halo = l // HALO
    per_tile = tm // HALO
    tok = lambda width: pl.BlockSpec((None, tm, width), lambda bi, i: (bi, i, 0))
    full = lambda a: pl.BlockSpec(a.shape, lambda bi, i: (0,) * a.ndim)
    prev = pl.BlockSpec((None, HALO, GW), lambda bi, i: (bi, jnp.maximum(i * per_tile - 1, 0), 0))
    nxt = pl.BlockSpec((None, HALO, GW),
                       lambda bi, i: (bi, jnp.minimum((i + 1) * per_tile, n_halo - 1), 0))
    return pl.pallas_call(
        functools.partial(_outproj_kernel, final_norm=final_norm),
        grid=(b, l // tm),
        in_specs=[tok(GW), tok(GW), tok(GW), tok(GW), prev, nxt, tok(GW), full(sc_w), full(w_out),
                  pl.BlockSpec((None, 1, d), lambda bi, i: (bi, 0, 0)), tok(d), full(final_g)],
        out_specs=tok(d),
        out_shape=jax.ShapeDtypeStruct((b, l, d), F32),
        compiler_params=_cparams(2),
        name="outproj",
    )(oa, ob, mc, p, p, p, bgd, sc_w, w_out, g, x, final_g)


def _prep_w_in(w):
    kr = w[:, 1408:1440]
    z = jnp.zeros((w.shape[0], 64), w.dtype)
    return jnp.concatenate([w[:, :1440], kr, z, w[:, 1440:]], axis=1).astype(BF16)


def _prep_w_uq(w):
    hd = MLA_NOPE_DIM + MLA_ROPE_DIM
    z = jnp.zeros((w.shape[0], 64), w.dtype)
    cols = []
    for p in range(MLA_HEADS // 2):
        h0, h1 = 2 * p, 2 * p + 1
        cols += [w[:, hd * h0 + MLA_NOPE_DIM:hd * (h0 + 1)], w[:, hd * h1 + MLA_NOPE_DIM:hd * (h1 + 1)], z,
                 w[:, hd * h0:hd * h0 + MLA_NOPE_DIM], w[:, hd * h1:hd * h1 + MLA_NOPE_DIM]]
    return jnp.concatenate(cols, axis=1).astype(BF16)


def _prep_w_ukv(w):
    hd = MLA_NOPE_DIM + MLA_V_DIM
    kn = [w[:, hd * h:hd * h + MLA_NOPE_DIM] for h in range(MLA_HEADS)]
    vv = [w[:, hd * h + MLA_NOPE_DIM:hd * (h + 1)] for h in range(MLA_HEADS)]
    return jnp.concatenate(kn + vv, axis=1).astype(BF16)


def _rope_tables(length):
    nf = MLA_ROPE_DIM // 4
    t = jnp.arange(length)
    row = (t // GRID_W).astype(F32)
    col = (t % GRID_W).astype(F32)
    inv = ROPE_BASE ** (-jnp.arange(nf, dtype=F32) / nf)
    ang = jnp.stack([row[:, None] * inv, col[:, None] * inv], axis=1)
    cos = jnp.cos(ang)
    sin = jnp.sin(ang)
    zero = jnp.zeros_like(sin)
    cos32 = jnp.stack([cos, cos], axis=2).reshape(length, MLA_ROPE_DIM)
    sina32 = jnp.stack([-sin, zero], axis=2).reshape(length, MLA_ROPE_DIM)
    sinb32 = jnp.stack([zero, sin], axis=2).reshape(length, MLA_ROPE_DIM)
    pad1 = jnp.ones((length, 64), F32)
    pad0 = jnp.zeros((length, 64), F32)
    return (jnp.concatenate([cos32, cos32, pad1], axis=1),
            jnp.concatenate([sina32, sina32, pad0], axis=1),
            jnp.concatenate([sinb32, sinb32, pad0], axis=1))


def _identity_rope_tables(length):
    return (jnp.ones((length, LANES), F32), jnp.zeros((length, LANES), F32),
            jnp.zeros((length, LANES), F32))


def kernel(x, c, ctx, c_ctx, norm_g, w_mod, b_mod, w_in, na_rpb, mla_qn_g, mla_w_uq, mla_kvn_g,
           mla_w_ukv, cm_ln_g, cm_w_s, cm_b_s, sc_w, w_out, final_g):
    b, l, d = x.shape
    lc = ctx.shape[1]
    depth = w_in.shape[0]
    rows = l // GRID_W
    assert l % (NA_QROWS * GRID_W) == 0 and rows >= NA_KROWS and lc % CM_CHUNK == 0
    assert w_in.shape[2] == 3488 and d == 4 * GW

    n_mod = -(-(b + 1) // 8) * 8
    cc = jnp.zeros((n_mod, d), F32).at[:b].set(c).at[b].set(c_ctx)
    mod = _modulation(cc, w_mod, b_mod)

    rope_x = _rope_tables(l)
    rope_c = _identity_rope_tables(lc)
    gavg = jnp.asarray(np.kron(np.eye(CM_GROUPS), np.full((CM_GROUP_DIM, CM_GROUP_DIM),
                                                          1.0 / CM_GROUP_DIM)), BF16)
    fg =final_g.reshape(1, d)

    for layer in range(depth):
        last = layer == depth - 1
        lw = {
            "norm_g": norm_g[layer].reshape(1, d),
            "w_in": _prep_w_in(w_in[layer]),
            "qn_g": mla_qn_g[layer].reshape(1, MLA_Q_RANK),
            "w_uq": _prep_w_uq(mla_w_uq[layer]),
            "kvn_g": mla_kvn_g[layer].reshape(1, MLA_KV_RANK),
            "w_ukv": _prep_w_ukv(mla_w_ukv[layer]),
            "ln_g": cm_ln_g[layer].reshape(1, GW),
            "gavg": gavg,
            "ws": jnp.transpose(cm_w_s[layer], (1, 0, 2)).reshape(CM_CHUNK, CM_GROUPS * CM_CHUNK).astype(BF16),
            "bs": jnp.repeat(cm_b_s[layer].T, CM_GROUP_DIM, axis=1),
        }
        w_out_b = w_out[layer].astype(BF16)
        m = mod[layer]
        sh_x, sc_x, g_x = (m[:b, i * d:(i + 1) * d].reshape(b, 1, d) for i in range(3))
        sh_c, sc_c, g_c = (jnp.broadcast_to(m[b, i * d:(i + 1) * d], (b, 1, d)) for i in range(3))

        (aq, ak, av, ga, bq, bk, bv, gb, mc, pp, bgd) = _inproj(x, sh_x, sc_x, lw, rope_x, tm=512)
        (aqc, akc, avc, gac, bqc, bkc, bvc, gbc, mcc, ppc, bgdc) = _inproj(ctx, sh_c, sc_c, lw, rope_c, tm=lc)

        strips, rowm = _na_tables(na_rpb[layer], rows)
        out_a = _neighbourhood_attention(aq, ak, av, akc, avc, strips, rowm, ga)
        out_b = _dense_attention(bq, [bk, bkc], [bv, bvc], gb, MLA_HEAD_SPECS, tq=l, tr=512)
        x_new = _outproj(out_a, out_b, mc, pp, bgd, sc_w[layer], w_out_b, g_x, x, fg, tm=512,
                         final_norm=last)
        if not last:
            ctx_a = _dense_attention(aqc, [akc], [avc], gac, NA_HEAD_SPECS, tq=lc, tr=lc)
            ctx_b = _dense_attention(bqc, [bkc], [bvc], gbc, MLA_HEAD_SPECS, tq=lc, tr=lc)
            ctx = _outproj(ctx_a, ctx_b, mcc, ppc, bgdc, sc_w[layer], w_out_b, g_c, ctx, fg, tm=lc,
                           final_norm=False)
        x = x_new
    return x
```

```python
import functools

import numpy as np
import jax
import jax.numpy as jnp
from jax import lax
from jax.experimental import pallas as pl
from jax.experimental.pallas import tpu as pltpu

F32 = jnp.float32
BF16 = jnp.bfloat16

GRID_W = 64
NA_HEADS = 4
NA_HEAD_DIM = 64
NA_WIN_ROWS = 8
NA_WIN_COLS = 16
MLA_HEADS = 4
MLA_NOPE_DIM = 64
MLA_ROPE_DIM = 32
MLA_V_DIM = 64
MLA_Q_RANK = 256
MLA_KV_RANK = 128
CM_GROUPS = 4
CM_GROUP_DIM = 64
CM_CHUNK = 128
SC_TAPS = 3
ROPE_BASE = 10000.0
NORM_EPS = 1e-6
NEG_INF = -1e9
LOG2E = 1.4426950408889634

GW = 256
LANES = 128
NA_QROWS = 4
NA_KCHUNK_ROWS = 4
NA_KCHUNKS = 3
NA_BATCH_PER_STEP = 8
HALO = 16
VMEM_LIMIT = 56 * 1024 * 1024

C_AQ, C_AK, C_AV, C_AG = 0, 256, 512, 768
C_CQ, C_CKV, C_KR, C_BG = 1024, 1280, 1408, 1536
C_CU, C_CV, C_CG = 1792, 2048, 2304
C_DB, C_DC, C_DH, C_DG = 2560, 2816, 3072, 3328
D_IN_P = 3584


def _silu(x):
    return x / (1.0 + jnp.exp(-x))


def _gelu(x):
    return 0.5 * x * (1.0 + lax.erf(x * 0.7071067811865476))


def _rms(x):
    return x * lax.rsqrt(jnp.mean(x * x, axis=-1, keepdims=True) + NORM_EPS)


def _dot(a, b):
    return jnp.dot(a, b, preferred_element_type=F32)


def _dot_nt(a, b):
    return lax.dot_general(a, b, (((1,), (1,)), ((), ())), preferred_element_type=F32)


def _cparams(n_grid):
    return pltpu.CompilerParams(dimension_semantics=("parallel",) * n_grid,
                                vmem_limit_bytes=VMEM_LIMIT)


def _mod_kernel(c_ref, w_ref, b_ref, o_ref):
    s = _silu(c_ref[...]).astype(BF16)
    o_ref[...] = _dot(s, w_ref[...].astype(BF16)) + b_ref[...]


def _modulation(cc, w_mod, b_mod):
    depth, d, n = w_mod.shape
    r = cc.shape[0]
    tn = 768
    return pl.pallas_call(
        _mod_kernel,
        grid=(depth, n // tn),
        in_specs=[pl.BlockSpec((r, d), lambda l, j: (0, 0)),
                  pl.BlockSpec((None, d, tn), lambda l, j: (l, 0, j)),
                  pl.BlockSpec((None, 1, tn), lambda l, j: (l, 0, j))],
        out_specs=pl.BlockSpec((None, r, tn), lambda l, j: (l, 0, j)),
        out_shape=jax.ShapeDtypeStruct((depth, r, n), F32),
        compiler_params=_cparams(2),
        name="modulation",
    )(cc, w_mod, b_mod.reshape(depth, 1, n))


def _rope128(x, cos, sina, sinb):
    return x * cos + pltpu.roll(x, LANES - 8, 1) * sina + pltpu.roll(x, 8, 1) * sinb


def _inproj_kernel(x_ref, sh_ref, sc_ref, ng_ref, w_ref, qng_ref, wuq_ref, kvng_ref, wukv_ref,
                   cos_ref, sina_ref, sinb_ref, lng_ref, gavg_ref, ws_ref, bs_ref,
                   aq_ref, ak_ref, av_ref, ga_ref, bq_ref, bk_ref, bv_ref, gb_ref, mc_ref, p_ref,
                   bgd_ref):
    tm = x_ref.shape[0]
    h = _rms(x_ref[...]) * ng_ref[...]
    hb = (h * (1.0 + sc_ref[...]) + sh_ref[...]).astype(BF16)

    def proj(off, width=GW):
        return _dot(hb, w_ref[:, off:off + width])

    cos, sina, sinb = cos_ref[...], sina_ref[...], sinb_ref[...]
    qscale = (MLA_NOPE_DIM + MLA_ROPE_DIM) ** -0.5 * LOG2E
    gavg = gavg_ref[...]
    lane = lax.broadcasted_iota(jnp.int32, (1, GW), 1)

    def split_bf16(t):
        hi = t.astype(BF16)
        return hi, (t - hi.astype(F32)).astype(BF16)

    qn = (_rms(proj(C_CQ)) * qng_ref[...]).astype(BF16)
    kvn = (_rms(proj(C_CKV, MLA_KV_RANK)) * kvng_ref[...]).astype(BF16)
    v = _gelu(proj(C_CV))
    v_hi, v_lo = split_bf16(v)

    aq_ref[...] = (proj(C_AQ) * (NA_HEAD_DIM ** -0.5 * LOG2E)).astype(BF16)
    ak_ref[...] = proj(C_AK).astype(BF16)
    av_ref[...] = proj(C_AV).astype(BF16)
    ga_ref[...] = _silu(proj(C_AG)).astype(BF16)

    qu = _dot(qn, wuq_ref[...])
    kvu = _dot(kvn, wukv_ref[...])
    cen = v - (_dot(v_hi, gavg) + _dot(v_lo, gavg))
    sq_hi, sq_lo = split_bf16(cen * cen)
    for p in range(MLA_HEADS // 2):
        o = p * GW
        bq_ref[:, o:o + LANES] = (_rope128(qu[:, o:o + LANES], cos, sina, sinb) * qscale).astype(BF16)
        bq_ref[:, o + LANES:o + GW] = (qu[:, o + LANES:o + GW] * qscale).astype(BF16)
    kr = _rope128(proj(C_KR, LANES), cos, sina, sinb).astype(BF16)
    for p in range(MLA_HEADS // 2):
        o = p * GW
        bk_ref[:, o:o + LANES] = kr
        bk_ref[:, o + LANES:o + GW] = kvu[:, p * LANES:(p + 1) * LANES].astype(BF16)
    bv_ref[...] = kvu[:, 2 * LANES:].astype(BF16)

    p_ref[...] = (proj(C_DC) * proj(C_DH)).astype(BF16)
    bgd_ref[...] = (proj(C_DB) * _silu(proj(C_DG))).astype(BF16)

    var = _dot(sq_hi, gavg) + _dot(sq_lo, gavg)
    vn = (cen * lax.rsqrt(var + NORM_EPS) * lng_ref[...]).astype(BF16)
    gb_ref[...] = _silu(proj(C_BG)).astype(BF16)
    u = _gelu(proj(C_CU))
    gate_c = _silu(proj(C_CG))

    for n in range(tm // CM_CHUNK):
        rows = slice(n * CM_CHUNK, (n + 1) * CM_CHUNK)
        vc = vn[rows]
        s = bs_ref[...]
        for g in range(CM_GROUPS):
            in_g = (lane >= g * CM_GROUP_DIM) & (lane < (g + 1) * CM_GROUP_DIM)
            s = s + _dot(ws_ref[:, g * CM_CHUNK:(g + 1) * CM_CHUNK], jnp.where(in_g, vc, jnp.zeros_like(vc)))
        mc_ref[rows, :] = (u[rows] * s * gate_c[rows]).astype(BF16)


def _inproj(x, shift, scale, lw, tabs, tm):
    b, l, d = x.shape
    cos, sina, sinb = tabs
    full = lambda a: pl.BlockSpec(a.shape, lambda bi, i: (0,) * a.ndim)
    tok = lambda width: pl.BlockSpec((None, tm, width), lambda bi, i: (bi, i, 0))
    per_b = pl.BlockSpec((None, 1, d), lambda bi, i: (bi, 0, 0))
    pos = pl.BlockSpec((tm, LANES), lambda bi, i: (i, 0))
    widths = (GW, GW, GW, GW, 2 * GW, 2 * GW, GW, GW, GW, GW, GW)
    consts = (lw["norm_g"], lw["w_in"], lw["qn_g"], lw["w_uq"], lw["kvn_g"], lw["w_ukv"])
    consts2 = (lw["ln_g"], lw["gavg"], lw["ws"], lw["bs"])
    return pl.pallas_call(
        _inproj_kernel,
        grid=(b, l // tm),
        in_specs=[tok(d), per_b, per_b] + [full(a) for a in consts] + [pos, pos, pos]
                 + [full(a) for a in consts2],
        out_specs=[tok(w) for w in widths],
        out_shape=[jax.ShapeDtypeStruct((b, l, w), BF16) for w in widths],
        compiler_params=_cparams(2),
        name="inproj",
    )(x, shift, scale, *consts, cos, sina, sinb, *consts2)


def _lane_mask(ranges, width):
    lane = lax.broadcasted_iota(jnp.int32, (1, width), 1)
    m = None
    for lo, hi in ranges:
        t = (lane >= lo) & (lane < hi)
        m = t if m is None else (m | t)
    return m


def _softmax_probs(scores):
    def lane_reduce(pieces, combine, reduce):
        by_width = {}
        for t in pieces:
            w = t.shape[-1]
            by_width[w] = t if w not in by_width else combine(by_width[w], t)
        out = None
        for t in by_width.values():
            r = reduce(t, axis=-1, keepdims=True)
            out = r if out is None else combine(out, r)
        return out

    m = lane_reduce(scores, jnp.maximum, jnp.max)
    exps = [jnp.exp2(s - m) for s in scores]
    l = lane_reduce(exps, jnp.add, jnp.sum)
    return [e.astype(BF16) for e in exps], l


def _weighted_values(probs, l, values):
    o = None
    for p, v in zip(probs, values):
        pv = _dot(p, v)
        o = pv if o is None else o + pv
    return o * (1.0 / l)


def _attend_blocks(n_blk, n_heads, scores_fn, values_fn, emit_fn, scr):
    def block(i, has_next, scores):
        values = values_fn(i)
        units = [(i, h) for h in range(n_heads)] + ([(i + 1, 0)] if has_next else [])
        outs = []
        for u in range(n_heads):
            nxt = scores_fn(*units[u + 1]) if u + 1 < len(units) else None
            probs, l = _softmax_probs(scores)
            outs.append(_weighted_values(probs, l, values))
            scores = nxt
        emit_fn(i, outs)
        return scores

    scores = scores_fn(0, 0)
    if n_blk == 1:
        block(0, False, scores)
        return

    def stash(scores):
        for ref, s in zip(scr, scores):
            ref[...] = s

    def body(i, carry):
        stash(block(i, True, [ref[...] for ref in scr]))
        return carry

    stash(scores)
    lax.fori_loop(0, n_blk - 1, body, 0)
    block(n_blk - 1, False, [ref[...] for ref in scr])


def _pipeline_scratch(tq, piece_widths):
    return [pltpu.VMEM((tq, w), F32) for w in piece_widths]


def _merge_heads(outs):
    acc = outs[0]
    for hd in range(1, len(outs)):
        acc = jnp.where(_lane_mask([(hd * 64, hd * 64 + 64)], GW), outs[hd], acc)
    return acc


def _head_scores(q_ref, rows, k_refs, head_spec):
    grp, ranges = head_spec
    q = q_ref[rows, grp * GW:(grp + 1) * GW]
    qm = jnp.where(_lane_mask(ranges, GW), q, jnp.zeros_like(q))
    return [_dot_nt(qm, k[:, grp * GW:(grp + 1) * GW]) for k in k_refs]


def _attn_kernel(*refs, head_specs, n_src, tr):
    q_ref = refs[0]
    k_refs = refs[1:1 + n_src]
    v_refs = refs[1 + n_src:1 + 2 * n_src]
    gate_ref, o_ref = refs[1 + 2 * n_src], refs[2 + 2 * n_src]
    scr = refs[3 + 2 * n_src:]
    n_blk = q_ref.shape[0] // tr

    def rows(i):
        start = i * tr
        return pl.ds(start if isinstance(start, int) else pl.multiple_of(start, tr), tr)

    def emit(i, outs):
        o_ref[rows(i), :] = (_merge_heads(outs) * gate_ref[rows(i), :].astype(F32)).astype(BF16)

    _attend_blocks(n_blk, len(head_specs),
                   lambda i, h: _head_scores(q_ref, rows(i), k_refs, head_specs[h]),
                   lambda i: [v[...] for v in v_refs], emit, scr)


def _dense_attention(q, ks, vs, gate, head_specs, tq, tr):
    b, lq, qw = q.shape
    n_src = len(ks)
    kv_spec = lambda a: pl.BlockSpec((None,) + a.shape[1:], lambda bi, i: (bi, 0, 0))
    scratch = _pipeline_scratch(tr, [a.shape[1] for a in ks]) if tq > tr else []
    return pl.pallas_call(
        functools.partial(_attn_kernel, head_specs=head_specs, n_src=n_src, tr=tr),
        grid=(b, lq // tq),
        in_specs=[pl.BlockSpec((None, tq, qw), lambda bi, i: (bi, i, 0))]
                 + [kv_spec(a) for a in ks] + [kv_spec(a) for a in vs]
                 + [pl.BlockSpec((None, tq, GW), lambda bi, i: (bi, i, 0))],
        out_specs=pl.BlockSpec((None, tq, GW), lambda bi, i: (bi, i, 0)),
        out_shape=jax.ShapeDtypeStruct((b, lq, GW), BF16),
        scratch_shapes=scratch,
        compiler_params=_cparams(2),
        name="dense_attention",
    )(q, *ks, *vs, gate)


NA_HEAD_SPECS = tuple((0, ((h * 64, h * 64 + 64),)) for h in range(NA_HEADS))
MLA_HEAD_SPECS = tuple((h // 2, ((32 * (h % 2), 32 * (h % 2) + 32),
                                 (LANES + 64 * (h % 2), LANES + 64 * (h % 2) + 64)))
                       for h in range(MLA_HEADS))


def _na_kernel(q_ref, k0, k1, k2, v0, v1, v2, kc_ref, vc_ref, strip_ref, win_ref, gate_ref,
               o_ref, bias_scr, *scr):
    qb = pl.program_id(0)
    n_b, tq, _ = q_ref.shape
    kw = NA_KCHUNK_ROWS * GRID_W
    t_base = 12 + jnp.where(qb == 0, 0, jnp.where(qb == pl.num_programs(0) - 1, 8, 4))
    for hd in range(NA_HEADS):
        for j in range(NA_KCHUNKS):
            start = pl.multiple_of((t_base - NA_KCHUNK_ROWS * j) * GRID_W, GRID_W)
            cols = slice(j * kw, (j + 1) * kw)
            bias_scr[hd, :, cols] = strip_ref[hd, pl.ds(start, tq), :] + win_ref[:, cols]

    def scores_fn(i, hd):
        q = q_ref[i]
        qm = jnp.where(_lane_mask([(hd * 64, hd * 64 + 64)], GW), q, jnp.zeros_like(q))
        scores = [_dot_nt(qm, k[i]) + bias_scr[hd, :, j * kw:(j + 1) * kw]
                  for j, k in enumerate((k0, k1, k2))]
        scores.append(_dot_nt(qm, kc_ref[i]))
        return scores

    def emit(i, outs):
        o_ref[i] = (_merge_heads(outs) * gate_ref[i].astype(F32)).astype(BF16)

    _attend_blocks(n_b, NA_HEADS, scores_fn,
                   lambda i: [v0[i], v1[i], v2[i], vc_ref[i]], emit, scr)


def _na_tables(rpb, rows):
    n_h = rpb.shape[0]
    n_dr, n_dc = 2 * NA_WIN_ROWS - 1, 2 * NA_WIN_COLS - 1
    kw = NA_KCHUNK_ROWS * GRID_W
    qc = np.arange(GRID_W)[:, None]
    kc = np.arange(GRID_W)[None, :]
    cs = np.clip(qc - NA_WIN_COLS // 2, 0, GRID_W - NA_WIN_COLS)
    col_ok = (kc >= cs) & (kc < cs + NA_WIN_COLS)
    ci = np.clip(kc - qc, -(NA_WIN_COLS - 1), NA_WIN_COLS - 1) + NA_WIN_COLS - 1
    sel_c = (ci[None] == np.arange(n_dc)[:, None, None]).astype(np.float32)
    mc = jnp.einsum("hac,cqk->haqk", rpb.astype(F32), jnp.asarray(sel_c),
                    precision=lax.Precision.HIGHEST) * LOG2E
    mc = jnp.tile(mc, (1, 1, 1, NA_KCHUNK_ROWS))
    n_strips = 28
    dr = 12 - np.arange(n_strips)[:, None] + np.arange(kw)[None, :] // GRID_W
    sel_a = (dr[None] == (np.arange(n_dr) - (NA_WIN_ROWS - 1))[:, None, None]).astype(np.float32)
    strips = jnp.sum(jnp.asarray(sel_a)[None, :, :, None, :] * mc[:, :, None, :, :], axis=1)
    strips = strips.reshape(n_h, n_strips * GRID_W, kw)

    kr_win = min(NA_WIN_ROWS, rows)
    n_qb = rows // NA_QROWS
    n_krows = NA_KCHUNKS * NA_KCHUNK_ROWS
    rs = np.clip(np.arange(rows) - kr_win // 2, 0, rows - kr_win).reshape(n_qb, NA_QROWS)
    c0 = np.clip(np.arange(n_qb) - 1, 0, rows // NA_KCHUNK_ROWS - NA_KCHUNKS)
    kr = NA_KCHUNK_ROWS * c0[:, None] + np.arange(n_krows)[None, :]
    row_ok = (kr[:, None, :] >= rs[:, :, None]) & (kr[:, None, :] < rs[:, :, None] + kr_win)
    ok = row_ok[:, :, None, :, None] & col_ok[None, None, :, None, :]
    win = np.where(ok, 0.0, NEG_INF).astype(np.float32).reshape(n_qb, NA_QROWS * GRID_W, n_krows * GRID_W)
    return strips, jnp.asarray(win)


def _neighbourhood_attention(q, k, v, kc, vc, strips, win, gate):
    b, l, _ = q.shape
    tq = NA_QROWS * GRID_W
    kw = NA_KCHUNK_ROWS * GRID_W
    n_qb = l // tq
    n_kblk = l // kw
    n_b = NA_BATCH_PER_STEP
    assert b % n_b == 0

    def kv_spec(j):
        return pl.BlockSpec((n_b, kw, GW),
                            lambda qb, bi: (bi, jnp.clip(qb - 1, 0, n_kblk - NA_KCHUNKS) + j, 0))

    tok = pl.BlockSpec((n_b, tq, GW), lambda qb, bi: (bi, qb, 0))
    ctx_spec = pl.BlockSpec((n_b,) + kc.shape[1:], lambda qb, bi: (bi, 0, 0))
    return pl.pallas_call(
        _na_kernel,
        grid=(n_qb, b // n_b),
        in_specs=[tok] + [kv_spec(j) for j in range(NA_KCHUNKS)] * 2 + [ctx_spec, ctx_spec,
                  pl.BlockSpec(strips.shape, lambda qb, bi: (0, 0, 0), pipeline_mode=pl.Buffered(1)),
                  pl.BlockSpec((None,) + win.shape[1:], lambda qb, bi: (qb, 0, 0)),
                  tok],
        out_specs=tok,
        out_shape=jax.ShapeDtypeStruct((b, l, GW), BF16),
        scratch_shapes=[pltpu.VMEM((NA_HEADS, tq, NA_KCHUNKS * kw), F32)]
                       + _pipeline_scratch(tq, [kw] * NA_KCHUNKS + [kc.shape[1]]),
        compiler_params=_cparams(2),
        name="neighbourhood_attention",
    )(q, *([k] * NA_KCHUNKS), *([v] * NA_KCHUNKS), kc, vc, strips, win, gate)


def _outproj_kernel(oa_ref, ob_ref, mc_ref, p_ref, pprev_ref, pnext_ref, bgd_ref, scw_ref, w_ref,
                    g_ref, x_ref, fg_ref, o_ref, *, final_norm):
    i = pl.program_id(1)
    tm = x_ref.shape[0]
    p = p_ref[...].astype(F32)
    row = lax.broadcasted_iota(jnp.int32, (tm, 1), 0)
    prev_row = jnp.where(i == 0, 0.0, pprev_ref[HALO - 1:HALO, :].astype(F32))
    next_row = jnp.where(i == pl.num_programs(1) - 1, 0.0, pnext_ref[0:1, :].astype(F32))
    p_dn = jnp.where(row == 0, prev_row, pltpu.roll(p, 1, 0))
    p_up = jnp.where(row == tm - 1, next_row, pltpu.roll(p, tm - 1, 0))
    y = p_dn * scw_ref[0:1, :] + p * scw_ref[1:2, :] + p_up * scw_ref[2:3, :]
    md = (bgd_ref[...].astype(F32) * y).astype(BF16)
    acc = (_dot(oa_ref[...], w_ref[0:GW, :]) + _dot(ob_ref[...], w_ref[GW:2 * GW, :])
           + _dot(mc_ref[...], w_ref[2 * GW:3 * GW, :]) + _dot(md, w_ref[3 * GW:4 * GW, :]))
    xn = x_ref[...] + g_ref[...] * acc
    if final_norm:
        xn = _rms(xn) * fg_ref[...]
    o_ref[...] = xn


def _outproj(oa, ob, mc, p, bgd, sc_w, w_out, g, x, final_g, tm, final_norm):
    b, l, d = x.shape
    n_halo = l // HALO
    per_tile = tm // HALO
    tok = lambda width: pl.BlockSpec((None, tm, width), lambda bi, i: (bi, i, 0))
    full = lambda a: pl.BlockSpec(a.shape, lambda bi, i: (0,) * a.ndim)
    prev = pl.BlockSpec((None, HALO, GW), lambda bi, i: (bi, jnp.maximum(i * per_tile - 1, 0), 0))
    nxt = pl.BlockSpec((None, HALO, GW),
                       lambda bi, i: (bi, jnp.minimum((i + 1) * per_tile, n_halo - 1), 0))
    return pl.pallas_call(
        functools.partial(_outproj_kernel, final_norm=final_norm),
        grid=(b, l // tm),
        in_specs=[tok(GW), tok(GW), tok(GW), tok(GW), prev, nxt, tok(GW), full(sc_w), full(w_out),
                  pl.BlockSpec((None, 1, d), lambda bi, i: (bi, 0, 0)), tok(d), full(final_g)],
        out_specs=tok(d),
        out_shape=jax.ShapeDtypeStruct((b, l, d), F32),
        compiler_params=_cparams(2),
        name="outproj",
    )(oa, ob, mc, p, p, p, bgd, sc_w, w_out, g, x, final_g)


def _prep_w_in(w):
    kr = w[:, 1408:1440]
    z = jnp.zeros((w.shape[0], 64), w.dtype)
    return jnp.concatenate([w[:, :1440], kr, z, w[:, 1440:]], axis=1).astype(BF16)


def _prep_w_uq(w):
    hd = MLA_NOPE_DIM + MLA_ROPE_DIM
    z = jnp.zeros((w.shape[0], 64), w.dtype)
    cols = []
    for p in range(MLA_HEADS // 2):
        h0, h1 = 2 * p, 2 * p + 1
        cols += [w[:, hd * h0 + MLA_NOPE_DIM:hd * (h0 + 1)], w[:, hd * h1 + MLA_NOPE_DIM:hd * (h1 + 1)], z,
                 w[:, hd * h0:hd * h0 + MLA_NOPE_DIM], w[:, hd * h1:hd * h1 + MLA_NOPE_DIM]]
    return jnp.concatenate(cols, axis=1).astype(BF16)


def _prep_w_ukv(w):
    hd = MLA_NOPE_DIM + MLA_V_DIM
    kn = [w[:, hd * h:hd * h + MLA_NOPE_DIM] for h in range(MLA_HEADS)]
    vv = [w[:, hd * h + MLA_NOPE_DIM:hd * (h + 1)] for h in range(MLA_HEADS)]
    return jnp.concatenate(kn + vv, axis=1).astype(BF16)


def _rope_tables(length):
    nf = MLA_ROPE_DIM // 4
    t = jnp.arange(length)
    row = (t // GRID_W).astype(F32)
    col = (t % GRID_W).astype(F32)
    inv = ROPE_BASE ** (-jnp.arange(nf, dtype=F32) / nf)
    ang = jnp.stack([row[:, None] * inv, col[:, None] * inv], axis=1)
    cos = jnp.cos(ang)
    sin = jnp.sin(ang)
    zero = jnp.zeros_like(sin)
    cos32 = jnp.stack([cos, cos], axis=2).reshape(length, MLA_ROPE_DIM)
    sina32 = jnp.stack([-sin, zero], axis=2).reshape(length, MLA_ROPE_DIM)
    sinb32 = jnp.stack([zero, sin], axis=2).reshape(length, MLA_ROPE_DIM)
    pad1 = jnp.ones((length, 64), F32)
    pad0 = jnp.zeros((length, 64), F32)
    return (jnp.concatenate([cos32, cos32, pad1], axis=1),
            jnp.concatenate([sina32, sina32, pad0], axis=1),
            jnp.concatenate([sinb32, sinb32, pad0], axis=1))


def _identity_rope_tables(length):
    return (jnp.ones((length, LANES), F32), jnp.zeros((length, LANES), F32),
            jnp.zeros((length, LANES), F32))


def kernel(x, c, ctx, c_ctx, norm_g, w_mod, b_mod, w_in, na_rpb, mla_qn_g, mla_w_uq, mla_kvn_g,
           mla_w_ukv, cm_ln_g, cm_w_s, cm_b_s, sc_w, w_out, final_g):
    b, l, d = x.shape
    lc = ctx.shape[1]
    depth = w_in.shape[0]
    rows = l // GRID_W
    assert l % (NA_QROWS * GRID_W) == 0 and rows >= NA_KCHUNKS * NA_KCHUNK_ROWS and lc % CM_CHUNK == 0
    assert w_in.shape[2] == 3488 and d == 4 * GW

    n_mod = -(-(b + 1) // 8) * 8
    cc = jnp.zeros((n_mod, d), F32).at[:b].set(c).at[b].set(c_ctx)
    mod = _modulation(cc, w_mod, b_mod)

    rope_x = _rope_tables(l)
    rope_c = _identity_rope_tables(lc)
    gavg = jnp.asarray(np.kron(np.eye(CM_GROUPS), np.full((CM_GROUP_DIM, CM_GROUP_DIM),
                                                          1.0 / CM_GROUP_DIM)), BF16)
    fg =final_g.reshape(1, d)

    for layer in range(depth):
        last = layer == depth - 1
        lw = {
            "norm_g": norm_g[layer].reshape(1, d),
            "w_in": _prep_w_in(w_in[layer]),
            "qn_g": mla_qn_g[layer].reshape(1, MLA_Q_RANK),
            "w_uq": _prep_w_uq(mla_w_uq[layer]),
            "kvn_g": mla_kvn_g[layer].reshape(1, MLA_KV_RANK),
            "w_ukv": _prep_w_ukv(mla_w_ukv[layer]),
            "ln_g": cm_ln_g[layer].reshape(1, GW),
            "gavg": gavg,
            "ws": jnp.transpose(cm_w_s[layer], (1, 0, 2)).reshape(CM_CHUNK, CM_GROUPS * CM_CHUNK).astype(BF16),
            "bs": jnp.repeat(cm_b_s[layer].T, CM_GROUP_DIM, axis=1),
        }
        w_out_b = w_out[layer].astype(BF16)
        m = mod[layer]
        sh_x, sc_x, g_x = (m[:b, i * d:(i + 1) * d].reshape(b, 1, d) for i in range(3))
        sh_c, sc_c, g_c = (jnp.broadcast_to(m[b, i * d:(i + 1) * d], (b, 1, d)) for i in range(3))

        (aq, ak, av, ga, bq, bk, bv, gb, mc, pp, bgd) = _inproj(x, sh_x, sc_x, lw, rope_x, tm=512)
        (aqc, akc, avc, gac, bqc, bkc, bvc, gbc, mcc, ppc, bgdc) = _inproj(ctx, sh_c, sc_c, lw, rope_c, tm=lc)

        strips, win = _na_tables(na_rpb[layer], rows)
        out_a = _neighbourhood_attention(aq, ak, av, akc, avc, strips, win, ga)
        out_b = _dense_attention(bq, [bk, bkc], [bv, bvc], gb, MLA_HEAD_SPECS, tq=l, tr=512)
        x_new = _outproj(out_a, out_b, mc, pp, bgd, sc_w[layer], w_out_b, g_x, x, fg, tm=512,
                         final_norm=last)
        if not last:
            ctx_a = _dense_attention(aqc, [akc], [avc], gac, NA_HEAD_SPECS, tq=lc, tr=lc)
            ctx_b = _dense_attention(bqc, [bkc], [bvc], gbc, MLA_HEAD_SPECS, tq=lc, tr=lc)
            ctx = _outproj(ctx_a, ctx_b, mcc, ppc, bgdc, sc_w[layer], w_out_b, g_c, ctx, fg, tm=lc,
                           final_norm=False)
        x = x_new
    return x
```

```python
import functools

import numpy as np
import jax
import jax.numpy as jnp
from jax import lax
from jax.experimental import pallas as pl
from jax.experimental.pallas import tpu as pltpu

F32 = jnp.float32
BF16 = jnp.bfloat16

GRID_W = 64
NA_HEADS = 4
NA_HEAD_DIM = 64
NA_WIN_ROWS = 8
NA_WIN_COLS = 16
MLA_HEADS = 4
MLA_NOPE_DIM = 64
MLA_ROPE_DIM = 32
MLA_V_DIM = 64
MLA_Q_RANK = 256
MLA_KV_RANK = 128
CM_GROUPS = 4
CM_GROUP_DIM = 64
CM_CHUNK = 128
SC_TAPS = 3
ROPE_BASE = 10000.0
NORM_EPS = 1e-6
NEG_INF = -1e9
LOG2E = 1.4426950408889634

GW = 256
LANES = 128
NA_QROWS = 4
NA_KCHUNK_ROWS = 4
NA_KCHUNKS = 3
NA_BATCH_PER_STEP = 8
HALO = 16
VMEM_LIMIT = 56 * 1024 * 1024

C_AQ, C_AK, C_AV, C_AG = 0, 256, 512, 768
C_CQ, C_CKV, C_KR, C_BG = 1024, 1280, 1408, 1536
C_CU, C_CV, C_CG = 1792, 2048, 2304
C_DB, C_DC, C_DH, C_DG = 2560, 2816, 3072, 3328
D_IN_P = 3584


def _silu(x):
    return x / (1.0 + jnp.exp(-x))


def _gelu(x):
    return 0.5 * x * (1.0 + lax.erf(x * 0.7071067811865476))


def _rms(x):
    return x * lax.rsqrt(jnp.mean(x * x, axis=-1, keepdims=True) + NORM_EPS)


def _dot(a, b):
    return jnp.dot(a, b, preferred_element_type=F32)


def _dot_nt(a, b):
    return lax.dot_general(a, b, (((1,), (1,)), ((), ())), preferred_element_type=F32)


def _cparams(n_grid):
    return pltpu.CompilerParams(dimension_semantics=("parallel",) * n_grid,
                                vmem_limit_bytes=VMEM_LIMIT)


def _mod_kernel(c_ref, w_ref, b_ref, o_ref):
    s = _silu(c_ref[...]).astype(BF16)
    o_ref[...] = _dot(s, w_ref[...].astype(BF16)) + b_ref[...]


def _modulation(cc, w_mod, b_mod):
    depth, d, n = w_mod.shape
    r = cc.shape[0]
    tn = 768
    return pl.pallas_call(
        _mod_kernel,
        grid=(depth, n // tn),
        in_specs=[pl.BlockSpec((r, d), lambda l, j: (0, 0)),
                  pl.BlockSpec((None, d, tn), lambda l, j: (l, 0, j)),
                  pl.BlockSpec((None, 1, tn), lambda l, j: (l, 0, j))],
        out_specs=pl.BlockSpec((None, r, tn), lambda l, j: (l, 0, j)),
        out_shape=jax.ShapeDtypeStruct((depth, r, n), F32),
        compiler_params=_cparams(2),
        name="modulation",
    )(cc, w_mod, b_mod.reshape(depth, 1, n))


def _rope128(x, cos, sina, sinb):
    return x * cos + pltpu.roll(x, LANES - 8, 1) * sina + pltpu.roll(x, 8, 1) * sinb


def _inproj_kernel(x_ref, sh_ref, sc_ref, ng_ref, w_ref, qng_ref, wuq_ref, kvng_ref, wukv_ref,
                   cos_ref, sina_ref, sinb_ref, lng_ref, gavg_ref, ws_ref, bs_ref,
                   aq_ref, ak_ref, avt_ref, ga_ref, bq_ref, bk_ref, bvt_ref, gb_ref, mc_ref, p_ref,
                   bgd_ref):
    tm = x_ref.shape[0]
    h = _rms(x_ref[...]) * ng_ref[...]
    hb = (h * (1.0 + sc_ref[...]) + sh_ref[...]).astype(BF16)

    def proj(off, width=GW):
        return _dot(hb, w_ref[:, off:off + width])

    cos, sina, sinb = cos_ref[...], sina_ref[...], sinb_ref[...]
    qscale = (MLA_NOPE_DIM + MLA_ROPE_DIM) ** -0.5 * LOG2E
    gavg = gavg_ref[...]
    lane = lax.broadcasted_iota(jnp.int32, (1, GW), 1)

    def split_bf16(t):
        hi = t.astype(BF16)
        return hi, (t - hi.astype(F32)).astype(BF16)

    qn = (_rms(proj(C_CQ)) * qng_ref[...]).astype(BF16)
    kvn = (_rms(proj(C_CKV, MLA_KV_RANK)) * kvng_ref[...]).astype(BF16)
    v = _gelu(proj(C_CV))
    v_hi, v_lo = split_bf16(v)

    aq_ref[...] = (proj(C_AQ) * (NA_HEAD_DIM ** -0.5 * LOG2E)).astype(BF16)
    ak_ref[...] = proj(C_AK).astype(BF16)
    avt_ref[...] = proj(C_AV).T.astype(BF16)
    ga_ref[...] = _silu(proj(C_AG)).astype(BF16)

    qu = _dot(qn, wuq_ref[...])
    kvu = _dot(kvn, wukv_ref[...])
    cen = v - (_dot(v_hi, gavg) + _dot(v_lo, gavg))
    sq_hi, sq_lo = split_bf16(cen * cen)
    for p in range(MLA_HEADS // 2):
        o = p * GW
        bq_ref[:, o:o + LANES] = (_rope128(qu[:, o:o + LANES], cos, sina, sinb) * qscale).astype(BF16)
        bq_ref[:, o + LANES:o + GW] = (qu[:, o + LANES:o + GW] * qscale).astype(BF16)
    kr = _rope128(proj(C_KR, LANES), cos, sina, sinb).astype(BF16)
    for p in range(MLA_HEADS // 2):
        o = p * GW
        bk_ref[:, o:o + LANES] = kr
        bk_ref[:, o + LANES:o + GW] = kvu[:, p * LANES:(p + 1) * LANES].astype(BF16)
    bvt_ref[...] = kvu[:, 2 * LANES:].T.astype(BF16)

    p_ref[...] = (proj(C_DC) * proj(C_DH)).astype(BF16)
    bgd_ref[...] = (proj(C_DB) * _silu(proj(C_DG))).astype(BF16)

    var = _dot(sq_hi, gavg) + _dot(sq_lo, gavg)
    vn = (cen * lax.rsqrt(var + NORM_EPS) * lng_ref[...]).astype(BF16)
    gb_ref[...] = _silu(proj(C_BG)).astype(BF16)
    u = _gelu(proj(C_CU))
    gate_c = _silu(proj(C_CG))

    for n in range(tm // CM_CHUNK):
        rows = slice(n * CM_CHUNK, (n + 1) * CM_CHUNK)
        vc = vn[rows]
        s = bs_ref[...]
        for g in range(CM_GROUPS):
            in_g = (lane >= g * CM_GROUP_DIM) & (lane < (g + 1) * CM_GROUP_DIM)
            s = s + _dot(ws_ref[:, g * CM_CHUNK:(g + 1) * CM_CHUNK], jnp.where(in_g, vc, jnp.zeros_like(vc)))
        mc_ref[rows, :] = (u[rows] * s * gate_c[rows]).astype(BF16)


def _inproj(x, shift, scale, lw, tabs, tm):
    b, l, d = x.shape
    cos, sina, sinb = tabs
    full = lambda a: pl.BlockSpec(a.shape, lambda bi, i: (0,) * a.ndim)
    tok = lambda width: pl.BlockSpec((None, tm, width), lambda bi, i: (bi, i, 0))
    tok_t = pl.BlockSpec((None, GW, tm), lambda bi, i: (bi, 0, i))
    per_b = pl.BlockSpec((None, 1, d), lambda bi, i: (bi, 0, 0))
    pos = pl.BlockSpec((tm, LANES), lambda bi, i: (i, 0))
    widths = (GW, GW, None, GW, 2 * GW, 2 * GW, None, GW, GW, GW, GW)
    consts = (lw["norm_g"], lw["w_in"], lw["qn_g"], lw["w_uq"], lw["kvn_g"], lw["w_ukv"])
    consts2 = (lw["ln_g"], lw["gavg"], lw["ws"], lw["bs"])
    return pl.pallas_call(
        _inproj_kernel,
        grid=(b, l // tm),
        in_specs=[tok(d), per_b, per_b] + [full(a) for a in consts] + [pos, pos, pos]
                 + [full(a) for a in consts2],
        out_specs=[tok_t if w is None else tok(w) for w in widths],
        out_shape=[jax.ShapeDtypeStruct((b, GW, l) if w is None else (b, l, w), BF16) for w in widths],
        compiler_params=_cparams(2),
        name="inproj",
    )(x, shift, scale, *consts, cos, sina, sinb, *consts2)


def _lane_mask(ranges, width):
    lane = lax.broadcasted_iota(jnp.int32, (1, width), 1)
    m = None
    for lo, hi in ranges:
        t = (lane >= lo) & (lane < hi)
        m = t if m is None else (m | t)
    return m


def _softmax_probs(scores):
    def key_reduce(pieces, combine, reduce):
        by_height = {}
        for t in pieces:
            n = t.shape[0]
            by_height[n] = t if n not in by_height else combine(by_height[n], t)
        out = None
        for t in by_height.values():
            r = reduce(t, axis=0, keepdims=True)
            out = r if out is None else combine(out, r)
        return out

    m = key_reduce(scores, jnp.maximum, jnp.max)
    exps = [jnp.exp2(s - m) for s in scores]
    l = key_reduce(exps, jnp.add, jnp.sum)
    return [e.astype(BF16) for e in exps], l


def _weighted_values(probs, l, values_t):
    o = None
    for p, vt in zip(probs, values_t):
        pv = _dot(vt, p)
        o = pv if o is None else o + pv
    return o * (1.0 / l)


def _attend_blocks(n_blk, n_heads, scores_fn, values_fn, emit_fn, scr):
    def block(i, has_next, scores):
        units = [(i, h) for h in range(n_heads)] + ([(i + 1, 0)] if has_next else [])
        outs = []
        for u in range(n_heads):
            nxt = scores_fn(*units[u + 1]) if u + 1 < len(units) else None
            probs, l = _softmax_probs(scores)
            outs.append(_weighted_values(probs, l, values_fn(i, u)))
            scores = nxt
        emit_fn(i, outs)
        return scores

    scores = scores_fn(0, 0)
    if n_blk == 1:
        block(0, False, scores)
        return

    def stash(scores):
        for ref, s in zip(scr, scores):
            ref[...] = s

    def body(i, carry):
        stash(block(i, True, [ref[...] for ref in scr]))
        return carry

    stash(scores)
    lax.fori_loop(0, n_blk - 1, body, 0)
    block(n_blk - 1, False, [ref[...] for ref in scr])


def _pipeline_scratch(tq, piece_heights):
    return [pltpu.VMEM((n, tq), F32) for n in piece_heights]


def _head_rows(refs, hd):
    return [r[hd * 64:(hd + 1) * 64, :] for r in refs]


def _head_scores(q_ref, rows, k_refs, head_spec):
    grp, ranges = head_spec
    q = q_ref[rows, grp * GW:(grp + 1) * GW]
    qm = jnp.where(_lane_mask(ranges, GW), q, jnp.zeros_like(q))
    return [_dot_nt(k[:, grp * GW:(grp + 1) * GW], qm) for k in k_refs]


def _attn_kernel(*refs, head_specs, n_src, tr):
    q_ref = refs[0]
    k_refs = refs[1:1 + n_src]
    vt_refs = refs[1 + n_src:1 + 2 * n_src]
    gate_ref, o_ref = refs[1 + 2 * n_src], refs[2 + 2 * n_src]
    scr = refs[3 + 2 * n_src:]
    n_blk = q_ref.shape[0] // tr

    def rows(i):
        start = i * tr
        return pl.ds(start if isinstance(start, int) else pl.multiple_of(start, tr), tr)

    def emit(i, outs):
        o = jnp.concatenate(outs, axis=0).T
        o_ref[rows(i), :] = (o * gate_ref[rows(i), :].astype(F32)).astype(BF16)

    _attend_blocks(n_blk, len(head_specs),
                   lambda i, h: _head_scores(q_ref, rows(i), k_refs, head_specs[h]),
                   lambda i, h: _head_rows(vt_refs, h), emit, scr)


def _dense_attention(q, ks, vts, gate, head_specs, tq, tr):
    b, lq, qw = q.shape
    n_src = len(ks)
    kv_spec = lambda a: pl.BlockSpec((None,) + a.shape[1:], lambda bi, i: (bi, 0, 0))
    scratch = _pipeline_scratch(tr, [a.shape[1] for a in ks]) if tq > tr else []
    return pl.pallas_call(
        functools.partial(_attn_kernel, head_specs=head_specs, n_src=n_src, tr=tr),
        grid=(b, lq // tq),
        in_specs=[pl.BlockSpec((None, tq, qw), lambda bi, i: (bi, i, 0))]
                 + [kv_spec(a) for a in ks] + [kv_spec(a) for a in vts]
                 + [pl.BlockSpec((None, tq, GW), lambda bi, i: (bi, i, 0))],
        out_specs=pl.BlockSpec((None, tq, GW), lambda bi, i: (bi, i, 0)),
        out_shape=jax.ShapeDtypeStruct((b, lq, GW), BF16),
        scratch_shapes=scratch,
        compiler_params=_cparams(2),
        name="dense_attention",
    )(q, *ks, *vts, gate)


NA_HEAD_SPECS = tuple((0, ((h * 64, h * 64 + 64),)) for h in range(NA_HEADS))
MLA_HEAD_SPECS = tuple((h // 2, ((32 * (h % 2), 32 * (h % 2) + 32),
                                 (LANES + 64 * (h % 2), LANES + 64 * (h % 2) + 64)))
                       for h in range(MLA_HEADS))


def _na_kernel(q_ref, k0, k1, k2, vt0, vt1, vt2, kc_ref, vct_ref, strip_ref, win_ref, gate_ref,
               o_ref, bias_scr, *scr):
    qb = pl.program_id(0)
    n_b = q_ref.shape[0]
    kw = NA_KCHUNK_ROWS * GRID_W
    t_base = 8 - jnp.where(qb == 0, 0, jnp.where(qb == pl.num_programs(0) - 1, 8, 4))
    for hd in range(NA_HEADS):
        for j in range(NA_KCHUNKS):
            start = pl.multiple_of((t_base + NA_KCHUNK_ROWS * j) * GRID_W, GRID_W)
            keys = slice(j * kw, (j + 1) * kw)
            bias_scr[hd, keys, :] = strip_ref[hd, pl.ds(start, kw), :] + win_ref[keys, :]

    def scores_fn(i, hd):
        q = q_ref[i]
        qm = jnp.where(_lane_mask([(hd * 64, hd * 64 + 64)], GW), q, jnp.zeros_like(q))
        scores = [_dot_nt(k[i], qm) + bias_scr[hd, j * kw:(j + 1) * kw, :]
                  for j, k in enumerate((k0, k1, k2))]
        scores.append(_dot_nt(kc_ref[i], qm))
        return scores

    def values_fn(i, hd):
        return [vt[i, hd * 64:(hd + 1) * 64, :] for vt in (vt0, vt1, vt2, vct_ref)]

    def emit(i, outs):
        o = jnp.concatenate(outs, axis=0).T
        o_ref[i] = (o * gate_ref[i].astype(F32)).astype(BF16)

    _attend_blocks(n_b, NA_HEADS, scores_fn, values_fn, emit, scr)


def _na_tables(rpb, rows):
    n_h = rpb.shape[0]
    n_dr, n_dc = 2 * NA_WIN_ROWS - 1, 2 * NA_WIN_COLS - 1
    tq = NA_QROWS * GRID_W
    kc = np.arange(GRID_W)[:, None]
    qc = np.arange(GRID_W)[None, :]
    cs = np.clip(qc - NA_WIN_COLS // 2, 0, GRID_W - NA_WIN_COLS)
    col_ok = (kc >= cs) & (kc < cs + NA_WIN_COLS)
    ci = np.clip(kc - qc, -(NA_WIN_COLS - 1), NA_WIN_COLS - 1) + NA_WIN_COLS - 1
    sel_c = (ci[None] == np.arange(n_dc)[:, None, None]).astype(np.float32)
    mc = jnp.einsum("hac,ckq->hakq", rpb.astype(F32), jnp.asarray(sel_c),
                    precision=lax.Precision.HIGHEST) * LOG2E
    mc = jnp.tile(mc, (1, 1, 1, NA_QROWS))
    n_strips = 20
    dr = np.arange(n_strips)[:, None] - 8 - np.arange(tq)[None, :] // GRID_W
    sel_a = (dr[None] == (np.arange(n_dr) - (NA_WIN_ROWS - 1))[:, None, None]).astype(np.float32)
    strips = jnp.sum(jnp.asarray(sel_a)[None, :, :, None, :] * mc[:, :, None, :, :], axis=1)
    strips = strips.reshape(n_h, n_strips * GRID_W, tq)

    kr_win = min(NA_WIN_ROWS, rows)
    n_qb = rows // NA_QROWS
    n_krows = NA_KCHUNKS * NA_KCHUNK_ROWS
    rs = np.clip(np.arange(rows) - kr_win // 2, 0, rows - kr_win).reshape(n_qb, NA_QROWS)
    c0 = np.clip(np.arange(n_qb) - 1, 0, rows // NA_KCHUNK_ROWS - NA_KCHUNKS)
    kr = NA_KCHUNK_ROWS * c0[:, None] + np.arange(n_krows)[None, :]
    row_ok = (kr[:, :, None] >= rs[:, None, :]) & (kr[:, :, None] < rs[:, None, :] + kr_win)
    ok = row_ok[:, :, None, :, None] & col_ok[None, None, :, None, :]
    win = np.where(ok, 0.0, NEG_INF).astype(np.float32).reshape(n_qb, n_krows * GRID_W, tq)
    return strips, jnp.asarray(win)


def _neighbourhood_attention(q, k, vt, kc, vct, strips, win, gate):
    b, l, _ = q.shape
    tq = NA_QROWS * GRID_W
    kw = NA_KCHUNK_ROWS * GRID_W
    n_qb = l // tq
    n_kblk = l // kw
    n_b = NA_BATCH_PER_STEP
    assert b % n_b == 0

    def chunk(j):
        return lambda qb: jnp.clip(qb - 1, 0, n_kblk - NA_KCHUNKS) + j

    k_specs = [pl.BlockSpec((n_b, kw, GW), lambda qb, bi, c=chunk(j): (bi, c(qb), 0))
               for j in range(NA_KCHUNKS)]
    vt_specs = [pl.BlockSpec((n_b, GW, kw), lambda qb, bi, c=chunk(j): (bi, 0, c(qb)))
                for j in range(NA_KCHUNKS)]
    tok = pl.BlockSpec((n_b, tq, GW), lambda qb, bi: (bi, qb, 0))
    ctx_spec = lambda a: pl.BlockSpec((n_b,) + a.shape[1:], lambda qb, bi: (bi, 0, 0))
    return pl.pallas_call(
        _na_kernel,
        grid=(n_qb, b // n_b),
        in_specs=[tok] + k_specs + vt_specs + [ctx_spec(kc), ctx_spec(vct),
                  pl.BlockSpec(strips.shape, lambda qb, bi: (0, 0, 0), pipeline_mode=pl.Buffered(1)),
                  pl.BlockSpec((None,) + win.shape[1:], lambda qb, bi: (qb, 0, 0)),
                  tok],
        out_specs=tok,
        out_shape=jax.ShapeDtypeStruct((b, l, GW), BF16),
        scratch_shapes=[pltpu.VMEM((NA_HEADS, NA_KCHUNKS * kw, tq), F32)]
                       + _pipeline_scratch(tq, [kw] * NA_KCHUNKS + [kc.shape[1]]),
        compiler_params=_cparams(2),
        name="neighbourhood_attention",
    )(q, *([k] * NA_KCHUNKS), *([vt] * NA_KCHUNKS), kc, vct, strips, win, gate)


def _outproj_kernel(oa_ref, ob_ref, mc_ref, p_ref, pprev_ref, pnext_ref, bgd_ref, scw_ref, w_ref,
                    g_ref, x_ref, fg_ref, o_ref, *, final_norm):
    i = pl.program_id(1)
    tm = x_ref.shape[0]
    p = p_ref[...].astype(F32)
    row = lax.broadcasted_iota(jnp.int32, (tm, 1), 0)
    prev_row = jnp.where(i == 0, 0.0, pprev_ref[HALO - 1:HALO, :].astype(F32))
    next_row = jnp.where(i == pl.num_programs(1) - 1, 0.0, pnext_ref[0:1, :].astype(F32))
    p_dn = jnp.where(row == 0, prev_row, pltpu.roll(p, 1, 0))
    p_up = jnp.where(row == tm - 1, next_row, pltpu.roll(p, tm - 1, 0))
    y = p_dn * scw_ref[0:1, :] + p * scw_ref[1:2, :] + p_up * scw_ref[2:3, :]
    md = (bgd_ref[...].astype(F32) * y).astype(BF16)
    acc = (_dot(oa_ref[...], w_ref[0:GW, :]) + _dot(ob_ref[...], w_ref[GW:2 * GW, :])
           + _dot(mc_ref[...], w_ref[2 * GW:3 * GW, :]) + _dot(md, w_ref[3 * GW:4 * GW, :]))
    xn = x_ref[...] + g_ref[...] * acc
    if final_norm:
        xn = _rms(xn) * fg_ref[...]
    o_ref[...] = xn


def _outproj(oa, ob, mc, p, bgd, sc_w, w_out, g, x, final_g, tm, final_norm):
    b, l, d = x.shape
    n_halo = l // HALO
    per_tile = tm // HALO
    tok = lambda width: pl.BlockSpec((None, tm, width), lambda bi, i: (bi, i, 0))
    full = lambda a: pl.BlockSpec(a.shape, lambda bi, i: (0,) * a.ndim)
    prev = pl.BlockSpec((None, HALO, GW), lambda bi, i: (bi, jnp.maximum(i * per_tile - 1, 0), 0))
    nxt = pl.BlockSpec((None, HALO, GW),
                       lambda bi, i: (bi, jnp.minimum((i + 1) * per_tile, n_halo - 1), 0))
    return pl.pallas_call(
        functools.partial(_outproj_kernel, final_norm=final_norm),
        grid=(b, l // tm),
        in_specs=[tok(GW), tok(GW), tok(GW), tok(GW), prev, nxt, tok(GW), full(sc_w), full(w_out),
                  pl.BlockSpec((None, 1, d), lambda bi, i: (bi, 0, 0)), tok(d), full(final_g)],
        out_specs=tok(d),
        out_shape=jax.ShapeDtypeStruct((b, l, d), F32),
        compiler_params=_cparams(2),
        name="outproj",
    )(oa, ob, mc, p, p, p, bgd, sc_w, w_out, g, x, final_g)


def _prep_w_in(w):
    kr = w[:, 1408:1440]
    z = jnp.zeros((w.shape[0], 64), w.dtype)
    return jnp.concatenate([w[:, :1440], kr, z, w[:, 1440:]], axis=1).astype(BF16)


def _prep_w_uq(w):
    hd = MLA_NOPE_DIM + MLA_ROPE_DIM
    z = jnp.zeros((w.shape[0], 64), w.dtype)
    cols = []
    for p in range(MLA_HEADS // 2):
        h0, h1 = 2 * p, 2 * p + 1
        cols += [w[:, hd * h0 + MLA_NOPE_DIM:hd * (h0 + 1)], w[:, hd * h1 + MLA_NOPE_DIM:hd * (h1 + 1)], z,
                 w[:, hd * h0:hd * h0 + MLA_NOPE_DIM], w[:, hd * h1:hd * h1 + MLA_NOPE_DIM]]
    return jnp.concatenate(cols, axis=1).astype(BF16)


def _prep_w_ukv(w):
    hd = MLA_NOPE_DIM + MLA_V_DIM
    kn = [w[:, hd * h:hd * h + MLA_NOPE_DIM] for h in range(MLA_HEADS)]
    vv = [w[:, hd * h + MLA_NOPE_DIM:hd * (h + 1)] for h in range(MLA_HEADS)]
    return jnp.concatenate(kn + vv, axis=1).astype(BF16)


def _rope_tables(length):
    nf = MLA_ROPE_DIM // 4
    t = jnp.arange(length)
    row = (t // GRID_W).astype(F32)
    col = (t % GRID_W).astype(F32)
    inv = ROPE_BASE ** (-jnp.arange(nf, dtype=F32) / nf)
    ang = jnp.stack([row[:, None] * inv, col[:, None] * inv], axis=1)
    cos = jnp.cos(ang)
    sin = jnp.sin(ang)
    zero = jnp.zeros_like(sin)
    cos32 = jnp.stack([cos, cos], axis=2).reshape(length, MLA_ROPE_DIM)
    sina32 = jnp.stack([-sin, zero], axis=2).reshape(length, MLA_ROPE_DIM)
    sinb32 = jnp.stack([zero, sin], axis=2).reshape(length, MLA_ROPE_DIM)
    pad1 = jnp.ones((length, 64), F32)
    pad0 = jnp.zeros((length, 64), F32)
    return (jnp.concatenate([cos32, cos32, pad1], axis=1),
            jnp.concatenate([sina32, sina32, pad0], axis=1),
            jnp.concatenate([sinb32, sinb32, pad0], axis=1))


def _identity_rope_tables(length):
    return (jnp.ones((length, LANES), F32), jnp.zeros((length, LANES), F32),
            jnp.zeros((length, LANES), F32))


def kernel(x, c, ctx, c_ctx, norm_g, w_mod, b_mod, w_in, na_rpb, mla_qn_g, mla_w_uq, mla_kvn_g,
           mla_w_ukv, cm_ln_g, cm_w_s, cm_b_s, sc_w, w_out, final_g):
    b, l, d = x.shape
    lc = ctx.shape[1]
    depth = w_in.shape[0]
    rows = l // GRID_W
    assert l % (NA_QROWS * GRID_W) == 0 and rows >= NA_KCHUNKS * NA_KCHUNK_ROWS and lc % CM_CHUNK == 0
    assert w_in.shape[2] == 3488 and d == 4 * GW

    n_mod = -(-(b + 1) // 8) * 8
    cc = jnp.zeros((n_mod, d), F32).at[:b].set(c).at[b].set(c_ctx)
    mod = _modulation(cc, w_mod, b_mod)

    rope_x = _rope_tables(l)
    rope_c = _identity_rope_tables(lc)
    gavg = jnp.asarray(np.kron(np.eye(CM_GROUPS), np.full((CM_GROUP_DIM, CM_GROUP_DIM),
                                                          1.0 / CM_GROUP_DIM)), BF16)
    fg = final_g.reshape(1, d)

    for layer in range(depth):
        last = layer == depth - 1
        lw = {
            "norm_g": norm_g[layer].reshape(1, d),
            "w_in": _prep_w_in(w_in[layer]),
            "qn_g": mla_qn_g[layer].reshape(1, MLA_Q_RANK),
            "w_uq": _prep_w_uq(mla_w_uq[layer]),
            "kvn_g": mla_kvn_g[layer].reshape(1, MLA_KV_RANK),
            "w_ukv": _prep_w_ukv(mla_w_ukv[layer]),
            "ln_g": cm_ln_g[layer].reshape(1, GW),
            "gavg": gavg,
            "ws": jnp.transpose(cm_w_s[layer], (1, 0, 2)).reshape(CM_CHUNK, CM_GROUPS * CM_CHUNK).astype(BF16),
            "bs": jnp.repeat(cm_b_s[layer].T, CM_GROUP_DIM, axis=1),
        }
        w_out_b = w_out[layer].astype(BF16)
        m = mod[layer]
        sh_x, sc_x, g_x = (m[:b, i * d:(i + 1) * d].reshape(b, 1, d) for i in range(3))
        sh_c, sc_c, g_c = (jnp.broadcast_to(m[b, i * d:(i + 1) * d], (b, 1, d)) for i in range(3))

        (aq, ak, avt, ga, bq, bk, bvt, gb, mc, pp, bgd) = _inproj(x, sh_x, sc_x, lw, rope_x, tm=512)
        (aqc, akc, avct, gac, bqc, bkc, bvct, gbc, mcc, ppc, bgdc) = _inproj(ctx, sh_c, sc_c, lw, rope_c, tm=lc)

        strips, win = _na_tables(na_rpb[layer], rows)
        out_a = _neighbourhood_attention(aq, ak, avt, akc, avct, strips, win, ga)
        out_b = _dense_attention(bq, [bk, bkc], [bvt, bvct], gb, MLA_HEAD_SPECS, tq=l, tr=512)
        x_new = _outproj(out_a, out_b, mc, pp, bgd, sc_w[layer], w_out_b, g_x, x, fg, tm=1024,
                         final_norm=last)
        if not last:
            ctx_a = _dense_attention(aqc, [akc], [avct], gac, NA_HEAD_SPECS, tq=lc, tr=lc)
            ctx_b = _dense_attention(bqc, [bkc], [bvct], gbc, MLA_HEAD_SPECS, tq=lc, tr=lc)
            ctx = _outproj(ctx_a, ctx_b, mcc, ppc, bgdc, sc_w[layer], w_out_b, g_c, ctx, fg, tm=lc,
                           final_norm=False)
        x = x_new
    return x
```

```python
import functools

import numpy as np
import jax
import jax.numpy as jnp
from jax import lax
from jax.experimental import pallas as pl
from jax.experimental.pallas import tpu as pltpu

F32 = jnp.float32
BF16 = jnp.bfloat16

GRID_W = 64
NA_HEADS = 4
NA_HEAD_DIM = 64
NA_WIN_ROWS = 8
NA_WIN_COLS = 16
MLA_HEADS = 4
MLA_NOPE_DIM = 64
MLA_ROPE_DIM = 32
MLA_V_DIM = 64
MLA_Q_RANK = 256
MLA_KV_RANK = 128
CM_GROUPS = 4
CM_GROUP_DIM = 64
CM_CHUNK = 128
SC_TAPS = 3
ROPE_BASE = 10000.0
NORM_EPS = 1e-6
NEG_INF = -1e9
LOG2E = 1.4426950408889634

GW = 256
LANES = 128
NA_QROWS = 4
NA_KCHUNK_ROWS = 4
NA_KCHUNKS = 3
NA_BATCH_PER_STEP = 8
HALO = 16
V_SLAB = 80
VT_ROWS = 4 * V_SLAB
VMEM_LIMIT = 56 * 1024 * 1024

C_AQ, C_AK, C_AV, C_AG = 0, 256, 512, 768
C_CQ, C_CKV, C_KR, C_BG = 1024, 1280, 1408, 1536
C_CU, C_CV, C_CG = 1792, 2048, 2304
C_DB, C_DC, C_DH, C_DG = 2560, 2816, 3072, 3328
D_IN_P = 3584


def _silu(x):
    return x / (1.0 + jnp.exp(-x))


def _gelu(x):
    return 0.5 * x * (1.0 + lax.erf(x * 0.7071067811865476))


def _rms(x):
    return x * lax.rsqrt(jnp.mean(x * x, axis=-1, keepdims=True) + NORM_EPS)


def _dot(a, b):
    return jnp.dot(a, b, preferred_element_type=F32)


def _dot_nt(a, b):
    return lax.dot_general(a, b, (((1,), (1,)), ((), ())), preferred_element_type=F32)


def _cparams(n_grid):
    return pltpu.CompilerParams(dimension_semantics=("parallel",) * n_grid,
                                vmem_limit_bytes=VMEM_LIMIT)


def _mod_kernel(c_ref, w_ref, b_ref, o_ref):
    s = _silu(c_ref[...]).astype(BF16)
    o_ref[...] = _dot(s, w_ref[...].astype(BF16)) + b_ref[...]


def _modulation(cc, w_mod, b_mod):
    depth, d, n = w_mod.shape
    r = cc.shape[0]
    tn = 768
    return pl.pallas_call(
        _mod_kernel,
        grid=(depth, n // tn),
        in_specs=[pl.BlockSpec((r, d), lambda l, j: (0, 0)),
                  pl.BlockSpec((None, d, tn), lambda l, j: (l, 0, j)),
                  pl.BlockSpec((None, 1, tn), lambda l, j: (l, 0, j))],
        out_specs=pl.BlockSpec((None, r, tn), lambda l, j: (l, 0, j)),
        out_shape=jax.ShapeDtypeStruct((depth, r, n), F32),
        compiler_params=_cparams(2),
        name="modulation",
    )(cc, w_mod, b_mod.reshape(depth, 1, n))


def _rope128(x, cos, sina, sinb):
    return x * cos + pltpu.roll(x, LANES - 8, 1) * sina + pltpu.roll(x, 8, 1) * sinb


def _modulated(x_ref, sh_ref, sc_ref, ng_ref):
    h = _rms(x_ref[...]) * ng_ref[...]
    return (h * (1.0 + sc_ref[...]) + sh_ref[...]).astype(BF16)


def _store_value_slabs(vt_ref, vals):
    tm = vals.shape[0]
    ones_row = (lax.broadcasted_iota(jnp.int32, (V_SLAB - MLA_V_DIM, tm), 0) == 0).astype(BF16)
    vt = vals.T.astype(BF16)
    for hd in range(NA_HEADS):
        vt_ref[hd * V_SLAB:hd * V_SLAB + MLA_V_DIM, :] = vt[hd * MLA_V_DIM:(hd + 1) * MLA_V_DIM]
        vt_ref[hd * V_SLAB + MLA_V_DIM:(hd + 1) * V_SLAB, :] = ones_row


def _store_mla_keys(bk_ref, kr, kvu):
    for p in range(MLA_HEADS // 2):
        o = p * GW
        bk_ref[:, o:o + LANES] = kr
        bk_ref[:, o + LANES:o + GW] = kvu[:, p * LANES:(p + 1) * LANES].astype(BF16)


def _inproj_kv_kernel(x_ref, sh_ref, sc_ref, ng_ref, w_ref, kvng_ref, wukv_ref,
                      ak_ref, avt_ref, bk_ref, bvt_ref):
    hb = _modulated(x_ref, sh_ref, sc_ref, ng_ref)

    def proj(off, width=GW):
        return _dot(hb, w_ref[:, off:off + width])

    kvn = (_rms(proj(C_CKV, MLA_KV_RANK)) * kvng_ref[...]).astype(BF16)
    ak_ref[...] = proj(C_AK).astype(BF16)
    _store_value_slabs(avt_ref, proj(C_AV))
    kvu = _dot(kvn, wukv_ref[...])
    _store_mla_keys(bk_ref, proj(C_KR, LANES).astype(BF16), kvu)
    _store_value_slabs(bvt_ref, kvu[:, 2 * LANES:])


def _inproj_kernel(x_ref, sh_ref, sc_ref, ng_ref, w_ref, qng_ref, wuq_ref, kvng_ref, wukv_ref,
                   cos_ref, sina_ref, sinb_ref, lng_ref, gavg_ref, ws_ref, bs_ref,
                   aq_ref, ak_ref, avt_ref, ga_ref, bq_ref, bk_ref, bvt_ref, gb_ref, mc_ref, p_ref,
                   bgd_ref):
    tm = x_ref.shape[0]
    hb = _modulated(x_ref, sh_ref, sc_ref, ng_ref)

    def proj(off, width=GW):
        return _dot(hb, w_ref[:, off:off + width])

    cos, sina, sinb = cos_ref[...], sina_ref[...], sinb_ref[...]
    qscale = (MLA_NOPE_DIM + MLA_ROPE_DIM) ** -0.5 * LOG2E
    gavg = gavg_ref[...]
    lane = lax.broadcasted_iota(jnp.int32, (1, GW), 1)

    def split_bf16(t):
        hi = t.astype(BF16)
        return hi, (t - hi.astype(F32)).astype(BF16)

    qn = (_rms(proj(C_CQ)) * qng_ref[...]).astype(BF16)
    kvn = (_rms(proj(C_CKV, MLA_KV_RANK)) * kvng_ref[...]).astype(BF16)
    v = _gelu(proj(C_CV))
    v_hi, v_lo = split_bf16(v)

    aq_ref[...] = (proj(C_AQ) * (NA_HEAD_DIM ** -0.5 * LOG2E)).astype(BF16)
    ak_ref[...] = proj(C_AK).astype(BF16)
    _store_value_slabs(avt_ref, proj(C_AV))
    ga_ref[...] = _silu(proj(C_AG)).astype(BF16)

    qu = _dot(qn, wuq_ref[...])
    kvu = _dot(kvn, wukv_ref[...])
    cen = v - (_dot(v_hi, gavg) + _dot(v_lo, gavg))
    sq_hi, sq_lo = split_bf16(cen * cen)
    for p in range(MLA_HEADS // 2):
        o = p * GW
        bq_ref[:, o:o + LANES] = (_rope128(qu[:, o:o + LANES], cos, sina, sinb) * qscale).astype(BF16)
        bq_ref[:, o + LANES:o + GW] = (qu[:, o + LANES:o + GW] * qscale).astype(BF16)
    _store_mla_keys(bk_ref, _rope128(proj(C_KR, LANES), cos, sina, sinb).astype(BF16), kvu)
    _store_value_slabs(bvt_ref, kvu[:, 2 * LANES:])

    p_ref[...] = (proj(C_DC) * proj(C_DH)).astype(BF16)
    bgd_ref[...] = (proj(C_DB) * _silu(proj(C_DG))).astype(BF16)

    var = _dot(sq_hi, gavg) + _dot(sq_lo, gavg)
    vn = (cen * lax.rsqrt(var + NORM_EPS) * lng_ref[...]).astype(BF16)
    gb_ref[...] = _silu(proj(C_BG)).astype(BF16)
    u = _gelu(proj(C_CU))
    gate_c = _silu(proj(C_CG))

    for n in range(tm // CM_CHUNK):
        rows = slice(n * CM_CHUNK, (n + 1) * CM_CHUNK)
        vc = vn[rows]
        s = bs_ref[...]
        for g in range(CM_GROUPS):
            in_g = (lane >= g * CM_GROUP_DIM) & (lane < (g + 1) * CM_GROUP_DIM)
            s = s + _dot(ws_ref[:, g * CM_CHUNK:(g + 1) * CM_CHUNK], jnp.where(in_g, vc, jnp.zeros_like(vc)))
        mc_ref[rows, :] = (u[rows] * s * gate_c[rows]).astype(BF16)


def _inproj(x, shift, scale, lw, tabs, tm):
    b, l, d = x.shape
    cos, sina, sinb = tabs
    full = lambda a: pl.BlockSpec(a.shape, lambda bi, i: (0,) * a.ndim)
    tok = lambda width: pl.BlockSpec((None, tm, width), lambda bi, i: (bi, i, 0))
    tok_t = pl.BlockSpec((None, VT_ROWS, tm), lambda bi, i: (bi, 0, i))
    per_b = pl.BlockSpec((None, 1, d), lambda bi, i: (bi, 0, 0))
    pos = pl.BlockSpec((tm, LANES), lambda bi, i: (i, 0))
    widths = (GW, GW, None, GW, 2 * GW, 2 * GW, None, GW, GW, GW, GW)
    consts = (lw["norm_g"], lw["w_in"], lw["qn_g"], lw["w_uq"], lw["kvn_g"], lw["w_ukv"])
    consts2 = (lw["ln_g"], lw["gavg"], lw["ws"], lw["bs"])
    return pl.pallas_call(
        _inproj_kernel,
        grid=(b, l // tm),
        in_specs=[tok(d), per_b, per_b] + [full(a) for a in consts] + [pos, pos, pos]
                 + [full(a) for a in consts2],
        out_specs=[tok_t if w is None else tok(w) for w in widths],
        out_shape=[jax.ShapeDtypeStruct((b, VT_ROWS, l) if w is None else (b, l, w), BF16) for w in widths],
        compiler_params=_cparams(2),
        name="inproj",
    )(x, shift, scale, *consts, cos, sina, sinb, *consts2)


def _inproj_kv(x, shift, scale, lw, tm):
    b, l, d = x.shape
    full = lambda a: pl.BlockSpec(a.shape, lambda bi, i: (0,) * a.ndim)
    tok = lambda width: pl.BlockSpec((None, tm, width), lambda bi, i: (bi, i, 0))
    tok_t = pl.BlockSpec((None, VT_ROWS, tm), lambda bi, i: (bi, 0, i))
    per_b = pl.BlockSpec((None, 1, d), lambda bi, i: (bi, 0, 0))
    consts = (lw["norm_g"], lw["w_in"], lw["kvn_g"], lw["w_ukv"])
    vt_shape = jax.ShapeDtypeStruct((b, VT_ROWS, l), BF16)
    return pl.pallas_call(
        _inproj_kv_kernel,
        grid=(b, l // tm),
        in_specs=[tok(d), per_b, per_b] + [full(a) for a in consts],
        out_specs=[tok(GW), tok_t, tok(2 * GW), tok_t],
        out_shape=[jax.ShapeDtypeStruct((b, l, GW), BF16), vt_shape,
                   jax.ShapeDtypeStruct((b, l, 2 * GW), BF16), vt_shape],
        compiler_params=_cparams(2),
        name="inproj_kv",
    )(x, shift, scale, *consts)


def _lane_mask(ranges, width):
    lane = lax.broadcasted_iota(jnp.int32, (1, width), 1)
    m = None
    for lo, hi in ranges:
        t = (lane >= lo) & (lane < hi)
        m = t if m is None else (m | t)
    return m


def _softmax_probs(scores, exp_dtype):
    by_height = {}
    for t in scores:
        n = t.shape[0]
        by_height[n] = t if n not in by_height else jnp.maximum(by_height[n], t)
    m = None
    for t in by_height.values():
        r = jnp.max(t, axis=0, keepdims=True)
        m = r if m is None else jnp.maximum(m, r)
    return [jnp.exp2((s - m).astype(exp_dtype)).astype(BF16) for s in scores]


def _weighted_values(probs, values_t):
    o = None
    for p, vt in zip(probs, values_t):
        pv = _dot(vt, p)
        o = pv if o is None else o + pv
    return o[:MLA_V_DIM] * (1.0 / o[MLA_V_DIM:MLA_V_DIM + 1])


def _attend_blocks(n_blk, n_heads, scores_fn, values_fn, emit_fn, scr, exp_dtype):
    def block(i, has_next, scores):
        units = [(i, h) for h in range(n_heads)] + ([(i + 1, 0)] if has_next else [])
        outs = []
        for u in range(n_heads):
            nxt = scores_fn(*units[u + 1]) if u + 1 < len(units) else None
            outs.append(_weighted_values(_softmax_probs(scores, exp_dtype), values_fn(i, u)))
            scores = nxt
        emit_fn(i, outs)
        return scores

    scores = scores_fn(0, 0)
    if n_blk == 1:
        block(0, False, scores)
        return

    def stash(scores):
        for ref, s in zip(scr, scores):
            ref[...] = s

    def body(i, carry):
        stash(block(i, True, [ref[...] for ref in scr]))
        return carry

    stash(scores)
    lax.fori_loop(0, n_blk - 1, body, 0)
    block(n_blk - 1, False, [ref[...] for ref in scr])


def _pipeline_scratch(tq, piece_heights):
    return [pltpu.VMEM((n, tq), F32) for n in piece_heights]


def _head_rows(refs, hd):
    return [r[hd * V_SLAB:(hd + 1) * V_SLAB, :] for r in refs]


def _head_scores(q_ref, rows, k_refs, head_spec):
    grp, ranges = head_spec
    q = q_ref[rows, grp * GW:(grp + 1) * GW]
    qm = jnp.where(_lane_mask(ranges, GW), q, jnp.zeros_like(q))
    return [_dot_nt(k[:, grp * GW:(grp + 1) * GW], qm) for k in k_refs]


def _attn_kernel(*refs, head_specs, n_src, tr):
    q_ref = refs[0]
    k_refs = refs[1:1 + n_src]
    vt_refs = refs[1 + n_src:1 + 2 * n_src]
    gate_ref, o_ref = refs[1 + 2 * n_src], refs[2 + 2 * n_src]
    scr = refs[3 + 2 * n_src:]
    n_blk = q_ref.shape[0] // tr

    def rows(i):
        start = i * tr
        return pl.ds(start if isinstance(start, int) else pl.multiple_of(start, tr), tr)

    def emit(i, outs):
        o = jnp.concatenate(outs, axis=0).T
        o_ref[rows(i), :] = (o * gate_ref[rows(i), :].astype(F32)).astype(BF16)

    _attend_blocks(n_blk, len(head_specs),
                   lambda i, h: _head_scores(q_ref, rows(i), k_refs, head_specs[h]),
                   lambda i, h: _head_rows(vt_refs, h), emit, scr, F32)


def _dense_attention(q, ks, vts, gate, head_specs, tq, tr):
    b, lq, qw = q.shape
    n_src = len(ks)
    kv_spec = lambda a: pl.BlockSpec((None,) + a.shape[1:], lambda bi, i: (bi, 0, 0))
    scratch = _pipeline_scratch(tr, [a.shape[1] for a in ks]) if tq > tr else []
    return pl.pallas_call(
        functools.partial(_attn_kernel, head_specs=head_specs, n_src=n_src, tr=tr),
        grid=(b, lq // tq),
        in_specs=[pl.BlockSpec((None, tq, qw), lambda bi, i: (bi, i, 0))]
                 + [kv_spec(a) for a in ks] + [kv_spec(a) for a in vts]
                 + [pl.BlockSpec((None, tq, GW), lambda bi, i: (bi, i, 0))],
        out_specs=pl.BlockSpec((None, tq, GW), lambda bi, i: (bi, i, 0)),
        out_shape=jax.ShapeDtypeStruct((b, lq, GW), BF16),
        scratch_shapes=scratch,
        compiler_params=_cparams(2),
        name="dense_attention",
    )(q, *ks, *vts, gate)


NA_HEAD_SPECS = tuple((0, ((h * 64, h * 64 + 64),)) for h in range(NA_HEADS))
MLA_HEAD_SPECS = tuple((h // 2, ((32 * (h % 2), 32 * (h % 2) + 32),
                                 (LANES + 64 * (h % 2), LANES + 64 * (h % 2) + 64)))
                       for h in range(MLA_HEADS))


def _na_kernel(q_ref, k0, k1, k2, vt0, vt1, vt2, kc_ref, vct_ref, strip_ref, win_ref, gate_ref,
               o_ref, bias_scr, *scr):
    qb = pl.program_id(0)
    n_b = q_ref.shape[0]
    kw = NA_KCHUNK_ROWS * GRID_W
    t_base = 8 - jnp.where(qb == 0, 0, jnp.where(qb == pl.num_programs(0) - 1, 8, 4))
    for hd in range(NA_HEADS):
        for j in range(NA_KCHUNKS):
            start = pl.multiple_of((t_base + NA_KCHUNK_ROWS * j) * GRID_W, GRID_W)
            keys = slice(j * kw, (j + 1) * kw)
            bias_scr[hd, keys, :] = strip_ref[hd, pl.ds(start, kw), :] + win_ref[keys, :]

    def scores_fn(i, hd):
        q = q_ref[i]
        qm = jnp.where(_lane_mask([(hd * 64, hd * 64 + 64)], GW), q, jnp.zeros_like(q))
        scores = [_dot_nt(k[i], qm) + bias_scr[hd, j * kw:(j + 1) * kw, :]
                  for j, k in enumerate((k0, k1, k2))]
        scores.append(_dot_nt(kc_ref[i], qm))
        return scores

    def values_fn(i, hd):
        return [vt[i, hd * V_SLAB:(hd + 1) * V_SLAB, :] for vt in (vt0, vt1, vt2, vct_ref)]

    def emit(i, outs):
        o = jnp.concatenate(outs, axis=0).T
        o_ref[i] = (o * gate_ref[i].astype(F32)).astype(BF16)

    _attend_blocks(n_b, NA_HEADS, scores_fn, values_fn, emit, scr, BF16)


def _na_tables(rpb, rows):
    n_h = rpb.shape[0]
    n_dr, n_dc = 2 * NA_WIN_ROWS - 1, 2 * NA_WIN_COLS - 1
    tq = NA_QROWS * GRID_W
    kc = np.arange(GRID_W)[:, None]
    qc = np.arange(GRID_W)[None, :]
    cs = np.clip(qc - NA_WIN_COLS // 2, 0, GRID_W - NA_WIN_COLS)
    col_ok = (kc >= cs) & (kc < cs + NA_WIN_COLS)
    ci = np.clip(kc - qc, -(NA_WIN_COLS - 1), NA_WIN_COLS - 1) + NA_WIN_COLS - 1
    sel_c = (ci[None] == np.arange(n_dc)[:, None, None]).astype(np.float32)
    mc = jnp.einsum("hac,ckq->hakq", rpb.astype(F32), jnp.asarray(sel_c),
                    precision=lax.Precision.HIGHEST) * LOG2E
    mc = jnp.tile(mc, (1, 1, 1, NA_QROWS))
    n_strips = 20
    dr = np.arange(n_strips)[:, None] - 8 - np.arange(tq)[None, :] // GRID_W
    sel_a = (dr[None] == (np.arange(n_dr) - (NA_WIN_ROWS - 1))[:, None, None]).astype(np.float32)
    strips = jnp.sum(jnp.asarray(sel_a)[None, :, :, None, :] * mc[:, :, None, :, :], axis=1)
    strips = strips.reshape(n_h, n_strips * GRID_W, tq)

    kr_win = min(NA_WIN_ROWS, rows)
    n_qb = rows // NA_QROWS
    n_krows = NA_KCHUNKS * NA_KCHUNK_ROWS
    rs = np.clip(np.arange(rows) - kr_win // 2, 0, rows - kr_win).reshape(n_qb, NA_QROWS)
    c0 = np.clip(np.arange(n_qb) - 1, 0, rows // NA_KCHUNK_ROWS - NA_KCHUNKS)
    kr = NA_KCHUNK_ROWS * c0[:, None] + np.arange(n_krows)[None, :]
    row_ok = (kr[:, :, None] >= rs[:, None, :]) & (kr[:, :, None] < rs[:, None, :] + kr_win)
    ok = row_ok[:, :, None, :, None] & col_ok[None, None, :, None, :]
    win = np.where(ok, 0.0, NEG_INF).astype(np.float32).reshape(n_qb, n_krows * GRID_W, tq)
    return strips, jnp.asarray(win)


def _neighbourhood_attention(q, k, vt, kc, vct, strips, win, gate):
    b, l, _ = q.shape
    tq = NA_QROWS * GRID_W
    kw = NA_KCHUNK_ROWS * GRID_W
    n_qb = l // tq
    n_kblk = l // kw
    n_b = NA_BATCH_PER_STEP
    assert b % n_b == 0

    def chunk(j):
        return lambda qb: jnp.clip(qb - 1, 0, n_kblk - NA_KCHUNKS) + j

    k_specs = [pl.BlockSpec((n_b, kw, GW), lambda qb, bi, c=chunk(j): (bi, c(qb), 0))
               for j in range(NA_KCHUNKS)]
    vt_specs = [pl.BlockSpec((n_b, VT_ROWS, kw), lambda qb, bi, c=chunk(j): (bi, 0, c(qb)))
                for j in range(NA_KCHUNKS)]
    tok = pl.BlockSpec((n_b, tq, GW), lambda qb, bi: (bi, qb, 0))
    ctx_spec = lambda a: pl.BlockSpec((n_b,) + a.shape[1:], lambda qb, bi: (bi, 0, 0))
    return pl.pallas_call(
        _na_kernel,
        grid=(n_qb, b // n_b),
        in_specs=[tok] + k_specs + vt_specs + [ctx_spec(kc), ctx_spec(vct),
                  pl.BlockSpec(strips.shape, lambda qb, bi: (0, 0, 0), pipeline_mode=pl.Buffered(1)),
                  pl.BlockSpec((None,) + win.shape[1:], lambda qb, bi: (qb, 0, 0)),
                  tok],
        out_specs=tok,
        out_shape=jax.ShapeDtypeStruct((b, l, GW), BF16),
        scratch_shapes=[pltpu.VMEM((NA_HEADS, NA_KCHUNKS * kw, tq), F32)]
                       + _pipeline_scratch(tq, [kw] * NA_KCHUNKS + [kc.shape[1]]),
        compiler_params=_cparams(2),
        name="neighbourhood_attention",
    )(q, *([k] * NA_KCHUNKS), *([vt] * NA_KCHUNKS), kc, vct, strips, win, gate)


def _outproj_kernel(oa_ref, ob_ref, mc_ref, p_ref, pprev_ref, pnext_ref, bgd_ref, scw_ref, w_ref,
                    g_ref, x_ref, fg_ref, o_ref, *, final_norm):
    i = pl.program_id(1)
    tm = x_ref.shape[0]
    p = p_ref[...].astype(F32)
    row = lax.broadcasted_iota(jnp.int32, (tm, 1), 0)
    prev_row = jnp.where(i == 0, 0.0, pprev_ref[HALO - 1:HALO, :].astype(F32))
    next_row = jnp.where(i == pl.num_programs(1) - 1, 0.0, pnext_ref[0:1, :].astype(F32))
    p_dn = jnp.where(row == 0, prev_row, pltpu.roll(p, 1, 0))
    p_up = jnp.where(row == tm - 1, next_row, pltpu.roll(p, tm - 1, 0))
    y = p_dn * scw_ref[0:1, :] + p * scw_ref[1:2, :] + p_up * scw_ref[2:3, :]
    md = (bgd_ref[...].astype(F32) * y).astype(BF16)
    acc = (_dot(oa_ref[...], w_ref[0:GW, :]) + _dot(ob_ref[...], w_ref[GW:2 * GW, :])
           + _dot(mc_ref[...], w_ref[2 * GW:3 * GW, :]) + _dot(md, w_ref[3 * GW:4 * GW, :]))
    xn = x_ref[...] + g_ref[...] * acc
    if final_norm:
        xn = _rms(xn) * fg_ref[...]
    o_ref[...] = xn


def _outproj(oa, ob, mc, p, bgd, sc_w, w_out, g, x, final_g, tm, final_norm):
    b, l, d = x.shape
    n_halo = l // HALO
    per_tile = tm // HALO
    tok = lambda width: pl.BlockSpec((None, tm, width), lambda bi, i: (bi, i, 0))
    full = lambda a: pl.BlockSpec(a.shape, lambda bi, i: (0,) * a.ndim)
    prev = pl.BlockSpec((None, HALO, GW), lambda bi, i: (bi, jnp.maximum(i * per_tile - 1, 0), 0))
    nxt = pl.BlockSpec((None, HALO, GW),
                       lambda bi, i: (bi, jnp.minimum((i + 1) * per_tile, n_halo - 1), 0))
    return pl.pallas_call(
        functools.partial(_outproj_kernel, final_norm=final_norm),
        grid=(b, l // tm),
        in_specs=[tok(GW), tok(GW), tok(GW), tok(GW), prev, nxt, tok(GW), full(sc_w), full(w_out),
                  pl.BlockSpec((None, 1, d), lambda bi, i: (bi, 0, 0)), tok(d), full(final_g)],
        out_specs=tok(d),
        out_shape=jax.ShapeDtypeStruct((b, l, d), F32),
        compiler_params=_cparams(2),
        name="outproj",
    )(oa, ob, mc, p, p, p, bgd, sc_w, w_out, g, x, final_g)


def _prep_w_in(w):
    kr = w[:, 1408:1440]
    z = jnp.zeros((w.shape[0], 64), w.dtype)
    return jnp.concatenate([w[:, :1440], kr, z, w[:, 1440:]], axis=1).astype(BF16)


def _prep_w_uq(w):
    hd = MLA_NOPE_DIM + MLA_ROPE_DIM
    z = jnp.zeros((w.shape[0], 64), w.dtype)
    cols = []
    for p in range(MLA_HEADS // 2):
        h0, h1 = 2 * p, 2 * p + 1
        cols += [w[:, hd * h0 + MLA_NOPE_DIM:hd * (h0 + 1)], w[:, hd * h1 + MLA_NOPE_DIM:hd * (h1 + 1)], z,
                 w[:, hd * h0:hd * h0 + MLA_NOPE_DIM], w[:, hd * h1:hd * h1 + MLA_NOPE_DIM]]
    return jnp.concatenate(cols, axis=1).astype(BF16)


def _prep_w_ukv(w):
    hd = MLA_NOPE_DIM + MLA_V_DIM
    kn = [w[:, hd * h:hd * h + MLA_NOPE_DIM] for h in range(MLA_HEADS)]
    vv = [w[:, hd * h + MLA_NOPE_DIM:hd * (h + 1)] for h in range(MLA_HEADS)]
    return jnp.concatenate(kn + vv, axis=1).astype(BF16)


def _rope_tables(length):
    nf = MLA_ROPE_DIM // 4
    t = jnp.arange(length)
    row = (t // GRID_W).astype(F32)
    col = (t % GRID_W).astype(F32)
    inv = ROPE_BASE ** (-jnp.arange(nf, dtype=F32) / nf)
    ang = jnp.stack([row[:, None] * inv, col[:, None] * inv], axis=1)
    cos = jnp.cos(ang)
    sin = jnp.sin(ang)
    zero = jnp.zeros_like(sin)
    cos32 = jnp.stack([cos, cos], axis=2).reshape(length, MLA_ROPE_DIM)
    sina32 = jnp.stack([-sin, zero], axis=2).reshape(length, MLA_ROPE_DIM)
    sinb32 = jnp.stack([zero, sin], axis=2).reshape(length, MLA_ROPE_DIM)
    pad1 = jnp.ones((length, 64), F32)
    pad0 = jnp.zeros((length, 64), F32)
    return (jnp.concatenate([cos32, cos32, pad1], axis=1),
            jnp.concatenate([sina32, sina32, pad0], axis=1),
            jnp.concatenate([sinb32, sinb32, pad0], axis=1))


def _identity_rope_tables(length):
    return (jnp.ones((length, LANES), F32), jnp.zeros((length, LANES), F32),
            jnp.zeros((length, LANES), F32))


def kernel(x, c, ctx, c_ctx, norm_g, w_mod, b_mod, w_in, na_rpb, mla_qn_g, mla_w_uq, mla_kvn_g,
           mla_w_ukv, cm_ln_g, cm_w_s, cm_b_s, sc_w, w_out, final_g):
    b, l, d = x.shape
    lc = ctx.shape[1]
    depth = w_in.shape[0]
    rows = l // GRID_W
    assert l % (NA_QROWS * GRID_W) == 0 and rows >= NA_KCHUNKS * NA_KCHUNK_ROWS and lc % CM_CHUNK == 0
    assert w_in.shape[2] == 3488 and d == 4 * GW

    n_mod = -(-(b + 1) // 8) * 8
    cc = jnp.zeros((n_mod, d), F32).at[:b].set(c).at[b].set(c_ctx)
    mod = _modulation(cc, w_mod, b_mod)

    rope_x = _rope_tables(l)
    rope_c = _identity_rope_tables(lc)
    gavg = jnp.asarray(np.kron(np.eye(CM_GROUPS), np.full((CM_GROUP_DIM, CM_GROUP_DIM),
                                                          1.0 / CM_GROUP_DIM)), BF16)
    fg = final_g.reshape(1, d)

    for layer in range(depth):
        last = layer == depth - 1
        lw = {
            "norm_g": norm_g[layer].reshape(1, d),
            "w_in": _prep_w_in(w_in[layer]),
            "qn_g": mla_qn_g[layer].reshape(1, MLA_Q_RANK),
            "w_uq": _prep_w_uq(mla_w_uq[layer]),
            "kvn_g": mla_kvn_g[layer].reshape(1, MLA_KV_RANK),
            "w_ukv": _prep_w_ukv(mla_w_ukv[layer]),
            "ln_g": cm_ln_g[layer].reshape(1, GW),
            "gavg": gavg,
            "ws": jnp.transpose(cm_w_s[layer], (1, 0, 2)).reshape(CM_CHUNK, CM_GROUPS * CM_CHUNK).astype(BF16),
            "bs": jnp.repeat(cm_b_s[layer].T, CM_GROUP_DIM, axis=1),
        }
        w_out_b = w_out[layer].astype(BF16)
        m = mod[layer]
        sh_x, sc_x, g_x = (m[:b, i * d:(i + 1) * d].reshape(b, 1, d) for i in range(3))
        sh_c, sc_c, g_c = (jnp.broadcast_to(m[b, i * d:(i + 1) * d], (b, 1, d)) for i in range(3))

        (aq, ak, avt, ga, bq, bk, bvt, gb, mc, pp, bgd) = _inproj(x, sh_x, sc_x, lw, rope_x, tm=512)
        if last:
            akc, avct, bkc, bvct = _inproj_kv(ctx, sh_c, sc_c, lw, tm=lc)
        else:
            (aqc, akc, avct, gac, bqc, bkc, bvct, gbc, mcc, ppc, bgdc) = _inproj(ctx, sh_c, sc_c, lw, rope_c,
                                                                                 tm=lc)

        strips, win = _na_tables(na_rpb[layer], rows)
        out_a = _neighbourhood_attention(aq, ak, avt, akc, avct, strips, win, ga)
        out_b = _dense_attention(bq, [bk, bkc], [bvt, bvct], gb, MLA_HEAD_SPECS, tq=l, tr=512)
        x_new = _outproj(out_a, out_b, mc, pp, bgd, sc_w[layer], w_out_b, g_x, x, fg, tm=1024,
                         final_norm=last)
        if not last:
            ctx_a = _dense_attention(aqc, [akc], [avct], gac, NA_HEAD_SPECS, tq=lc, tr=lc)
            ctx_b = _dense_attention(bqc, [bkc], [bvct], gbc, MLA_HEAD_SPECS, tq=lc, tr=lc)
            ctx = _outproj(ctx_a, ctx_b, mcc, ppc, bgdc, sc_w[layer], w_out_b, g_c, ctx, fg, tm=lc,
                           final_norm=False)
        x = x_new
    return x
```

```python
import functools

import numpy as np
import jax
import jax.numpy as jnp
from jax import lax
from jax.experimental import pallas as pl
from jax.experimental.pallas import tpu as pltpu

F32 = jnp.float32
BF16 = jnp.bfloat16

GRID_W = 64
NA_HEADS = 4
NA_HEAD_DIM = 64
NA_WIN_ROWS = 8
NA_WIN_COLS = 16
MLA_HEADS = 4
MLA_NOPE_DIM = 64
MLA_ROPE_DIM = 32
MLA_V_DIM = 64
MLA_Q_RANK = 256
MLA_KV_RANK = 128
CM_GROUPS = 4
CM_GROUP_DIM = 64
CM_CHUNK = 128
SC_TAPS = 3
ROPE_BASE = 10000.0
NORM_EPS = 1e-6
NEG_INF = -1e9
LOG2E = 1.4426950408889634

GW = 256
LANES = 128
NA_QROWS = 4
NA_KCHUNK_ROWS = 4
NA_KCHUNKS = 3
NA_BATCH_PER_STEP = 8
HALO = 16
V_SLAB = 80
VT_ROWS = 4 * V_SLAB
VMEM_LIMIT = 56 * 1024 * 1024

C_AQ, C_AK, C_AV, C_AG = 0, 256, 512, 768
C_CQ, C_CKV, C_KR, C_BG = 1024, 1280, 1408, 1536
C_CU, C_CV, C_CG = 1792, 2048, 2304
C_DB, C_DC, C_DH, C_DG = 2560, 2816, 3072, 3328
D_IN_P = 3584


def _silu(x):
    return x / (1.0 + jnp.exp(-x))


def _gelu(x):
    return 0.5 * x * (1.0 + lax.erf(x * 0.7071067811865476))


def _rms(x):
    return x * lax.rsqrt(jnp.mean(x * x, axis=-1, keepdims=True) + NORM_EPS)


def _dot(a, b):
    return jnp.dot(a, b, preferred_element_type=F32)


def _dot_nt(a, b):
    return lax.dot_general(a, b, (((1,), (1,)), ((), ())), preferred_element_type=F32)


def _cparams(n_grid):
    return pltpu.CompilerParams(dimension_semantics=("parallel",) * n_grid,
                                vmem_limit_bytes=VMEM_LIMIT)


def _mod_kernel(c_ref, w_ref, b_ref, o_ref):
    s = _silu(c_ref[...]).astype(BF16)
    o_ref[...] = _dot(s, w_ref[...].astype(BF16)) + b_ref[...]


def _modulation(cc, w_mod, b_mod):
    depth, d, n = w_mod.shape
    r = cc.shape[0]
    tn = 768
    return pl.pallas_call(
        _mod_kernel,
        grid=(depth, n // tn),
        in_specs=[pl.BlockSpec((r, d), lambda l, j: (0, 0)),
                  pl.BlockSpec((None, d, tn), lambda l, j: (l, 0, j)),
                  pl.BlockSpec((None, 1, tn), lambda l, j: (l, 0, j))],
        out_specs=pl.BlockSpec((None, r, tn), lambda l, j: (l, 0, j)),
        out_shape=jax.ShapeDtypeStruct((depth, r, n), F32),
        compiler_params=_cparams(2),
        name="modulation",
    )(cc, w_mod, b_mod.reshape(depth, 1, n))


def _rope128(x, cos, sina, sinb):
    return x * cos + pltpu.roll(x, LANES - 8, 1) * sina + pltpu.roll(x, 8, 1) * sinb


def _modulated(x_ref, sh_ref, sc_ref, ng_ref):
    h = _rms(x_ref[...]) * ng_ref[...]
    return (h * (1.0 + sc_ref[...]) + sh_ref[...]).astype(BF16)


def _store_value_slabs(vt_ref, vals):
    tm = vals.shape[0]
    ones_row = (lax.broadcasted_iota(jnp.int32, (V_SLAB - MLA_V_DIM, tm), 0) == 0).astype(BF16)
    vt = vals.T.astype(BF16)
    for hd in range(NA_HEADS):
        vt_ref[hd * V_SLAB:hd * V_SLAB + MLA_V_DIM, :] = vt[hd * MLA_V_DIM:(hd + 1) * MLA_V_DIM]
        vt_ref[hd * V_SLAB + MLA_V_DIM:(hd + 1) * V_SLAB, :] = ones_row


def _store_mla_keys(bk_ref, kr, kvu):
    for p in range(MLA_HEADS // 2):
        o = p * GW
        bk_ref[:, o:o + LANES] = kr
        bk_ref[:, o + LANES:o + GW] = kvu[:, p * LANES:(p + 1) * LANES].astype(BF16)


def _inproj_kv_kernel(x_ref, sh_ref, sc_ref, ng_ref, w_ref, kvng_ref, wukv_ref,
                      ak_ref, avt_ref, bk_ref, bvt_ref):
    hb = _modulated(x_ref, sh_ref, sc_ref, ng_ref)

    def proj(off, width=GW):
        return _dot(hb, w_ref[:, off:off + width])

    kvn = (_rms(proj(C_CKV, MLA_KV_RANK)) * kvng_ref[...]).astype(BF16)
    ak_ref[...] = proj(C_AK).astype(BF16)
    _store_value_slabs(avt_ref, proj(C_AV))
    kvu = _dot(kvn, wukv_ref[...])
    _store_mla_keys(bk_ref, proj(C_KR, LANES).astype(BF16), kvu)
    _store_value_slabs(bvt_ref, kvu[:, 2 * LANES:])


def _inproj_kernel(x_ref, sh_ref, sc_ref, ng_ref, w_ref, qng_ref, wuq_ref, kvng_ref, wukv_ref,
                   cos_ref, sina_ref, sinb_ref, lng_ref, gavg_ref, ws_ref, bs_ref,
                   aq_ref, ak_ref, avt_ref, ga_ref, bq_ref, bk_ref, bvt_ref, gb_ref, mc_ref, p_ref,
                   bgd_ref):
    tm = x_ref.shape[0]
    hb = _modulated(x_ref, sh_ref, sc_ref, ng_ref)

    def proj(off, width=GW):
        return _dot(hb, w_ref[:, off:off + width])

    cos, sina, sinb = cos_ref[...], sina_ref[...], sinb_ref[...]
    qscale = (MLA_NOPE_DIM + MLA_ROPE_DIM) ** -0.5 * LOG2E
    gavg = gavg_ref[...]
    lane = lax.broadcasted_iota(jnp.int32, (1, GW), 1)

    def split_bf16(t):
        hi = t.astype(BF16)
        return hi, (t - hi.astype(F32)).astype(BF16)

    qn = (_rms(proj(C_CQ)) * qng_ref[...]).astype(BF16)
    kvn = (_rms(proj(C_CKV, MLA_KV_RANK)) * kvng_ref[...]).astype(BF16)
    v = _gelu(proj(C_CV))
    v_hi, v_lo = split_bf16(v)

    aq_ref[...] = (proj(C_AQ) * (NA_HEAD_DIM ** -0.5 * LOG2E)).astype(BF16)
    ak_ref[...] = proj(C_AK).astype(BF16)
    _store_value_slabs(avt_ref, proj(C_AV))
    ga_ref[...] = _silu(proj(C_AG)).astype(BF16)

    qu = _dot(qn, wuq_ref[...])
    kvu = _dot(kvn, wukv_ref[...])
    cen = v - (_dot(v_hi, gavg) + _dot(v_lo, gavg))
    sq_hi, sq_lo = split_bf16(cen * cen)
    for p in range(MLA_HEADS // 2):
        o = p * GW
        bq_ref[:, o:o + LANES] = (_rope128(qu[:, o:o + LANES], cos, sina, sinb) * qscale).astype(BF16)
        bq_ref[:, o + LANES:o + GW] = (qu[:, o + LANES:o + GW] * qscale).astype(BF16)
    _store_mla_keys(bk_ref, _rope128(proj(C_KR, LANES), cos, sina, sinb).astype(BF16), kvu)
    _store_value_slabs(bvt_ref, kvu[:, 2 * LANES:])

    p_ref[...] = (proj(C_DC) * proj(C_DH)).astype(BF16)
    bgd_ref[...] = (proj(C_DB) * _silu(proj(C_DG))).astype(BF16)

    var = _dot(sq_hi, gavg) + _dot(sq_lo, gavg)
    vn = (cen * lax.rsqrt(var + NORM_EPS) * lng_ref[...]).astype(BF16)
    gb_ref[...] = _silu(proj(C_BG)).astype(BF16)
    u = _gelu(proj(C_CU))
    gate_c = _silu(proj(C_CG))

    for n in range(tm // CM_CHUNK):
        rows = slice(n * CM_CHUNK, (n + 1) * CM_CHUNK)
        vc = vn[rows]
        stacked = jnp.concatenate(
            [jnp.where((lane >= g * CM_GROUP_DIM) & (lane < (g + 1) * CM_GROUP_DIM), vc, jnp.zeros_like(vc))
             for g in range(CM_GROUPS)], axis=0)
        s = bs_ref[...] + _dot(ws_ref[...], stacked)
        mc_ref[rows, :] = (u[rows] * s * gate_c[rows]).astype(BF16)


def _layer_spec(a, layer, **kwargs):
    return pl.BlockSpec((None,) + a.shape[1:], lambda *_: (layer,) + (0,) * (a.ndim - 1), **kwargs)


def _mod_spec(mod, layer, row, part):
    d = mod.shape[-1] // 3
    return pl.BlockSpec((None, None, 1, d), lambda bi, i: (layer, row(bi), 0, part))


def _inproj(x, mod, row, lw, layer, tabs, tm):
    b, l, d = x.shape
    cos, sina, sinb = tabs
    full = lambda a: pl.BlockSpec(a.shape, lambda bi, i: (0,) * a.ndim)
    tok = lambda width: pl.BlockSpec((None, tm, width), lambda bi, i: (bi, i, 0))
    tok_t = pl.BlockSpec((None, VT_ROWS, tm), lambda bi, i: (bi, 0, i))
    pos = pl.BlockSpec((tm, LANES), lambda bi, i: (i, 0))
    widths = (GW, GW, None, GW, 2 * GW, 2 * GW, None, GW, GW, GW, GW)
    consts = (lw["norm_g"], lw["w_in"], lw["qn_g"], lw["w_uq"], lw["kvn_g"], lw["w_ukv"])
    return pl.pallas_call(
        _inproj_kernel,
        grid=(b, l // tm),
        in_specs=[tok(d), _mod_spec(mod, layer, row, 0), _mod_spec(mod, layer, row, 1)]
                 + [_layer_spec(a, layer) for a in consts] + [pos, pos, pos]
                 + [_layer_spec(lw["ln_g"], layer), full(lw["gavg"]), _layer_spec(lw["ws"], layer),
                    _layer_spec(lw["bs"], layer)],
        out_specs=[tok_t if w is None else tok(w) for w in widths],
        out_shape=[jax.ShapeDtypeStruct((b, VT_ROWS, l) if w is None else (b, l, w), BF16) for w in widths],
        compiler_params=_cparams(2),
        name="inproj",
    )(x, mod, mod, *consts, cos, sina, sinb, lw["ln_g"], lw["gavg"], lw["ws"], lw["bs"])


def _inproj_kv(x, mod, row, lw, layer, tm):
    b, l, d = x.shape
    tok = lambda width: pl.BlockSpec((None, tm, width), lambda bi, i: (bi, i, 0))
    tok_t = pl.BlockSpec((None, VT_ROWS, tm), lambda bi, i: (bi, 0, i))
    consts = (lw["norm_g"], lw["w_in"], lw["kvn_g"], lw["w_ukv"])
    vt_shape = jax.ShapeDtypeStruct((b, VT_ROWS, l), BF16)
    return pl.pallas_call(
        _inproj_kv_kernel,
        grid=(b, l // tm),
        in_specs=[tok(d), _mod_spec(mod, layer, row, 0), _mod_spec(mod, layer, row, 1)]
                 + [_layer_spec(a, layer) for a in consts],
        out_specs=[tok(GW), tok_t, tok(2 * GW), tok_t],
        out_shape=[jax.ShapeDtypeStruct((b, l, GW), BF16), vt_shape,
                   jax.ShapeDtypeStruct((b, l, 2 * GW), BF16), vt_shape],
        compiler_params=_cparams(2),
        name="inproj_kv",
    )(x, mod, mod, *consts)


def _lane_mask(ranges, width):
    lane = lax.broadcasted_iota(jnp.int32, (1, width), 1)
    m = None
    for lo, hi in ranges:
        t = (lane >= lo) & (lane < hi)
        m = t if m is None else (m | t)
    return m


def _key_reduce(pieces, combine, reduce):
    by_height = {}
    for t in pieces:
        n = t.shape[0]
        by_height[n] = t if n not in by_height else combine(by_height[n], t)
    out = None
    for t in by_height.values():
        r = reduce(t, axis=0, keepdims=True)
        out = r if out is None else combine(out, r)
    return out


def _softmax_probs(scores, exp_dtype, sums_on_mxu):
    m = _key_reduce(scores, jnp.maximum, jnp.max)
    exps = [jnp.exp2((s - m).astype(exp_dtype)) for s in scores]
    l = None if sums_on_mxu else _key_reduce([e.astype(F32) for e in exps], jnp.add, jnp.sum)
    return [e.astype(BF16) for e in exps], l


def _weighted_values(probs, l, values_t):
    o = None
    for p, vt in zip(probs, values_t):
        pv = _dot(vt, p)
        o = pv if o is None else o + pv
    if l is None:
        l = o[MLA_V_DIM:MLA_V_DIM + 1]
    return o[:MLA_V_DIM] * (1.0 / l)


def _attend_blocks(n_blk, n_heads, scores_fn, values_fn, emit_fn, scr, exp_dtype, sums_on_mxu):
    def block(i, has_next, scores):
        units = [(i, h) for h in range(n_heads)] + ([(i + 1, 0)] if has_next else [])
        outs = []
        for u in range(n_heads):
            nxt = scores_fn(*units[u + 1]) if u + 1 < len(units) else None
            probs, l = _softmax_probs(scores, exp_dtype, sums_on_mxu)
            outs.append(_weighted_values(probs, l, values_fn(i, u)))
            scores = nxt
        emit_fn(i, outs)
        return scores

    scores = scores_fn(0, 0)
    if n_blk == 1:
        block(0, False, scores)
        return

    def stash(scores):
        for ref, s in zip(scr, scores):
            ref[...] = s

    def body(i, carry):
        stash(block(i, True, [ref[...] for ref in scr]))
        return carry

    stash(scores)
    lax.fori_loop(0, n_blk - 1, body, 0)
    block(n_blk - 1, False, [ref[...] for ref in scr])


def _pipeline_scratch(tq, piece_heights):
    return [pltpu.VMEM((n, tq), F32) for n in piece_heights]


def _head_rows(refs, hd):
    return [r[hd * V_SLAB:hd * V_SLAB + MLA_V_DIM, :] for r in refs]


def _head_scores(q_ref, rows, k_refs, head_spec):
    grp, ranges = head_spec
    q = q_ref[rows, grp * GW:(grp + 1) * GW]
    qm = jnp.where(_lane_mask(ranges, GW), q, jnp.zeros_like(q))
    return [_dot_nt(k[:, grp * GW:(grp + 1) * GW], qm) for k in k_refs]


def _attn_kernel(*refs, head_specs, n_src, tr):
    q_ref = refs[0]
    k_refs = refs[1:1 + n_src]
    vt_refs = refs[1 + n_src:1 + 2 * n_src]
    gate_ref, o_ref = refs[1 + 2 * n_src], refs[2 + 2 * n_src]
    scr = refs[3 + 2 * n_src:]
    n_blk = q_ref.shape[0] // tr

    def rows(i):
        start = i * tr
        return pl.ds(start if isinstance(start, int) else pl.multiple_of(start, tr), tr)

    def emit(i, outs):
        o = jnp.concatenate(outs, axis=0).T
        o_ref[rows(i), :] = (o * gate_ref[rows(i), :].astype(F32)).astype(BF16)

    _attend_blocks(n_blk, len(head_specs),
                   lambda i, h: _head_scores(q_ref, rows(i), k_refs, head_specs[h]),
                   lambda i, h: _head_rows(vt_refs, h), emit, scr, F32, False)


def _dense_attention(q, ks, vts, gate, head_specs, tq, tr):
    b, lq, qw = q.shape
    n_src = len(ks)
    kv_spec = lambda a: pl.BlockSpec((None,) + a.shape[1:], lambda bi, i: (bi, 0, 0))
    scratch = _pipeline_scratch(tr, [a.shape[1] for a in ks]) if tq > tr else []
    return pl.pallas_call(
        functools.partial(_attn_kernel, head_specs=head_specs, n_src=n_src, tr=tr),
        grid=(b, lq // tq),
        in_specs=[pl.BlockSpec((None, tq, qw), lambda bi, i: (bi, i, 0))]
                 + [kv_spec(a) for a in ks] + [kv_spec(a) for a in vts]
                 + [pl.BlockSpec((None, tq, GW), lambda bi, i: (bi, i, 0))],
        out_specs=pl.BlockSpec((None, tq, GW), lambda bi, i: (bi, i, 0)),
        out_shape=jax.ShapeDtypeStruct((b, lq, GW), BF16),
        scratch_shapes=scratch,
        compiler_params=_cparams(2),
        name="dense_attention",
    )(q, *ks, *vts, gate)


NA_HEAD_SPECS = tuple((0, ((h * 64, h * 64 + 64),)) for h in range(NA_HEADS))
MLA_HEAD_SPECS = tuple((h // 2, ((32 * (h % 2), 32 * (h % 2) + 32),
                                 (LANES + 64 * (h % 2), LANES + 64 * (h % 2) + 64)))
                       for h in range(MLA_HEADS))


def _na_kernel(q_ref, k0, k1, k2, vt0, vt1, vt2, kc_ref, vct_ref, strip_ref, win_ref, gate_ref,
               o_ref, bias_scr, *scr):
    qb = pl.program_id(0)
    n_b = q_ref.shape[0]
    kw = NA_KCHUNK_ROWS * GRID_W
    t_base = 8 - jnp.where(qb == 0, 0, jnp.where(qb == pl.num_programs(0) - 1, 8, 4))
    for hd in range(NA_HEADS):
        for j in range(NA_KCHUNKS):
            start = pl.multiple_of((t_base + NA_KCHUNK_ROWS * j) * GRID_W, GRID_W)
            keys = slice(j * kw, (j + 1) * kw)
            bias_scr[hd, keys, :] = strip_ref[hd, pl.ds(start, kw), :] + win_ref[keys, :]

    def scores_fn(i, hd):
        q = q_ref[i]
        qm = jnp.where(_lane_mask([(hd * 64, hd * 64 + 64)], GW), q, jnp.zeros_like(q))
        scores = [_dot_nt(k[i], qm) + bias_scr[hd, j * kw:(j + 1) * kw, :]
                  for j, k in enumerate((k0, k1, k2))]
        scores.append(_dot_nt(kc_ref[i], qm))
        return scores

    def values_fn(i, hd):
        return [vt[i, hd * V_SLAB:(hd + 1) * V_SLAB, :] for vt in (vt0, vt1, vt2, vct_ref)]

    def emit(i, outs):
        o = jnp.concatenate(outs, axis=0).T
        o_ref[i] = (o * gate_ref[i].astype(F32)).astype(BF16)

    _attend_blocks(n_b, NA_HEADS, scores_fn, values_fn, emit, scr, BF16, True)


def _na_tables(rpb, rows):
    lead = rpb.shape[:-2]
    rpb = rpb.reshape((-1,) + rpb.shape[-2:])
    n_h = rpb.shape[0]
    n_dr, n_dc = 2 * NA_WIN_ROWS - 1, 2 * NA_WIN_COLS - 1
    tq = NA_QROWS * GRID_W
    kc = np.arange(GRID_W)[:, None]
    qc = np.arange(GRID_W)[None, :]
    cs = np.clip(qc - NA_WIN_COLS // 2, 0, GRID_W - NA_WIN_COLS)
    col_ok = (kc >= cs) & (kc < cs + NA_WIN_COLS)
    ci = np.clip(kc - qc, -(NA_WIN_COLS - 1), NA_WIN_COLS - 1) + NA_WIN_COLS - 1
    sel_c = (ci[None] == np.arange(n_dc)[:, None, None]).astype(np.float32)
    mc = jnp.einsum("hac,ckq->hakq", rpb.astype(F32), jnp.asarray(sel_c),
                    precision=lax.Precision.HIGHEST) * LOG2E
    mc = jnp.tile(mc, (1, 1, 1, NA_QROWS))
    n_strips = 20
    dr = np.arange(n_strips)[:, None] - 8 - np.arange(tq)[None, :] // GRID_W
    sel_a = (dr[None] == (np.arange(n_dr) - (NA_WIN_ROWS - 1))[:, None, None]).astype(np.float32)
    strips = jnp.sum(jnp.asarray(sel_a)[None, :, :, None, :] * mc[:, :, None, :, :], axis=1)
    strips = strips.reshape(lead + (n_strips * GRID_W, tq))

    kr_win = min(NA_WIN_ROWS, rows)
    n_qb = rows // NA_QROWS
    n_krows = NA_KCHUNKS * NA_KCHUNK_ROWS
    rs = np.clip(np.arange(rows) - kr_win // 2, 0, rows - kr_win).reshape(n_qb, NA_QROWS)
    c0 = np.clip(np.arange(n_qb) - 1, 0, rows // NA_KCHUNK_ROWS - NA_KCHUNKS)
    kr = NA_KCHUNK_ROWS * c0[:, None] + np.arange(n_krows)[None, :]
    row_ok = (kr[:, :, None] >= rs[:, None, :]) & (kr[:, :, None] < rs[:, None, :] + kr_win)
    ok = row_ok[:, :, None, :, None] & col_ok[None, None, :, None, :]
    win = np.where(ok, 0.0, NEG_INF).astype(np.float32).reshape(n_qb, n_krows * GRID_W, tq)
    return strips, jnp.asarray(win)


def _neighbourhood_attention(q, k, vt, kc, vct, strips, layer, win, gate):
    b, l, _ = q.shape
    tq = NA_QROWS * GRID_W
    kw = NA_KCHUNK_ROWS * GRID_W
    n_qb = l // tq
    n_kblk = l // kw
    n_b = NA_BATCH_PER_STEP
    assert b % n_b == 0

    def chunk(j):
        return lambda qb: jnp.clip(qb - 1, 0, n_kblk - NA_KCHUNKS) + j

    k_specs = [pl.BlockSpec((n_b, kw, GW), lambda qb, bi, c=chunk(j): (bi, c(qb), 0))
               for j in range(NA_KCHUNKS)]
    vt_specs = [pl.BlockSpec((n_b, VT_ROWS, kw), lambda qb, bi, c=chunk(j): (bi, 0, c(qb)))
                for j in range(NA_KCHUNKS)]
    tok = pl.BlockSpec((n_b, tq, GW), lambda qb, bi: (bi, qb, 0))
    ctx_spec = lambda a: pl.BlockSpec((n_b,) + a.shape[1:], lambda qb, bi: (bi, 0, 0))
    return pl.pallas_call(
        _na_kernel,
        grid=(n_qb, b // n_b),
        in_specs=[tok] + k_specs + vt_specs + [ctx_spec(kc), ctx_spec(vct),
                  _layer_spec(strips, layer, pipeline_mode=pl.Buffered(1)),
                  pl.BlockSpec((None,) + win.shape[1:], lambda qb, bi: (qb, 0, 0)),
                  tok],
        out_specs=tok,
        out_shape=jax.ShapeDtypeStruct((b, l, GW), BF16),
        scratch_shapes=[pltpu.VMEM((NA_HEADS, NA_KCHUNKS * kw, tq), F32)]
                       + _pipeline_scratch(tq, [kw] * NA_KCHUNKS + [kc.shape[1]]),
        compiler_params=_cparams(2),
        name="neighbourhood_attention",
    )(q, *([k] * NA_KCHUNKS), *([vt] * NA_KCHUNKS), kc, vct, strips, win, gate)


def _outproj_kernel(oa_ref, ob_ref, mc_ref, p_ref, pprev_ref, pnext_ref, bgd_ref, scw_ref, w_ref,
                    g_ref, x_ref, fg_ref, o_ref, *, final_norm):
    i = pl.program_id(1)
    tm = x_ref.shape[0]
    p = p_ref[...].astype(F32)
    row = lax.broadcasted_iota(jnp.int32, (tm, 1), 0)
    prev_row = jnp.where(i == 0, 0.0, pprev_ref[HALO - 1:HALO, :].astype(F32))
    next_row = jnp.where(i == pl.num_programs(1) - 1, 0.0, pnext_ref[0:1, :].astype(F32))
    p_dn = jnp.where(row == 0, prev_row, pltpu.roll(p, 1, 0))
    p_up = jnp.where(row == tm - 1, next_row, pltpu.roll(p, tm - 1, 0))
    y = p_dn * scw_ref[0:1, :] + p * scw_ref[1:2, :] + p_up * scw_ref[2:3, :]
    md = (bgd_ref[...].astype(F32) * y).astype(BF16)
    acc = (_dot(oa_ref[...], w_ref[0:GW, :]) + _dot(ob_ref[...], w_ref[GW:2 * GW, :])
           + _dot(mc_ref[...], w_ref[2 * GW:3 * GW, :]) + _dot(md, w_ref[3 * GW:4 * GW, :]))
    xn = x_ref[...] + g_ref[...] * acc
    if final_norm:
        xn = _rms(xn) * fg_ref[...]
    o_ref[...] = xn


def _outproj(oa, ob, mc, p, bgd, sc_w, w_out, layer, mod, row, x, final_g, tm, final_norm):
    b, l, d = x.shape
    n_halo = l // HALO
    per_tile = tm // HALO
    tok = lambda width: pl.BlockSpec((None, tm, width), lambda bi, i: (bi, i, 0))
    full = lambda a: pl.BlockSpec(a.shape, lambda bi, i: (0,) * a.ndim)
    prev = pl.BlockSpec((None, HALO, GW), lambda bi, i: (bi, jnp.maximum(i * per_tile - 1, 0), 0))
    nxt = pl.BlockSpec((None, HALO, GW),
                       lambda bi, i: (bi, jnp.minimum((i + 1) * per_tile, n_halo - 1), 0))
    return pl.pallas_call(
        functools.partial(_outproj_kernel, final_norm=final_norm),
        grid=(b, l // tm),
        in_specs=[tok(GW), tok(GW), tok(GW), tok(GW), prev, nxt, tok(GW), _layer_spec(sc_w, layer),
                  _layer_spec(w_out, layer), _mod_spec(mod, layer, row, 2), tok(d), full(final_g)],
        out_specs=tok(d),
        out_shape=jax.ShapeDtypeStruct((b, l, d), F32),
        compiler_params=_cparams(2),
        name="outproj",
    )(oa, ob, mc, p, p, p, bgd, sc_w, w_out, mod, x, final_g)


def _prep_w_in(w):
    kr = w[..., 1408:1440]
    z = jnp.zeros(w.shape[:-1] + (64,), w.dtype)
    return jnp.concatenate([w[..., :1440], kr, z, w[..., 1440:]], axis=-1).astype(BF16)


def _prep_w_uq(w):
    hd = MLA_NOPE_DIM + MLA_ROPE_DIM
    z = jnp.zeros(w.shape[:-1] + (64,), w.dtype)
    cols = []
    for p in range(MLA_HEADS // 2):
        h0, h1 = 2 * p, 2 * p + 1
        cols += [w[..., hd * h0 + MLA_NOPE_DIM:hd * (h0 + 1)], w[..., hd * h1 + MLA_NOPE_DIM:hd * (h1 + 1)], z,
                 w[..., hd * h0:hd * h0 + MLA_NOPE_DIM], w[..., hd * h1:hd * h1 + MLA_NOPE_DIM]]
    return jnp.concatenate(cols, axis=-1).astype(BF16)


def _prep_w_ukv(w):
    hd = MLA_NOPE_DIM + MLA_V_DIM
    kn = [w[..., hd * h:hd * h + MLA_NOPE_DIM] for h in range(MLA_HEADS)]
    vv = [w[..., hd * h + MLA_NOPE_DIM:hd * (h + 1)] for h in range(MLA_HEADS)]
    return jnp.concatenate(kn + vv, axis=-1).astype(BF16)


def _rope_tables(length):
    nf = MLA_ROPE_DIM // 4
    t = np.arange(length)
    inv = ROPE_BASE ** (-np.arange(nf, dtype=np.float64) / nf)
    ang = np.stack([(t // GRID_W)[:, None] * inv, (t % GRID_W)[:, None] * inv], axis=1)
    cos = np.cos(ang)
    sin = np.sin(ang)
    zero = np.zeros_like(sin)
    cos32 = np.stack([cos, cos], axis=2).reshape(length, MLA_ROPE_DIM)
    sina32 = np.stack([-sin, zero], axis=2).reshape(length, MLA_ROPE_DIM)
    sinb32 = np.stack([zero, sin], axis=2).reshape(length, MLA_ROPE_DIM)
    pad1 = np.ones((length, 64))
    pad0 = np.zeros((length, 64))
    tabs = (np.concatenate([cos32, cos32, pad1], axis=1), np.concatenate([sina32, sina32, pad0], axis=1),
            np.concatenate([sinb32, sinb32, pad0], axis=1))
    return tuple(jnp.asarray(a, F32) for a in tabs)


def _identity_rope_tables(length):
    return (jnp.ones((length, LANES), F32), jnp.zeros((length, LANES), F32),
            jnp.zeros((length, LANES), F32))


def kernel(x, c, ctx, c_ctx, norm_g, w_mod, b_mod, w_in, na_rpb, mla_qn_g, mla_w_uq, mla_kvn_g,
           mla_w_ukv, cm_ln_g, cm_w_s, cm_b_s, sc_w, w_out, final_g):
    b, l, d = x.shape
    lc = ctx.shape[1]
    depth = w_in.shape[0]
    rows = l // GRID_W
    assert l % (NA_QROWS * GRID_W) == 0 and rows >= NA_KCHUNKS * NA_KCHUNK_ROWS and lc % CM_CHUNK == 0
    assert w_in.shape[2] == 3488 and d == 4 * GW

    n_mod = -(-(b + 1) // 8) * 8
    cc = jnp.concatenate([c, c_ctx[None], jnp.zeros((n_mod - b - 1, d), F32)], axis=0)
    mod = _modulation(cc, w_mod, b_mod).reshape(depth, n_mod, 1, 3 * d)
    row_x = lambda bi: bi
    row_c = lambda bi: b

    lw = {
        "norm_g": norm_g.reshape(depth, 1, d),
        "w_in": _prep_w_in(w_in),
        "qn_g": mla_qn_g.reshape(depth, 1, MLA_Q_RANK),
        "w_uq": _prep_w_uq(mla_w_uq),
        "kvn_g": mla_kvn_g.reshape(depth, 1, MLA_KV_RANK),
        "w_ukv": _prep_w_ukv(mla_w_ukv),
        "ln_g": cm_ln_g.reshape(depth, 1, GW),
        "gavg": jnp.asarray(np.kron(np.eye(CM_GROUPS), np.full((CM_GROUP_DIM, CM_GROUP_DIM),
                                                               1.0 / CM_GROUP_DIM)), BF16),
        "ws": jnp.transpose(cm_w_s, (0, 2, 1, 3)).reshape(depth, CM_CHUNK, CM_GROUPS * CM_CHUNK).astype(BF16),
        "bs": jnp.repeat(jnp.transpose(cm_b_s, (0, 2, 1)), CM_GROUP_DIM, axis=2),
    }
    w_out_b = w_out.astype(BF16)
    strips, win = _na_tables(na_rpb, rows)
    rope_x = _rope_tables(l)
    rope_c = _identity_rope_tables(lc)
    fg = final_g.reshape(1, d)

    for layer in range(depth):
        last = layer == depth - 1
        (aq, ak, avt, ga, bq, bk, bvt, gb, mc, pp, bgd) = _inproj(x, mod, row_x, lw, layer, rope_x, tm=512)
        if last:
            akc, avct, bkc, bvct = _inproj_kv(ctx, mod, row_c, lw, layer, tm=lc)
        else:
            (aqc, akc, avct, gac, bqc, bkc, bvct, gbc, mcc, ppc, bgdc) = _inproj(ctx, mod, row_c, lw, layer,
                                                                                 rope_c, tm=lc)

        out_a = _neighbourhood_attention(aq, ak, avt, akc, avct, strips, layer, win, ga)
        out_b = _dense_attention(bq, [bk, bkc], [bvt, bvct], gb, MLA_HEAD_SPECS, tq=l, tr=512)
        x_new = _outproj(out_a, out_b, mc, pp, bgd, sc_w, w_out_b, layer, mod, row_x, x, fg, tm=1024,
                         final_norm=last)
        if not last:
            ctx_a = _dense_attention(aqc, [akc], [avct], gac, NA_HEAD_SPECS, tq=lc, tr=lc)
            ctx_b = _dense_attention(bqc, [bkc], [bvct], gbc, MLA_HEAD_SPECS, tq=lc, tr=lc)
            ctx = _outproj(ctx_a, ctx_b, mcc, ppc, bgdc, sc_w, w_out_b, layer, mod, row_c, ctx, fg, tm=lc,
                           final_norm=False)
        x = x_new
    return x
```

```python
import functools

import numpy as np
import jax
import jax.numpy as jnp
from jax import lax
from jax.experimental import pallas as pl
from jax.experimental.pallas import tpu as pltpu

F32 = jnp.float32
BF16 = jnp.bfloat16

GRID_W = 64
NA_HEADS = 4
NA_HEAD_DIM = 64
NA_WIN_ROWS = 8
NA_WIN_COLS = 16
MLA_HEADS = 4
MLA_NOPE_DIM = 64
MLA_ROPE_DIM = 32
MLA_V_DIM = 64
MLA_Q_RANK = 256
MLA_KV_RANK = 128
CM_GROUPS = 4
CM_GROUP_DIM = 64
CM_CHUNK = 128
SC_TAPS = 3
ROPE_BASE = 10000.0
NORM_EPS = 1e-6
NEG_INF = -1e9
LOG2E = 1.4426950408889634

GW = 256
LANES = 128
NA_QROWS = 4
NA_KCHUNK_ROWS = 4
NA_KCHUNKS = 3
NA_BATCH_PER_STEP = 8
HALO = 16
V_SLAB = 80
VT_ROWS = 4 * V_SLAB
V7X_VMEM_BYTES = 64 * 1024 * 1024
VMEM_LIMIT = V7X_VMEM_BYTES - 8 * 1024 * 1024

C_AQ, C_AK, C_AV, C_AG = 0, 256, 512, 768
C_CQ, C_CKV, C_KR, C_BG = 1024, 1280, 1408, 1536
C_CU, C_CV, C_CG = 1792, 2048, 2304
C_DB, C_DC, C_DH, C_DG = 2560, 2816, 3072, 3328
D_IN_P = 3584


def _silu(x):
    return x / (1.0 + jnp.exp(-x))


def _gelu(x):
    return 0.5 * x * (1.0 + lax.erf(x * 0.7071067811865476))


def _rms(x):
    return x * lax.rsqrt(jnp.mean(x * x, axis=-1, keepdims=True) + NORM_EPS)


def _dot(a, b):
    return jnp.dot(a, b, preferred_element_type=F32)


def _dot_nt(a, b):
    return lax.dot_general(a, b, (((1,), (1,)), ((), ())), preferred_element_type=F32)


def _cparams(n_grid):
    return pltpu.CompilerParams(dimension_semantics=("parallel",) * n_grid,
                                vmem_limit_bytes=VMEM_LIMIT)


def _mod_kernel(c_ref, w_ref, b_ref, o_ref):
    s = _silu(c_ref[...]).astype(BF16)
    o_ref[...] = _dot(s, w_ref[...].astype(BF16)) + b_ref[...]


def _modulation(cc, w_mod, b_mod):
    depth, d, n = w_mod.shape
    r = cc.shape[0]
    tn = 768
    return pl.pallas_call(
        _mod_kernel,
        grid=(depth, n // tn),
        in_specs=[pl.BlockSpec((r, d), lambda l, j: (0, 0)),
                  pl.BlockSpec((None, d, tn), lambda l, j: (l, 0, j)),
                  pl.BlockSpec((None, 1, tn), lambda l, j: (l, 0, j))],
        out_specs=pl.BlockSpec((None, r, tn), lambda l, j: (l, 0, j)),
        out_shape=jax.ShapeDtypeStruct((depth, r, n), F32),
        compiler_params=_cparams(2),
        name="modulation",
    )(cc, w_mod, b_mod.reshape(depth, 1, n))


def _rope128(x, cos, sina, sinb):
    return x * cos + pltpu.roll(x, LANES - 8, 1) * sina + pltpu.roll(x, 8, 1) * sinb


def _modulated(x_ref, sh_ref, sc_ref, ng_ref):
    h = _rms(x_ref[...]) * ng_ref[...]
    return (h * (1.0 + sc_ref[...]) + sh_ref[...]).astype(BF16)


def _store_value_slabs(vt_ref, vals):
    tm = vals.shape[0]
    ones_row = (lax.broadcasted_iota(jnp.int32, (V_SLAB - MLA_V_DIM, tm), 0) == 0).astype(BF16)
    vt = vals.T.astype(BF16)
    for hd in range(NA_HEADS):
        vt_ref[hd * V_SLAB:hd * V_SLAB + MLA_V_DIM, :] = vt[hd * MLA_V_DIM:(hd + 1) * MLA_V_DIM]
        vt_ref[hd * V_SLAB + MLA_V_DIM:(hd + 1) * V_SLAB, :] = ones_row


def _store_mla_keys(bk_ref, kr, kvu):
    for p in range(MLA_HEADS // 2):
        o = p * GW
        bk_ref[:, o:o + LANES] = kr
        bk_ref[:, o + LANES:o + GW] = kvu[:, p * LANES:(p + 1) * LANES].astype(BF16)


def _inproj_kv_kernel(x_ref, sh_ref, sc_ref, ng_ref, w_ref, kvng_ref, wukv_ref,
                      ak_ref, avt_ref, bk_ref, bvt_ref):
    hb = _modulated(x_ref, sh_ref, sc_ref, ng_ref)

    def proj(off, width=GW):
        return _dot(hb, w_ref[:, off:off + width])

    kvn = (_rms(proj(C_CKV, MLA_KV_RANK)) * kvng_ref[...]).astype(BF16)
    ak_ref[...] = proj(C_AK).astype(BF16)
    _store_value_slabs(avt_ref, proj(C_AV))
    kvu = _dot(kvn, wukv_ref[...])
    _store_mla_keys(bk_ref, proj(C_KR, LANES).astype(BF16), kvu)
    _store_value_slabs(bvt_ref, kvu[:, 2 * LANES:])


def _inproj_kernel(x_ref, sh_ref, sc_ref, ng_ref, w_ref, qng_ref, wuq_ref, kvng_ref, wukv_ref,
                   cos_ref, sina_ref, sinb_ref, lng_ref, gavg_ref, ws_ref, bs_ref,
                   aq_ref, ak_ref, avt_ref, ga_ref, bq_ref, bk_ref, bvt_ref, gb_ref, mc_ref, p_ref,
                   bgd_ref):
    tm = x_ref.shape[0]
    hb = _modulated(x_ref, sh_ref, sc_ref, ng_ref)

    def proj(off, width=GW):
        return _dot(hb, w_ref[:, off:off + width])

    cos, sina, sinb = cos_ref[...], sina_ref[...], sinb_ref[...]
    qscale = (MLA_NOPE_DIM + MLA_ROPE_DIM) ** -0.5 * LOG2E
    gavg = gavg_ref[...]
    lane = lax.broadcasted_iota(jnp.int32, (1, GW), 1)

    def split_bf16(t):
        hi = t.astype(BF16)
        return hi, (t - hi.astype(F32)).astype(BF16)

    qn = (_rms(proj(C_CQ)) * qng_ref[...]).astype(BF16)
    kvn = (_rms(proj(C_CKV, MLA_KV_RANK)) * kvng_ref[...]).astype(BF16)
    v = _gelu(proj(C_CV))
    v_hi, v_lo = split_bf16(v)

    aq_ref[...] = (proj(C_AQ) * (NA_HEAD_DIM ** -0.5 * LOG2E)).astype(BF16)
    ak_ref[...] = proj(C_AK).astype(BF16)
    _store_value_slabs(avt_ref, proj(C_AV))
    ga_ref[...] = _silu(proj(C_AG)).astype(BF16)

    qu = _dot(qn, wuq_ref[...])
    kvu = _dot(kvn, wukv_ref[...])
    cen = v - (_dot(v_hi, gavg) + _dot(v_lo, gavg))
    sq_hi, sq_lo = split_bf16(cen * cen)
    for p in range(MLA_HEADS // 2):
        o = p * GW
        bq_ref[:, o:o + LANES] = (_rope128(qu[:, o:o + LANES], cos, sina, sinb) * qscale).astype(BF16)
        bq_ref[:, o + LANES:o + GW] = (qu[:, o + LANES:o + GW] * qscale).astype(BF16)
    _store_mla_keys(bk_ref, _rope128(proj(C_KR, LANES), cos, sina, sinb).astype(BF16), kvu)
    _store_value_slabs(bvt_ref, kvu[:, 2 * LANES:])

    p_ref[...] = (proj(C_DC) * proj(C_DH)).astype(BF16)
    bgd_ref[...] = (proj(C_DB) * _silu(proj(C_DG))).astype(BF16)

    var = _dot(sq_hi, gavg) + _dot(sq_lo, gavg)
    vn = (cen * lax.rsqrt(var + NORM_EPS) * lng_ref[...]).astype(BF16)
    gb_ref[...] = _silu(proj(C_BG)).astype(BF16)
    u = _gelu(proj(C_CU))
    gate_c = _silu(proj(C_CG))

    for n in range(tm // CM_CHUNK):
        rows = slice(n * CM_CHUNK, (n + 1) * CM_CHUNK)
        vc = vn[rows]
        stacked = jnp.concatenate(
            [jnp.where((lane >= g * CM_GROUP_DIM) & (lane < (g + 1) * CM_GROUP_DIM), vc, jnp.zeros_like(vc))
             for g in range(CM_GROUPS)], axis=0)
        s = bs_ref[...] + _dot(ws_ref[...], stacked)
        mc_ref[rows, :] = (u[rows] * s * gate_c[rows]).astype(BF16)


def _layer_spec(a, layer, **kwargs):
    return pl.BlockSpec((None,) + a.shape[1:], lambda *_: (layer,) + (0,) * (a.ndim - 1), **kwargs)


def _mod_spec(mod, layer, row, part):
    d = mod.shape[-1] // 3
    return pl.BlockSpec((None, None, 1, d), lambda bi, i: (layer, row(bi), 0, part))


def _inproj(x, mod, row, lw, layer, tabs, tm):
    b, l, d = x.shape
    cos, sina, sinb = tabs
    full = lambda a: pl.BlockSpec(a.shape, lambda bi, i: (0,) * a.ndim)
    tok = lambda width: pl.BlockSpec((None, tm, width), lambda bi, i: (bi, i, 0))
    tok_t = pl.BlockSpec((None, VT_ROWS, tm), lambda bi, i: (bi, 0, i))
    pos = pl.BlockSpec((tm, LANES), lambda bi, i: (i, 0))
    widths = (GW, GW, None, GW, 2 * GW, 2 * GW, None, GW, GW, GW, GW)
    consts = (lw["norm_g"], lw["w_in"], lw["qn_g"], lw["w_uq"], lw["kvn_g"], lw["w_ukv"])
    return pl.pallas_call(
        _inproj_kernel,
        grid=(b, l // tm),
        in_specs=[tok(d), _mod_spec(mod, layer, row, 0), _mod_spec(mod, layer, row, 1)]
                 + [_layer_spec(a, layer) for a in consts] + [pos, pos, pos]
                 + [_layer_spec(lw["ln_g"], layer), full(lw["gavg"]), _layer_spec(lw["ws"], layer),
                    _layer_spec(lw["bs"], layer)],
        out_specs=[tok_t if w is None else tok(w) for w in widths],
        out_shape=[jax.ShapeDtypeStruct((b, VT_ROWS, l) if w is None else (b, l, w), BF16) for w in widths],
        compiler_params=_cparams(2),
        name="inproj",
    )(x, mod, mod, *consts, cos, sina, sinb, lw["ln_g"], lw["gavg"], lw["ws"], lw["bs"])


def _inproj_kv(x, mod, row, lw, layer, tm):
    b, l, d = x.shape
    tok = lambda width: pl.BlockSpec((None, tm, width), lambda bi, i: (bi, i, 0))
    tok_t = pl.BlockSpec((None, VT_ROWS, tm), lambda bi, i: (bi, 0, i))
    consts = (lw["norm_g"], lw["w_in"], lw["kvn_g"], lw["w_ukv"])
    vt_shape = jax.ShapeDtypeStruct((b, VT_ROWS, l), BF16)
    return pl.pallas_call(
        _inproj_kv_kernel,
        grid=(b, l // tm),
        in_specs=[tok(d), _mod_spec(mod, layer, row, 0), _mod_spec(mod, layer, row, 1)]
                 + [_layer_spec(a, layer) for a in consts],
        out_specs=[tok(GW), tok_t, tok(2 * GW), tok_t],
        out_shape=[jax.ShapeDtypeStruct((b, l, GW), BF16), vt_shape,
                   jax.ShapeDtypeStruct((b, l, 2 * GW), BF16), vt_shape],
        compiler_params=_cparams(2),
        name="inproj_kv",
    )(x, mod, mod, *consts)


def _lane_mask(ranges, width):
    lane = lax.broadcasted_iota(jnp.int32, (1, width), 1)
    m = None
    for lo, hi in ranges:
        t = (lane >= lo) & (lane < hi)
        m = t if m is None else (m | t)
    return m


def _key_reduce(pieces, combine, reduce):
    by_height = {}
    for t in pieces:
        n = t.shape[0]
        by_height[n] = t if n not in by_height else combine(by_height[n], t)
    out = None
    for t in by_height.values():
        r = reduce(t, axis=0, keepdims=True)
        out = r if out is None else combine(out, r)
    return out


def _softmax_probs(scores, exp_dtype, sums_on_mxu):
    m = _key_reduce(scores, jnp.maximum, jnp.max)
    exps = [jnp.exp2((s - m).astype(exp_dtype)) for s in scores]
    l = None if sums_on_mxu else _key_reduce([e.astype(F32) for e in exps], jnp.add, jnp.sum)
    return [e.astype(BF16) for e in exps], l


def _weighted_values(probs, l, values_t):
    o = None
    for p, vt in zip(probs, values_t):
        pv = _dot(vt, p)
        o = pv if o is None else o + pv
    if l is None:
        l = o[MLA_V_DIM:MLA_V_DIM + 1]
    return o[:MLA_V_DIM] * (1.0 / l)


def _attend_blocks(n_blk, n_heads, scores_fn, values_fn, emit_fn, scr, exp_dtype, sums_on_mxu):
    def block(i, has_next, scores):
        units = [(i, h) for h in range(n_heads)] + ([(i + 1, 0)] if has_next else [])
        outs = []
        for u in range(n_heads):
            nxt = scores_fn(*units[u + 1]) if u + 1 < len(units) else None
            probs, l = _softmax_probs(scores, exp_dtype, sums_on_mxu)
            outs.append(_weighted_values(probs, l, values_fn(i, u)))
            scores = nxt
        emit_fn(i, outs)
        return scores

    scores = scores_fn(0, 0)
    if n_blk == 1:
        block(0, False, scores)
        return

    def stash(scores):
        for ref, s in zip(scr, scores):
            ref[...] = s

    def body(i, carry):
        stash(block(i, True, [ref[...] for ref in scr]))
        return carry

    stash(scores)
    lax.fori_loop(0, n_blk - 1, body, 0)
    block(n_blk - 1, False, [ref[...] for ref in scr])


def _pipeline_scratch(tq, piece_heights):
    return [pltpu.VMEM((n, tq), F32) for n in piece_heights]


def _head_rows(refs, hd):
    return [r[hd * V_SLAB:hd * V_SLAB + MLA_V_DIM, :] for r in refs]


def _head_scores(q_ref, rows, k_refs, head_spec):
    grp, ranges = head_spec
    q = q_ref[rows, grp * GW:(grp + 1) * GW]
    qm = jnp.where(_lane_mask(ranges, GW), q, jnp.zeros_like(q))
    return [_dot_nt(k[:, grp * GW:(grp + 1) * GW], qm) for k in k_refs]


def _attn_kernel(*refs, head_specs, n_src, tr):
    q_ref = refs[0]
    k_refs = refs[1:1 + n_src]
    vt_refs = refs[1 + n_src:1 + 2 * n_src]
    gate_ref, o_ref = refs[1 + 2 * n_src], refs[2 + 2 * n_src]
    scr = refs[3 + 2 * n_src:]
    n_blk = q_ref.shape[0] // tr

    def rows(i):
        start = i * tr
        return pl.ds(start if isinstance(start, int) else pl.multiple_of(start, tr), tr)

    def emit(i, outs):
        o = jnp.concatenate(outs, axis=0).T
        o_ref[rows(i), :] = (o * gate_ref[rows(i), :].astype(F32)).astype(BF16)

    _attend_blocks(n_blk, len(head_specs),
                   lambda i, h: _head_scores(q_ref, rows(i), k_refs, head_specs[h]),
                   lambda i, h: _head_rows(vt_refs, h), emit, scr, F32, False)


def _dense_attention(q, ks, vts, gate, head_specs, tq, tr):
    b, lq, qw = q.shape
    n_src = len(ks)
    kv_spec = lambda a: pl.BlockSpec((None,) + a.shape[1:], lambda bi, i: (bi, 0, 0))
    scratch = _pipeline_scratch(tr, [a.shape[1] for a in ks]) if tq > tr else []
    return pl.pallas_call(
        functools.partial(_attn_kernel, head_specs=head_specs, n_src=n_src, tr=tr),
        grid=(b, lq // tq),
        in_specs=[pl.BlockSpec((None, tq, qw), lambda bi, i: (bi, i, 0))]
                 + [kv_spec(a) for a in ks] + [kv_spec(a) for a in vts]
                 + [pl.BlockSpec((None, tq, GW), lambda bi, i: (bi, i, 0))],
        out_specs=pl.BlockSpec((None, tq, GW), lambda bi, i: (bi, i, 0)),
        out_shape=jax.ShapeDtypeStruct((b, lq, GW), BF16),
        scratch_shapes=scratch,
        compiler_params=_cparams(2),
        name="dense_attention",
    )(q, *ks, *vts, gate)


NA_HEAD_SPECS = tuple((0, ((h * 64, h * 64 + 64),)) for h in range(NA_HEADS))
MLA_HEAD_SPECS = tuple((h // 2, ((32 * (h % 2), 32 * (h % 2) + 32),
                                 (LANES + 64 * (h % 2), LANES + 64 * (h % 2) + 64)))
                       for h in range(MLA_HEADS))


def _na_kernel(q_ref, k0, k1, k2, vt0, vt1, vt2, kc_ref, vct_ref, strip_ref, win_ref, gate_ref,
               o_ref, bias_scr, *scr):
    qb = pl.program_id(0)
    n_b = q_ref.shape[0]
    kw = NA_KCHUNK_ROWS * GRID_W
    t_base = 8 - jnp.where(qb == 0, 0, jnp.where(qb == pl.num_programs(0) - 1, 8, 4))
    for hd in range(NA_HEADS):
        for j in range(NA_KCHUNKS):
            strips = strip_ref[hd, pl.ds(t_base + NA_KCHUNK_ROWS * j, NA_KCHUNK_ROWS)]
            keys = slice(j * kw, (j + 1) * kw)
            bias_scr[hd, keys, :] = strips.reshape(kw, strips.shape[-1]) + win_ref[keys, :]

    def scores_fn(i, hd):
        q = q_ref[i]
        qm = jnp.where(_lane_mask([(hd * 64, hd * 64 + 64)], GW), q, jnp.zeros_like(q))
        scores = [_dot_nt(k[i], qm) + bias_scr[hd, j * kw:(j + 1) * kw, :]
                  for j, k in enumerate((k0, k1, k2))]
        scores.append(_dot_nt(kc_ref[i], qm))
        return scores

    def values_fn(i, hd):
        return [vt[i, hd * V_SLAB:(hd + 1) * V_SLAB, :] for vt in (vt0, vt1, vt2, vct_ref)]

    def emit(i, outs):
        o = jnp.concatenate(outs, axis=0).T
        o_ref[i] = (o * gate_ref[i].astype(F32)).astype(BF16)

    _attend_blocks(n_b, NA_HEADS, scores_fn, values_fn, emit, scr, BF16, True)


def _na_tables(rpb, rows):
    lead = rpb.shape[:-2]
    rpb = rpb.reshape((-1,) + rpb.shape[-2:])
    n_h = rpb.shape[0]
    n_dr, n_dc = 2 * NA_WIN_ROWS - 1, 2 * NA_WIN_COLS - 1
    tq = NA_QROWS * GRID_W
    kc = np.arange(GRID_W)[:, None]
    qc = np.arange(GRID_W)[None, :]
    cs = np.clip(qc - NA_WIN_COLS // 2, 0, GRID_W - NA_WIN_COLS)
    col_ok = (kc >= cs) & (kc < cs + NA_WIN_COLS)
    ci = np.clip(kc - qc, -(NA_WIN_COLS - 1), NA_WIN_COLS - 1) + NA_WIN_COLS - 1
    sel_c = (ci[None] == np.arange(n_dc)[:, None, None]).astype(np.float32)
    mc = jnp.einsum("hac,ckq->hakq", rpb.astype(F32), jnp.asarray(sel_c),
                    precision=lax.Precision.HIGHEST) * LOG2E
    mc = jnp.tile(mc, (1, 1, 1, NA_QROWS))
    n_strips = 20
    dr = np.arange(n_strips)[:, None] - 8 - np.arange(tq)[None, :] // GRID_W
    sel_a = (dr[None] == (np.arange(n_dr) - (NA_WIN_ROWS - 1))[:, None, None]).astype(np.float32)
    strips = jnp.sum(jnp.asarray(sel_a)[None, :, :, None, :] * mc[:, :, None, :, :], axis=1)
    strips = strips.reshape(lead + (n_strips, GRID_W, tq))

    kr_win = min(NA_WIN_ROWS, rows)
    n_qb = rows // NA_QROWS
    n_krows = NA_KCHUNKS * NA_KCHUNK_ROWS
    rs = np.clip(np.arange(rows) - kr_win // 2, 0, rows - kr_win).reshape(n_qb, NA_QROWS)
    c0 = np.clip(np.arange(n_qb) - 1, 0, rows // NA_KCHUNK_ROWS - NA_KCHUNKS)
    kr = NA_KCHUNK_ROWS * c0[:, None] + np.arange(n_krows)[None, :]
    row_ok = (kr[:, :, None] >= rs[:, None, :]) & (kr[:, :, None] < rs[:, None, :] + kr_win)
    ok = row_ok[:, :, None, :, None] & col_ok[None, None, :, None, :]
    win = np.where(ok, 0.0, NEG_INF).astype(np.float32).reshape(n_qb, n_krows * GRID_W, tq)
    return strips, jnp.asarray(win)


def _neighbourhood_attention(q, k, vt, kc, vct, strips, layer, win, gate):
    b, l, _ = q.shape
    tq = NA_QROWS * GRID_W
    kw = NA_KCHUNK_ROWS * GRID_W
    n_qb = l // tq
    n_kblk = l // kw
    n_b = NA_BATCH_PER_STEP
    assert b % n_b == 0

    def chunk(j):
        return lambda qb: jnp.clip(qb - 1, 0, n_kblk - NA_KCHUNKS) + j

    k_specs = [pl.BlockSpec((n_b, kw, GW), lambda qb, bi, c=chunk(j): (bi, c(qb), 0))
               for j in range(NA_KCHUNKS)]
    vt_specs = [pl.BlockSpec((n_b, VT_ROWS, kw), lambda qb, bi, c=chunk(j): (bi, 0, c(qb)))
                for j in range(NA_KCHUNKS)]
    tok = pl.BlockSpec((n_b, tq, GW), lambda qb, bi: (bi, qb, 0))
    ctx_spec = lambda a: pl.BlockSpec((n_b,) + a.shape[1:], lambda qb, bi: (bi, 0, 0))
    return pl.pallas_call(
        _na_kernel,
        grid=(n_qb, b // n_b),
        in_specs=[tok] + k_specs + vt_specs + [ctx_spec(kc), ctx_spec(vct),
                  _layer_spec(strips, layer, pipeline_mode=pl.Buffered(1)),
                  pl.BlockSpec((None,) + win.shape[1:], lambda qb, bi: (qb, 0, 0)),
                  tok],
        out_specs=tok,
        out_shape=jax.ShapeDtypeStruct((b, l, GW), BF16),
        scratch_shapes=[pltpu.VMEM((NA_HEADS, NA_KCHUNKS * kw, tq), F32)]
                       + _pipeline_scratch(tq, [kw] * NA_KCHUNKS + [kc.shape[1]]),
        compiler_params=_cparams(2),
        name="neighbourhood_attention",
    )(q, *([k] * NA_KCHUNKS), *([vt] * NA_KCHUNKS), kc, vct, strips, win, gate)


def _outproj_kernel(oa_ref, ob_ref, mc_ref, p_ref, pprev_ref, pnext_ref, bgd_ref, scw_ref, w_ref,
                    g_ref, x_ref, fg_ref, o_ref, *, final_norm):
    i = pl.program_id(1)
    tm = x_ref.shape[0]
    p = p_ref[...].astype(F32)
    row = lax.broadcasted_iota(jnp.int32, (tm, 1), 0)
    prev_row = jnp.where(i == 0, 0.0, pprev_ref[HALO - 1:HALO, :].astype(F32))
    next_row = jnp.where(i == pl.num_programs(1) - 1, 0.0, pnext_ref[0:1, :].astype(F32))
    p_dn = jnp.where(row == 0, prev_row, pltpu.roll(p, 1, 0))
    p_up = jnp.where(row == tm - 1, next_row, pltpu.roll(p, tm - 1, 0))
    y = p_dn * scw_ref[0:1, :] + p * scw_ref[1:2, :] + p_up * scw_ref[2:3, :]
    md = (bgd_ref[...].astype(F32) * y).astype(BF16)
    acc = (_dot(oa_ref[...], w_ref[0:GW, :]) + _dot(ob_ref[...], w_ref[GW:2 * GW, :])
           + _dot(mc_ref[...], w_ref[2 * GW:3 * GW, :]) + _dot(md, w_ref[3 * GW:4 * GW, :]))
    xn = x_ref[...] + g_ref[...] * acc
    if final_norm:
        xn = _rms(xn) * fg_ref[...]
    o_ref[...] = xn


def _outproj(oa, ob, mc, p, bgd, sc_w, w_out, layer, mod, row, x, final_g, tm, final_norm):
    b, l, d = x.shape
    n_halo = l // HALO
    per_tile = tm // HALO
    tok = lambda width: pl.BlockSpec((None, tm, width), lambda bi, i: (bi, i, 0))
    full = lambda a: pl.BlockSpec(a.shape, lambda bi, i: (0,) * a.ndim)
    prev = pl.BlockSpec((None, HALO, GW), lambda bi, i: (bi, jnp.maximum(i * per_tile - 1, 0), 0))
    nxt = pl.BlockSpec((None, HALO, GW),
                       lambda bi, i: (bi, jnp.minimum((i + 1) * per_tile, n_halo - 1), 0))
    return pl.pallas_call(
        functools.partial(_outproj_kernel, final_norm=final_norm),
        grid=(b, l // tm),
        in_specs=[tok(GW), tok(GW), tok(GW), tok(GW), prev, nxt, tok(GW), _layer_spec(sc_w, layer),
                  _layer_spec(w_out, layer), _mod_spec(mod, layer, row, 2), tok(d), full(final_g)],
        out_specs=tok(d),
        out_shape=jax.ShapeDtypeStruct((b, l, d), F32),
        compiler_params=_cparams(2),
        name="outproj",
    )(oa, ob, mc, p, p, p, bgd, sc_w, w_out, mod, x, final_g)


def _prep_w_in(w):
    kr = w[..., 1408:1440]
    z = jnp.zeros(w.shape[:-1] + (64,), w.dtype)
    return jnp.concatenate([w[..., :1440], kr, z, w[..., 1440:]], axis=-1).astype(BF16)


def _prep_w_uq(w):
    hd = MLA_NOPE_DIM + MLA_ROPE_DIM
    z = jnp.zeros(w.shape[:-1] + (64,), w.dtype)
    cols = []
    for p in range(MLA_HEADS // 2):
        h0, h1 = 2 * p, 2 * p + 1
        cols += [w[..., hd * h0 + MLA_NOPE_DIM:hd * (h0 + 1)], w[..., hd * h1 + MLA_NOPE_DIM:hd * (h1 + 1)], z,
                 w[..., hd * h0:hd * h0 + MLA_NOPE_DIM], w[..., hd * h1:hd * h1 + MLA_NOPE_DIM]]
    return jnp.concatenate(cols, axis=-1).astype(BF16)


def _prep_w_ukv(w):
    hd = MLA_NOPE_DIM + MLA_V_DIM
    kn = [w[..., hd * h:hd * h + MLA_NOPE_DIM] for h in range(MLA_HEADS)]
    vv = [w[..., hd * h + MLA_NOPE_DIM:hd * (h + 1)] for h in range(MLA_HEADS)]
    return jnp.concatenate(kn + vv, axis=-1).astype(BF16)


def _rope_tables(length):
    nf = MLA_ROPE_DIM // 4
    t = np.arange(length)
    inv = ROPE_BASE ** (-np.arange(nf, dtype=np.float64) / nf)
    ang = np.stack([(t // GRID_W)[:, None] * inv, (t % GRID_W)[:, None] * inv], axis=1)
    cos = np.cos(ang)
    sin = np.sin(ang)
    zero = np.zeros_like(sin)
    cos32 = np.stack([cos, cos], axis=2).reshape(length, MLA_ROPE_DIM)
    sina32 = np.stack([-sin, zero], axis=2).reshape(length, MLA_ROPE_DIM)
    sinb32 = np.stack([zero, sin], axis=2).reshape(length, MLA_ROPE_DIM)
    pad1 = np.ones((length, 64))
    pad0 = np.zeros((length, 64))
    tabs = (np.concatenate([cos32, cos32, pad1], axis=1), np.concatenate([sina32, sina32, pad0], axis=1),
            np.concatenate([sinb32, sinb32, pad0], axis=1))
    return tuple(jnp.asarray(a, F32) for a in tabs)


def _identity_rope_tables(length):
    return (jnp.ones((length, LANES), F32), jnp.zeros((length, LANES), F32),
            jnp.zeros((length, LANES), F32))


def kernel(x, c, ctx, c_ctx, norm_g, w_mod, b_mod, w_in, na_rpb, mla_qn_g, mla_w_uq, mla_kvn_g,
           mla_w_ukv, cm_ln_g, cm_w_s, cm_b_s, sc_w, w_out, final_g):
    b, l, d = x.shape
    lc = ctx.shape[1]
    depth = w_in.shape[0]
    rows = l // GRID_W
    assert l % (NA_QROWS * GRID_W) == 0 and rows >= NA_KCHUNKS * NA_KCHUNK_ROWS and lc % CM_CHUNK == 0
    assert w_in.shape[2] == 3488 and d == 4 * GW

    n_mod = -(-(b + 1) // 8) * 8
    cc = jnp.concatenate([c, c_ctx[None], jnp.zeros((n_mod - b - 1, d), F32)], axis=0)
    mod = _modulation(cc, w_mod, b_mod).reshape(depth, n_mod, 1, 3 * d)
    row_x = lambda bi: bi
    row_c = lambda bi: b

    lw = {
        "norm_g": norm_g.reshape(depth, 1, d),
        "w_in": _prep_w_in(w_in),
        "qn_g": mla_qn_g.reshape(depth, 1, MLA_Q_RANK),
        "w_uq": _prep_w_uq(mla_w_uq),
        "kvn_g": mla_kvn_g.reshape(depth, 1, MLA_KV_RANK),
        "w_ukv": _prep_w_ukv(mla_w_ukv),
        "ln_g": cm_ln_g.reshape(depth, 1, GW),
        "gavg": jnp.asarray(np.kron(np.eye(CM_GROUPS), np.full((CM_GROUP_DIM, CM_GROUP_DIM),
                                                               1.0 / CM_GROUP_DIM)), BF16),
        "ws": jnp.transpose(cm_w_s, (0, 2, 1, 3)).reshape(depth, CM_CHUNK, CM_GROUPS * CM_CHUNK).astype(BF16),
        "bs": jnp.repeat(jnp.transpose(cm_b_s, (0, 2, 1)), CM_GROUP_DIM, axis=2),
    }
    w_out_b = w_out.astype(BF16)
    strips, win = _na_tables(na_rpb, rows)
    rope_x = _rope_tables(l)
    rope_c = _identity_rope_tables(lc)
    fg = final_g.reshape(1, d)

    for layer in range(depth):
        last = layer == depth - 1
        (aq, ak, avt, ga, bq, bk, bvt, gb, mc, pp, bgd) = _inproj(x, mod, row_x, lw, layer, rope_x, tm=1024)
        if last:
            akc, avct, bkc, bvct = _inproj_kv(ctx, mod, row_c, lw, layer, tm=lc)
        else:
            (aqc, akc, avct, gac, bqc, bkc, bvct, gbc, mcc, ppc, bgdc) = _inproj(ctx, mod, row_c, lw, layer,
                                                                                 rope_c, tm=lc)

        out_a = _neighbourhood_attention(aq, ak, avt, akc, avct, strips, layer, win, ga)
        out_b = _dense_attention(bq, [bk, bkc], [bvt, bvct], gb, MLA_HEAD_SPECS, tq=l, tr=512)
        x_new = _outproj(out_a, out_b, mc, pp, bgd, sc_w, w_out_b, layer, mod, row_x, x, fg, tm=1024,
                         final_norm=last)
        if not last:
            ctx_a = _dense_attention(aqc, [akc], [avct], gac, NA_HEAD_SPECS, tq=lc, tr=lc)
            ctx_b = _dense_attention(bqc, [bkc], [bvct], gbc, MLA_HEAD_SPECS, tq=lc, tr=lc)
            ctx = _outproj(ctx_a, ctx_b, mcc, ppc, bgdc, sc_w, w_out_b, layer, mod, row_c, ctx, fg, tm=lc,
                           final_norm=False)
        x = x_new
    return x
```

```python
import functools

import numpy as np
import jax
import jax.numpy as jnp
from jax import lax
from jax.experimental import pallas as pl
from jax.experimental.pallas import tpu as pltpu

F32 = jnp.float32
BF16 = jnp.bfloat16

GRID_W = 64
NA_HEADS = 4
NA_HEAD_DIM = 64
NA_WIN_ROWS = 8
NA_WIN_COLS = 16
MLA_HEADS = 4
MLA_NOPE_DIM = 64
MLA_ROPE_DIM = 32
MLA_V_DIM = 64
MLA_Q_RANK = 256
MLA_KV_RANK = 128
CM_GROUPS = 4
CM_GROUP_DIM = 64
CM_CHUNK = 128
SC_TAPS = 3
ROPE_BASE = 10000.0
NORM_EPS = 1e-6
NEG_INF = -1e9
LOG2E = 1.4426950408889634

GW = 256
LANES = 128
NA_QROWS = 4
NA_KCHUNK_ROWS = 4
NA_KCHUNKS = 3
NA_BATCH_PER_STEP = 8
HALO = 16
V_SLAB = 80
VT_ROWS = 4 * V_SLAB
V7X_VMEM_BYTES = 64 * 1024 * 1024
VMEM_LIMIT = V7X_VMEM_BYTES - 8 * 1024 * 1024

C_AQ, C_AK, C_AV, C_AG = 0, 256, 512, 768
C_CQ, C_CKV, C_KR, C_BG = 1024, 1280, 1408, 1536
C_CU, C_CV, C_CG = 1792, 2048, 2304
C_DB, C_DC, C_DH, C_DG = 2560, 2816, 3072, 3328
D_IN_P = 3584


def _silu(x):
    return x / (1.0 + jnp.exp(-x))


def _gelu(x):
    return 0.5 * x * (1.0 + lax.erf(x * 0.7071067811865476))


def _rms(x):
    return x * lax.rsqrt(jnp.mean(x * x, axis=-1, keepdims=True) + NORM_EPS)


def _dot(a, b):
    return jnp.dot(a, b, preferred_element_type=F32)


def _dot_nt(a, b):
    return lax.dot_general(a, b, (((1,), (1,)), ((), ())), preferred_element_type=F32)


def _cparams(n_grid):
    return pltpu.CompilerParams(dimension_semantics=("parallel",) * n_grid,
                                vmem_limit_bytes=VMEM_LIMIT)


def _mod_kernel(c_ref, w_ref, b_ref, o_ref):
    s = _silu(c_ref[...]).astype(BF16)
    o_ref[...] = _dot(s, w_ref[...].astype(BF16)) + b_ref[...]


def _modulation(cc, w_mod, b_mod):
    depth, d, n = w_mod.shape
    r = cc.shape[0]
    tn = 768
    return pl.pallas_call(
        _mod_kernel,
        grid=(depth, n // tn),
        in_specs=[pl.BlockSpec((r, d), lambda l, j: (0, 0)),
                  pl.BlockSpec((None, d, tn), lambda l, j: (l, 0, j)),
                  pl.BlockSpec((None, 1, tn), lambda l, j: (l, 0, j))],
        out_specs=pl.BlockSpec((None, r, tn), lambda l, j: (l, 0, j)),
        out_shape=jax.ShapeDtypeStruct((depth, r, n), F32),
        compiler_params=_cparams(2),
        name="modulation",
    )(cc, w_mod, b_mod.reshape(depth, 1, n))


def _rope128(x, cos, sina, sinb):
    return x * cos + pltpu.roll(x, LANES - 8, 1) * sina + pltpu.roll(x, 8, 1) * sinb


def _modulated(x_ref, sh_ref, sc_ref, ng_ref):
    h = _rms(x_ref[...]) * ng_ref[...]
    return (h * (1.0 + sc_ref[...]) + sh_ref[...]).astype(BF16)


def _store_value_slabs(vt_ref, vals):
    tm = vals.shape[0]
    ones_row = (lax.broadcasted_iota(jnp.int32, (V_SLAB - MLA_V_DIM, tm), 0) == 0).astype(BF16)
    vt = vals.T.astype(BF16)
    for hd in range(NA_HEADS):
        vt_ref[hd * V_SLAB:hd * V_SLAB + MLA_V_DIM, :] = vt[hd * MLA_V_DIM:(hd + 1) * MLA_V_DIM]
        vt_ref[hd * V_SLAB + MLA_V_DIM:(hd + 1) * V_SLAB, :] = ones_row


def _store_mla_keys(bk_ref, kr, kvu):
    for p in range(MLA_HEADS // 2):
        o = p * GW
        bk_ref[:, o:o + LANES] = kr
        bk_ref[:, o + LANES:o + GW] = kvu[:, p * LANES:(p + 1) * LANES].astype(BF16)


def _inproj_kv_kernel(x_ref, sh_ref, sc_ref, ng_ref, w_ref, kvng_ref, wukv_ref,
                      ak_ref, avt_ref, bk_ref, bvt_ref):
    hb = _modulated(x_ref, sh_ref, sc_ref, ng_ref)

    def proj(off, width=GW):
        return _dot(hb, w_ref[:, off:off + width])

    kvn = (_rms(proj(C_CKV, MLA_KV_RANK)) * kvng_ref[...]).astype(BF16)
    ak_ref[...] = proj(C_AK).astype(BF16)
    _store_value_slabs(avt_ref, proj(C_AV))
    kvu = _dot(kvn, wukv_ref[...])
    _store_mla_keys(bk_ref, proj(C_KR, LANES).astype(BF16), kvu)
    _store_value_slabs(bvt_ref, kvu[:, 2 * LANES:])


def _inproj_kernel(x_ref, sh_ref, sc_ref, ng_ref, w_ref, qng_ref, wuq_ref, kvng_ref, wukv_ref,
                   cos_ref, sina_ref, sinb_ref, lng_ref, gavg_ref, ws_ref, bs_ref,
                   aq_ref, ak_ref, avt_ref, ga_ref, bq_ref, bk_ref, bvt_ref, gb_ref, mc_ref, p_ref,
                   bgd_ref):
    tm = x_ref.shape[0]
    hb = _modulated(x_ref, sh_ref, sc_ref, ng_ref)

    def proj(off, width=GW):
        return _dot(hb, w_ref[:, off:off + width])

    cos, sina, sinb = cos_ref[...], sina_ref[...], sinb_ref[...]
    qscale = (MLA_NOPE_DIM + MLA_ROPE_DIM) ** -0.5 * LOG2E
    gavg = gavg_ref[...]
    lane = lax.broadcasted_iota(jnp.int32, (1, GW), 1)

    def split_bf16(t):
        hi = t.astype(BF16)
        return hi, (t - hi.astype(F32)).astype(BF16)

    qn = (_rms(proj(C_CQ)) * qng_ref[...]).astype(BF16)
    kvn = (_rms(proj(C_CKV, MLA_KV_RANK)) * kvng_ref[...]).astype(BF16)
    v = _gelu(proj(C_CV))
    v_hi, v_lo = split_bf16(v)

    aq_ref[...] = (proj(C_AQ) * (NA_HEAD_DIM ** -0.5 * LOG2E)).astype(BF16)
    ak_ref[...] = proj(C_AK).astype(BF16)
    _store_value_slabs(avt_ref, proj(C_AV))
    ga_ref[...] = _silu(proj(C_AG)).astype(BF16)

    qu = _dot(qn, wuq_ref[...])
    kvu = _dot(kvn, wukv_ref[...])
    cen = v - (_dot(v_hi, gavg) + _dot(v_lo, gavg))
    sq_hi, sq_lo = split_bf16(cen * cen)
    for p in range(MLA_HEADS // 2):
        o = p * GW
        bq_ref[:, o:o + LANES] = (_rope128(qu[:, o:o + LANES], cos, sina, sinb) * qscale).astype(BF16)
        bq_ref[:, o + LANES:o + GW] = (qu[:, o + LANES:o + GW] * qscale).astype(BF16)
    _store_mla_keys(bk_ref, _rope128(proj(C_KR, LANES), cos, sina, sinb).astype(BF16), kvu)
    _store_value_slabs(bvt_ref, kvu[:, 2 * LANES:])

    p_ref[...] = (proj(C_DC) * proj(C_DH)).astype(BF16)
    bgd_ref[...] = (proj(C_DB) * _silu(proj(C_DG))).astype(BF16)

    var = _dot(sq_hi, gavg) + _dot(sq_lo, gavg)
    vn = (cen * lax.rsqrt(var + NORM_EPS) * lng_ref[...]).astype(BF16)
    gb_ref[...] = _silu(proj(C_BG)).astype(BF16)
    u = _gelu(proj(C_CU))
    gate_c = _silu(proj(C_CG))

    for n in range(tm // CM_CHUNK):
        rows = slice(n * CM_CHUNK, (n + 1) * CM_CHUNK)
        vc = vn[rows]
        stacked = jnp.concatenate(
            [jnp.where((lane >= g * CM_GROUP_DIM) & (lane < (g + 1) * CM_GROUP_DIM), vc, jnp.zeros_like(vc))
             for g in range(CM_GROUPS)], axis=0)
        s = bs_ref[...] + _dot(ws_ref[...], stacked)
        mc_ref[rows, :] = (u[rows] * s * gate_c[rows]).astype(BF16)


def _layer_spec(a, layer, **kwargs):
    return pl.BlockSpec((None,) + a.shape[1:], lambda *_: (layer,) + (0,) * (a.ndim - 1), **kwargs)


def _mod_spec(mod, layer, row, part):
    d = mod.shape[-1] // 3
    return pl.BlockSpec((None, None, 1, d), lambda bi, i: (layer, row(bi), 0, part))


def _inproj(x, mod, row, lw, layer, tabs, tm):
    b, l, d = x.shape
    cos, sina, sinb = tabs
    full = lambda a: pl.BlockSpec(a.shape, lambda bi, i: (0,) * a.ndim)
    tok = lambda width: pl.BlockSpec((None, tm, width), lambda bi, i: (bi, i, 0))
    tok_t = pl.BlockSpec((None, VT_ROWS, tm), lambda bi, i: (bi, 0, i))
    pos = pl.BlockSpec((tm, LANES), lambda bi, i: (i, 0))
    widths = (GW, GW, None, GW, 2 * GW, 2 * GW, None, GW, GW, GW, GW)
    consts = (lw["norm_g"], lw["w_in"], lw["qn_g"], lw["w_uq"], lw["kvn_g"], lw["w_ukv"])
    return pl.pallas_call(
        _inproj_kernel,
        grid=(b, l // tm),
        in_specs=[tok(d), _mod_spec(mod, layer, row, 0), _mod_spec(mod, layer, row, 1)]
                 + [_layer_spec(a, layer) for a in consts] + [pos, pos, pos]
                 + [_layer_spec(lw["ln_g"], layer), full(lw["gavg"]), _layer_spec(lw["ws"], layer),
                    _layer_spec(lw["bs"], layer)],
        out_specs=[tok_t if w is None else tok(w) for w in widths],
        out_shape=[jax.ShapeDtypeStruct((b, VT_ROWS, l) if w is None else (b, l, w), BF16) for w in widths],
        compiler_params=_cparams(2),
        name="inproj",
    )(x, mod, mod, *consts, cos, sina, sinb, lw["ln_g"], lw["gavg"], lw["ws"], lw["bs"])


def _inproj_kv(x, mod, row, lw, layer, tm):
    b, l, d = x.shape
    tok = lambda width: pl.BlockSpec((None, tm, width), lambda bi, i: (bi, i, 0))
    tok_t = pl.BlockSpec((None, VT_ROWS, tm), lambda bi, i: (bi, 0, i))
    consts = (lw["norm_g"], lw["w_in"], lw["kvn_g"], lw["w_ukv"])
    vt_shape = jax.ShapeDtypeStruct((b, VT_ROWS, l), BF16)
    return pl.pallas_call(
        _inproj_kv_kernel,
        grid=(b, l // tm),
        in_specs=[tok(d), _mod_spec(mod, layer, row, 0), _mod_spec(mod, layer, row, 1)]
                 + [_layer_spec(a, layer) for a in consts],
        out_specs=[tok(GW), tok_t, tok(2 * GW), tok_t],
        out_shape=[jax.ShapeDtypeStruct((b, l, GW), BF16), vt_shape,
                   jax.ShapeDtypeStruct((b, l, 2 * GW), BF16), vt_shape],
        compiler_params=_cparams(2),
        name="inproj_kv",
    )(x, mod, mod, *consts)


def _lane_mask(ranges, width):
    lane = lax.broadcasted_iota(jnp.int32, (1, width), 1)
    m = None
    for lo, hi in ranges:
        t = (lane >= lo) & (lane < hi)
        m = t if m is None else (m | t)
    return m


def _key_reduce(pieces, combine, reduce):
    by_height = {}
    for t in pieces:
        n = t.shape[0]
        by_height[n] = t if n not in by_height else combine(by_height[n], t)
    out = None
    for t in by_height.values():
        r = reduce(t, axis=0, keepdims=True)
        out = r if out is None else combine(out, r)
    return out


def _softmax_probs(scores, exp_dtype, sums_on_mxu):
    m = _key_reduce(scores, jnp.maximum, jnp.max)
    exps = [jnp.exp2((s - m).astype(exp_dtype)) for s in scores]
    l = None if sums_on_mxu else _key_reduce([e.astype(F32) for e in exps], jnp.add, jnp.sum)
    return [e.astype(BF16) for e in exps], l


def _weighted_values(probs, l, values_t):
    o = None
    for p, vt in zip(probs, values_t):
        pv = _dot(vt, p)
        o = pv if o is None else o + pv
    if l is None:
        l = o[MLA_V_DIM:MLA_V_DIM + 1]
    return o[:MLA_V_DIM] * (1.0 / l)


def _attend_blocks(n_blk, n_heads, scores_fn, values_fn, emit_fn, scr, exp_dtype, sums_on_mxu):
    def block(i, has_next, scores):
        units = [(i, h) for h in range(n_heads)] + ([(i + 1, 0)] if has_next else [])
        outs = []
        for u in range(n_heads):
            nxt = scores_fn(*units[u + 1]) if u + 1 < len(units) else None
            probs, l = _softmax_probs(scores, exp_dtype, sums_on_mxu)
            outs.append(_weighted_values(probs, l, values_fn(i, u)))
            scores = nxt
        emit_fn(i, outs)
        return scores

    scores = scores_fn(0, 0)
    if n_blk == 1:
        block(0, False, scores)
        return

    def stash(scores):
        for ref, s in zip(scr, scores):
            ref[...] = s

    def body(i, carry):
        stash(block(i, True, [ref[...] for ref in scr]))
        return carry

    stash(scores)
    lax.fori_loop(0, n_blk - 1, body, 0)
    block(n_blk - 1, False, [ref[...] for ref in scr])


def _pipeline_scratch(tq, piece_heights):
    return [pltpu.VMEM((n, tq), F32) for n in piece_heights]


def _head_rows(refs, hd):
    return [r[hd * V_SLAB:hd * V_SLAB + MLA_V_DIM, :] for r in refs]


def _head_scores(q_ref, rows, k_refs, head_spec):
    grp, ranges = head_spec
    q = q_ref[rows, grp * GW:(grp + 1) * GW]
    qm = jnp.where(_lane_mask(ranges, GW), q, jnp.zeros_like(q))
    return [_dot_nt(k[:, grp * GW:(grp + 1) * GW], qm) for k in k_refs]


def _attn_kernel(*refs, head_specs, n_src, tr):
    q_ref = refs[0]
    k_refs = refs[1:1 + n_src]
    vt_refs = refs[1 + n_src:1 + 2 * n_src]
    gate_ref, o_ref = refs[1 + 2 * n_src], refs[2 + 2 * n_src]
    scr = refs[3 + 2 * n_src:]
    n_blk = q_ref.shape[0] // tr

    def rows(i):
        start = i * tr
        return pl.ds(start if isinstance(start, int) else pl.multiple_of(start, tr), tr)

    def emit(i, outs):
        o = jnp.concatenate(outs, axis=0).T
        o_ref[rows(i), :] = (o * gate_ref[rows(i), :].astype(F32)).astype(BF16)

    _attend_blocks(n_blk, len(head_specs),
                   lambda i, h: _head_scores(q_ref, rows(i), k_refs, head_specs[h]),
                   lambda i, h: _head_rows(vt_refs, h), emit, scr, F32, False)


def _dense_attention(q, ks, vts, gate, head_specs, tq, tr):
    b, lq, qw = q.shape
    n_src = len(ks)
    kv_spec = lambda a: pl.BlockSpec((None,) + a.shape[1:], lambda bi, i: (bi, 0, 0))
    scratch = _pipeline_scratch(tr, [a.shape[1] for a in ks]) if tq > tr else []
    return pl.pallas_call(
        functools.partial(_attn_kernel, head_specs=head_specs, n_src=n_src, tr=tr),
        grid=(b, lq // tq),
        in_specs=[pl.BlockSpec((None, tq, qw), lambda bi, i: (bi, i, 0))]
                 + [kv_spec(a) for a in ks] + [kv_spec(a) for a in vts]
                 + [pl.BlockSpec((None, tq, GW), lambda bi, i: (bi, i, 0))],
        out_specs=pl.BlockSpec((None, tq, GW), lambda bi, i: (bi, i, 0)),
        out_shape=jax.ShapeDtypeStruct((b, lq, GW), BF16),
        scratch_shapes=scratch,
        compiler_params=_cparams(2),
        name="dense_attention",
    )(q, *ks, *vts, gate)


NA_HEAD_SPECS = tuple((0, ((h * 64, h * 64 + 64),)) for h in range(NA_HEADS))
MLA_HEAD_SPECS = tuple((h // 2, ((32 * (h % 2), 32 * (h % 2) + 32),
                                 (LANES + 64 * (h % 2), LANES + 64 * (h % 2) + 64)))
                       for h in range(MLA_HEADS))


def _na_kernel(q_ref, k0, k1, k2, vt0, vt1, vt2, kc_ref, vct_ref, bias_ref, win_ref, gate_ref,
               o_ref, bias_scr, *scr):
    qb = pl.program_id(0)
    n_b = q_ref.shape[0]
    kw = NA_KCHUNK_ROWS * GRID_W
    d0 = -jnp.where(qb == 0, 0, jnp.where(qb == pl.num_programs(0) - 1, 8, 4))
    q_row = lax.broadcasted_iota(jnp.int32, (1, q_ref.shape[1]), 1) // GRID_W
    n_dr = bias_ref.shape[1]
    for hd in range(NA_HEADS):
        for j in range(NA_KCHUNKS):
            for i in range(NA_KCHUNK_ROWS):
                blk = None
                for r in range(NA_QROWS):
                    a = jnp.clip(d0 + NA_KCHUNK_ROWS * j + i - r + NA_WIN_ROWS - 1, 0, n_dr - 1)
                    piece = bias_ref[hd, a]
                    blk = piece if blk is None else jnp.where(q_row == r, piece, blk)
                keys = slice(j * kw + i * GRID_W, j * kw + (i + 1) * GRID_W)
                bias_scr[hd, keys, :] = blk + win_ref[keys, :]

    def scores_fn(i, hd):
        q = q_ref[i]
        qm = jnp.where(_lane_mask([(hd * 64, hd * 64 + 64)], GW), q, jnp.zeros_like(q))
        scores = [_dot_nt(k[i], qm) + bias_scr[hd, j * kw:(j + 1) * kw, :]
                  for j, k in enumerate((k0, k1, k2))]
        scores.append(_dot_nt(kc_ref[i], qm))
        return scores

    def values_fn(i, hd):
        return [vt[i, hd * V_SLAB:(hd + 1) * V_SLAB, :] for vt in (vt0, vt1, vt2, vct_ref)]

    def emit(i, outs):
        o = jnp.concatenate(outs, axis=0).T
        o_ref[i] = (o * gate_ref[i].astype(F32)).astype(BF16)

    _attend_blocks(n_b, NA_HEADS, scores_fn, values_fn, emit, scr, BF16, True)


def _na_tables(rpb, rows):
    n_dc = 2 * NA_WIN_COLS - 1
    tq = NA_QROWS * GRID_W
    kc = np.arange(GRID_W)[:, None]
    qc = np.arange(GRID_W)[None, :]
    cs = np.clip(qc - NA_WIN_COLS // 2, 0, GRID_W - NA_WIN_COLS)
    col_ok = (kc >= cs) & (kc < cs + NA_WIN_COLS)
    ci = np.clip(kc - qc, -(NA_WIN_COLS - 1), NA_WIN_COLS - 1) + NA_WIN_COLS - 1
    sel_c = (ci[None] == np.arange(n_dc)[:, None, None]).astype(np.float32)
    sel_c = np.tile(sel_c, (1, 1, NA_QROWS))
    bias = jnp.einsum("...ac,ckq->...akq", rpb.astype(F32), jnp.asarray(sel_c),
                      precision=lax.Precision.HIGHEST) * LOG2E

    kr_win = min(NA_WIN_ROWS, rows)
    n_qb = rows // NA_QROWS
    n_krows = NA_KCHUNKS * NA_KCHUNK_ROWS
    rs = np.clip(np.arange(rows) - kr_win // 2, 0, rows - kr_win).reshape(n_qb, NA_QROWS)
    c0 = np.clip(np.arange(n_qb) - 1, 0, rows // NA_KCHUNK_ROWS - NA_KCHUNKS)
    kr = NA_KCHUNK_ROWS * c0[:, None] + np.arange(n_krows)[None, :]
    row_ok = (kr[:, :, None] >= rs[:, None, :]) & (kr[:, :, None] < rs[:, None, :] + kr_win)
    ok = row_ok[:, :, None, :, None] & col_ok[None, None, :, None, :]
    win = np.where(ok, 0.0, NEG_INF).astype(np.float32).reshape(n_qb, n_krows * GRID_W, tq)
    return bias, jnp.asarray(win)


def _neighbourhood_attention(q, k, vt, kc, vct, bias, layer, win, gate):
    b, l, _ = q.shape
    tq = NA_QROWS * GRID_W
    kw = NA_KCHUNK_ROWS * GRID_W
    n_qb = l // tq
    n_kblk = l // kw
    n_b = NA_BATCH_PER_STEP
    assert b % n_b == 0

    def chunk(j):
        return lambda qb: jnp.clip(qb - 1, 0, n_kblk - NA_KCHUNKS) + j

    k_specs = [pl.BlockSpec((n_b, kw, GW), lambda qb, bi, c=chunk(j): (bi, c(qb), 0))
               for j in range(NA_KCHUNKS)]
    vt_specs = [pl.BlockSpec((n_b, VT_ROWS, kw), lambda qb, bi, c=chunk(j): (bi, 0, c(qb)))
                for j in range(NA_KCHUNKS)]
    tok = pl.BlockSpec((n_b, tq, GW), lambda qb, bi: (bi, qb, 0))
    ctx_spec = lambda a: pl.BlockSpec((n_b,) + a.shape[1:], lambda qb, bi: (bi, 0, 0))
    return pl.pallas_call(
        _na_kernel,
        grid=(n_qb, b // n_b),
        in_specs=[tok] + k_specs + vt_specs + [ctx_spec(kc), ctx_spec(vct),
                  _layer_spec(bias, layer, pipeline_mode=pl.Buffered(1)),
                  pl.BlockSpec((None,) + win.shape[1:], lambda qb, bi: (qb, 0, 0)),
                  tok],
        out_specs=tok,
        out_shape=jax.ShapeDtypeStruct((b, l, GW), BF16),
        scratch_shapes=[pltpu.VMEM((NA_HEADS, NA_KCHUNKS * kw, tq), F32)]
                       + _pipeline_scratch(tq, [kw] * NA_KCHUNKS + [kc.shape[1]]),
        compiler_params=_cparams(2),
        name="neighbourhood_attention",
    )(q, *([k] * NA_KCHUNKS), *([vt] * NA_KCHUNKS), kc, vct, bias, win, gate)


def _outproj_kernel(oa_ref, ob_ref, mc_ref, p_ref, pprev_ref, pnext_ref, bgd_ref, scw_ref, w_ref,
                    g_ref, x_ref, fg_ref, o_ref, *, final_norm):
    i = pl.program_id(1)
    tm = x_ref.shape[0]
    p = p_ref[...].astype(F32)
    row = lax.broadcasted_iota(jnp.int32, (tm, 1), 0)
    prev_row = jnp.where(i == 0, 0.0, pprev_ref[HALO - 1:HALO, :].astype(F32))
    next_row = jnp.where(i == pl.num_programs(1) - 1, 0.0, pnext_ref[0:1, :].astype(F32))
    p_dn = jnp.where(row == 0, prev_row, pltpu.roll(p, 1, 0))
    p_up = jnp.where(row == tm - 1, next_row, pltpu.roll(p, tm - 1, 0))
    y = p_dn * scw_ref[0:1, :] + p * scw_ref[1:2, :] + p_up * scw_ref[2:3, :]
    md = (bgd_ref[...].astype(F32) * y).astype(BF16)
    acc = (_dot(oa_ref[...], w_ref[0:GW, :]) + _dot(ob_ref[...], w_ref[GW:2 * GW, :])
           + _dot(mc_ref[...], w_ref[2 * GW:3 * GW, :]) + _dot(md, w_ref[3 * GW:4 * GW, :]))
    xn = x_ref[...] + g_ref[...] * acc
    if final_norm:
        xn = _rms(xn) * fg_ref[...]
    o_ref[...] = xn


def _outproj(oa, ob, mc, p, bgd, sc_w, w_out, layer, mod, row, x, final_g, tm, final_norm):
    b, l, d = x.shape
    n_halo = l // HALO
    per_tile = tm // HALO
    tok = lambda width: pl.BlockSpec((None, tm, width), lambda bi, i: (bi, i, 0))
    full = lambda a: pl.BlockSpec(a.shape, lambda bi, i: (0,) * a.ndim)
    prev = pl.BlockSpec((None, HALO, GW), lambda bi, i: (bi, jnp.maximum(i * per_tile - 1, 0), 0))
    nxt = pl.BlockSpec((None, HALO, GW),
                       lambda bi, i: (bi, jnp.minimum((i + 1) * per_tile, n_halo - 1), 0))
    return pl.pallas_call(
        functools.partial(_outproj_kernel, final_norm=final_norm),
        grid=(b, l // tm),
        in_specs=[tok(GW), tok(GW), tok(GW), tok(GW), prev, nxt, tok(GW), _layer_spec(sc_w, layer),
                  _layer_spec(w_out, layer), _mod_spec(mod, layer, row, 2), tok(d), full(final_g)],
        out_specs=tok(d),
        out_shape=jax.ShapeDtypeStruct((b, l, d), F32),
        compiler_params=_cparams(2),
        name="outproj",
    )(oa, ob, mc, p, p, p, bgd, sc_w, w_out, mod, x, final_g)


def _prep_w_in(w):
    w = w.astype(BF16)
    kr = w[..., 1408:1440]
    z = jnp.zeros(w.shape[:-1] + (64,), w.dtype)
    return jnp.concatenate([w[..., :1440], kr, z, w[..., 1440:]], axis=-1)


def _prep_w_uq(w):
    hd = MLA_NOPE_DIM + MLA_ROPE_DIM
    z = jnp.zeros(w.shape[:-1] + (64,), w.dtype)
    cols = []
    for p in range(MLA_HEADS // 2):
        h0, h1 = 2 * p, 2 * p + 1
        cols += [w[..., hd * h0 + MLA_NOPE_DIM:hd * (h0 + 1)], w[..., hd * h1 + MLA_NOPE_DIM:hd * (h1 + 1)], z,
                 w[..., hd * h0:hd * h0 + MLA_NOPE_DIM], w[..., hd * h1:hd * h1 + MLA_NOPE_DIM]]
    return jnp.concatenate(cols, axis=-1).astype(BF16)


def _prep_w_ukv(w):
    hd = MLA_NOPE_DIM + MLA_V_DIM
    kn = [w[..., hd * h:hd * h + MLA_NOPE_DIM] for h in range(MLA_HEADS)]
    vv = [w[..., hd * h + MLA_NOPE_DIM:hd * (h + 1)] for h in range(MLA_HEADS)]
    return jnp.concatenate(kn + vv, axis=-1).astype(BF16)


def _rope_tables(length):
    nf = MLA_ROPE_DIM // 4
    t = np.arange(length)
    inv = ROPE_BASE ** (-np.arange(nf, dtype=np.float64) / nf)
    ang = np.stack([(t // GRID_W)[:, None] * inv, (t % GRID_W)[:, None] * inv], axis=1)
    cos = np.cos(ang)
    sin = np.sin(ang)
    zero = np.zeros_like(sin)
    cos32 = np.stack([cos, cos], axis=2).reshape(length, MLA_ROPE_DIM)
    sina32 = np.stack([-sin, zero], axis=2).reshape(length, MLA_ROPE_DIM)
    sinb32 = np.stack([zero, sin], axis=2).reshape(length, MLA_ROPE_DIM)
    pad1 = np.ones((length, 64))
    pad0 = np.zeros((length, 64))
    tabs = (np.concatenate([cos32, cos32, pad1], axis=1), np.concatenate([sina32, sina32, pad0], axis=1),
            np.concatenate([sinb32, sinb32, pad0], axis=1))
    return tuple(jnp.asarray(a, F32) for a in tabs)


def _identity_rope_tables(length):
    return (jnp.ones((length, LANES), F32), jnp.zeros((length, LANES), F32),
            jnp.zeros((length, LANES), F32))


def kernel(x, c, ctx, c_ctx, norm_g, w_mod, b_mod, w_in, na_rpb, mla_qn_g, mla_w_uq, mla_kvn_g,
           mla_w_ukv, cm_ln_g, cm_w_s, cm_b_s, sc_w, w_out, final_g):
    b, l, d = x.shape
    lc = ctx.shape[1]
    depth = w_in.shape[0]
    rows = l // GRID_W
    assert l % (NA_QROWS * GRID_W) == 0 and rows >= NA_KCHUNKS * NA_KCHUNK_ROWS and lc % CM_CHUNK == 0
    assert w_in.shape[2] == 3488 and d == 4 * GW

    n_mod = -(-(b + 1) // 8) * 8
    cc = jnp.concatenate([c, c_ctx[None], jnp.zeros((n_mod - b - 1, d), F32)], axis=0)
    mod = _modulation(cc, w_mod, b_mod).reshape(depth, n_mod, 1, 3 * d)
    row_x = lambda bi: bi
    row_c = lambda bi: b

    lw = {
        "norm_g": norm_g.reshape(depth, 1, d),
        "w_in": _prep_w_in(w_in),
        "qn_g": mla_qn_g.reshape(depth, 1, MLA_Q_RANK),
        "w_uq": _prep_w_uq(mla_w_uq),
        "kvn_g": mla_kvn_g.reshape(depth, 1, MLA_KV_RANK),
        "w_ukv": _prep_w_ukv(mla_w_ukv),
        "ln_g": cm_ln_g.reshape(depth, 1, GW),
        "gavg": jnp.asarray(np.kron(np.eye(CM_GROUPS), np.full((CM_GROUP_DIM, CM_GROUP_DIM),
                                                               1.0 / CM_GROUP_DIM)), BF16),
        "ws": jnp.transpose(cm_w_s, (0, 2, 1, 3)).reshape(depth, CM_CHUNK, CM_GROUPS * CM_CHUNK).astype(BF16),
        "bs": jnp.repeat(jnp.transpose(cm_b_s, (0, 2, 1)), CM_GROUP_DIM, axis=2),
    }
    w_out_b = w_out.astype(BF16)
    na_bias, win = _na_tables(na_rpb, rows)
    rope_x = _rope_tables(l)
    rope_c = _identity_rope_tables(lc)
    fg = final_g.reshape(1, d)

    for layer in range(depth):
        last = layer == depth - 1
        (aq, ak, avt, ga, bq, bk, bvt, gb, mc, pp, bgd) = _inproj(x, mod, row_x, lw, layer, rope_x, tm=1024)
        if last:
            akc, avct, bkc, bvct = _inproj_kv(ctx, mod, row_c, lw, layer, tm=lc)
        else:
            (aqc, akc, avct, gac, bqc, bkc, bvct, gbc, mcc, ppc, bgdc) = _inproj(ctx, mod, row_c, lw, layer,
                                                                                 rope_c, tm=lc)

        out_a = _neighbourhood_attention(aq, ak, avt, akc, avct, na_bias, layer, win, ga)
        out_b = _dense_attention(bq, [bk, bkc], [bvt, bvct], gb, MLA_HEAD_SPECS, tq=l, tr=512)
        x_new = _outproj(out_a, out_b, mc, pp, bgd, sc_w, w_out_b, layer, mod, row_x, x, fg, tm=1024,
                         final_norm=last)
        if not last:
            ctx_a = _dense_attention(aqc, [akc], [avct], gac, NA_HEAD_SPECS, tq=lc, tr=lc)
            ctx_b = _dense_attention(bqc, [bkc], [bvct], gbc, MLA_HEAD_SPECS, tq=lc, tr=lc)
            ctx = _outproj(ctx_a, ctx_b, mcc, ppc, bgdc, sc_w, w_out_b, layer, mod, row_c, ctx, fg, tm=lc,
                           final_norm=False)
        x = x_new
    return x
```

```python
import functools

import numpy as np
import jax
import jax.numpy as jnp
from jax import lax
from jax.experimental import pallas as pl
from jax.experimental.pallas import tpu as pltpu

F32 = jnp.float32
BF16 = jnp.bfloat16

GRID_W = 64
NA_HEADS = 4
NA_HEAD_DIM = 64
NA_WIN_ROWS = 8
NA_WIN_COLS = 16
MLA_HEADS = 4
MLA_NOPE_DIM = 64
MLA_ROPE_DIM = 32
MLA_V_DIM = 64
MLA_Q_RANK = 256
MLA_KV_RANK = 128
CM_GROUPS = 4
CM_GROUP_DIM = 64
CM_CHUNK = 128
SC_TAPS = 3
ROPE_BASE = 10000.0
NORM_EPS = 1e-6
NEG_INF = -1e9
LOG2E = 1.4426950408889634

GW = 256
LANES = 128
NA_QROWS = 4
NA_KCHUNK_ROWS = 4
NA_KCHUNKS = 3
NA_BATCH_PER_STEP = 8
HALO = 16
V_SLAB = 80
VT_ROWS = 4 * V_SLAB
V7X_VMEM_BYTES = 64 * 1024 * 1024
VMEM_LIMIT = V7X_VMEM_BYTES - 8 * 1024 * 1024

C_AQ, C_AK, C_AV, C_AG = 0, 256, 512, 768
C_CQ, C_CKV, C_KR, C_BG = 1024, 1280, 1408, 1536
C_CU, C_CV, C_CG = 1792, 2048, 2304
C_DB, C_DC, C_DH, C_DG = 2560, 2816, 3072, 3328
D_IN_P = 3584


def _silu(x):
    return x / (1.0 + jnp.exp(-x))


def _gelu(x):
    return 0.5 * x * (1.0 + lax.erf(x * 0.7071067811865476))


def _rms(x):
    return x * lax.rsqrt(jnp.mean(x * x, axis=-1, keepdims=True) + NORM_EPS)


def _dot(a, b):
    return jnp.dot(a, b, preferred_element_type=F32)


def _dot_nt(a, b):
    return lax.dot_general(a, b, (((1,), (1,)), ((), ())), preferred_element_type=F32)


def _cparams(n_grid):
    return pltpu.CompilerParams(dimension_semantics=("parallel",) * n_grid,
                                vmem_limit_bytes=VMEM_LIMIT)


def _mod_kernel(c_ref, w_ref, b_ref, o_ref):
    s = _silu(c_ref[...]).astype(BF16)
    o_ref[...] = _dot(s, w_ref[...].astype(BF16)) + b_ref[...]


def _modulation(cc, w_mod, b_mod):
    depth, d, n = w_mod.shape
    r = cc.shape[0]
    tn = 768
    return pl.pallas_call(
        _mod_kernel,
        grid=(depth, n // tn),
        in_specs=[pl.BlockSpec((r, d), lambda l, j: (0, 0)),
                  pl.BlockSpec((None, d, tn), lambda l, j: (l, 0, j)),
                  pl.BlockSpec((None, 1, tn), lambda l, j: (l, 0, j))],
        out_specs=pl.BlockSpec((None, r, tn), lambda l, j: (l, 0, j)),
        out_shape=jax.ShapeDtypeStruct((depth, r, n), F32),
        compiler_params=_cparams(2),
        name="modulation",
    )(cc, w_mod, b_mod.reshape(depth, 1, n))


def _rope128(x, cos, sina, sinb):
    return x * cos + pltpu.roll(x, LANES - 8, 1) * sina + pltpu.roll(x, 8, 1) * sinb


def _modulated(x_ref, sh_ref, sc_ref, ng_ref):
    h = _rms(x_ref[...]) * ng_ref[...]
    return (h * (1.0 + sc_ref[...]) + sh_ref[...]).astype(BF16)


def _store_value_slabs(vt_ref, vals):
    tm = vals.shape[0]
    ones_row = (lax.broadcasted_iota(jnp.int32, (V_SLAB - MLA_V_DIM, tm), 0) == 0).astype(BF16)
    vt = vals.T.astype(BF16)
    for hd in range(NA_HEADS):
        vt_ref[hd * V_SLAB:hd * V_SLAB + MLA_V_DIM, :] = vt[hd * MLA_V_DIM:(hd + 1) * MLA_V_DIM]
        vt_ref[hd * V_SLAB + MLA_V_DIM:(hd + 1) * V_SLAB, :] = ones_row


def _store_mla_keys(bk_ref, kr, kvu):
    for p in range(MLA_HEADS // 2):
        o = p * GW
        bk_ref[:, o:o + LANES] = kr
        bk_ref[:, o + LANES:o + GW] = kvu[:, p * LANES:(p + 1) * LANES].astype(BF16)


def _inproj_kv_kernel(x_ref, sh_ref, sc_ref, ng_ref, w_ref, kvng_ref, wukv_ref,
                      ak_ref, avt_ref, bk_ref, bvt_ref):
    hb = _modulated(x_ref, sh_ref, sc_ref, ng_ref)

    def proj(off, width=GW):
        return _dot(hb, w_ref[:, off:off + width])

    kvn = (_rms(proj(C_CKV, MLA_KV_RANK)) * kvng_ref[...]).astype(BF16)
    ak_ref[...] = proj(C_AK).astype(BF16)
    _store_value_slabs(avt_ref, proj(C_AV))
    kvu = _dot(kvn, wukv_ref[...])
    _store_mla_keys(bk_ref, proj(C_KR, LANES).astype(BF16), kvu)
    _store_value_slabs(bvt_ref, kvu[:, 2 * LANES:])


def _inproj_kernel(x_ref, sh_ref, sc_ref, ng_ref, w_ref, qng_ref, wuq_ref, kvng_ref, wukv_ref,
                   cos_ref, sina_ref, sinb_ref, lng_ref, gavg_ref, ws_ref, bs_ref,
                   aq_ref, ak_ref, avt_ref, ga_ref, bq_ref, bk_ref, bvt_ref, gb_ref, mc_ref, p_ref,
                   bgd_ref):
    tm = x_ref.shape[0]
    hb = _modulated(x_ref, sh_ref, sc_ref, ng_ref)

    def proj(off, width=GW):
        return _dot(hb, w_ref[:, off:off + width])

    cos, sina, sinb = cos_ref[...], sina_ref[...], sinb_ref[...]
    qscale = (MLA_NOPE_DIM + MLA_ROPE_DIM) ** -0.5 * LOG2E
    gavg = gavg_ref[...]
    lane = lax.broadcasted_iota(jnp.int32, (1, GW), 1)

    def split_bf16(t):
        hi = t.astype(BF16)
        return hi, (t - hi.astype(F32)).astype(BF16)

    qn = (_rms(proj(C_CQ)) * qng_ref[...]).astype(BF16)
    kvn = (_rms(proj(C_CKV, MLA_KV_RANK)) * kvng_ref[...]).astype(BF16)
    v = _gelu(proj(C_CV))
    v_hi, v_lo = split_bf16(v)

    aq_ref[...] = (proj(C_AQ) * (NA_HEAD_DIM ** -0.5 * LOG2E)).astype(BF16)
    ak_ref[...] = proj(C_AK).astype(BF16)
    _store_value_slabs(avt_ref, proj(C_AV))
    ga_ref[...] = _silu(proj(C_AG)).astype(BF16)

    qu = _dot(qn, wuq_ref[...])
    kvu = _dot(kvn, wukv_ref[...])
    cen = v - (_dot(v_hi, gavg) + _dot(v_lo, gavg))
    sq_hi, sq_lo = split_bf16(cen * cen)
    for p in range(MLA_HEADS // 2):
        o = p * GW
        bq_ref[:, o:o + LANES] = (_rope128(qu[:, o:o + LANES], cos, sina, sinb) * qscale).astype(BF16)
        bq_ref[:, o + LANES:o + GW] = (qu[:, o + LANES:o + GW] * qscale).astype(BF16)
    _store_mla_keys(bk_ref, _rope128(proj(C_KR, LANES), cos, sina, sinb).astype(BF16), kvu)
    _store_value_slabs(bvt_ref, kvu[:, 2 * LANES:])

    p_ref[...] = (proj(C_DC) * proj(C_DH)).astype(BF16)
    bgd_ref[...] = (proj(C_DB) * _silu(proj(C_DG))).astype(BF16)

    var = _dot(sq_hi, gavg) + _dot(sq_lo, gavg)
    vn = (cen * lax.rsqrt(var + NORM_EPS) * lng_ref[...]).astype(BF16)
    gb_ref[...] = _silu(proj(C_BG)).astype(BF16)
    u = _gelu(proj(C_CU))
    gate_c = _silu(proj(C_CG))

    for n in range(tm // CM_CHUNK):
        rows = slice(n * CM_CHUNK, (n + 1) * CM_CHUNK)
        vc = vn[rows]
        stacked = jnp.concatenate(
            [jnp.where((lane >= g * CM_GROUP_DIM) & (lane < (g + 1) * CM_GROUP_DIM), vc, jnp.zeros_like(vc))
             for g in range(CM_GROUPS)], axis=0)
        s = bs_ref[...] + _dot(ws_ref[...], stacked)
        mc_ref[rows, :] = (u[rows] * s * gate_c[rows]).astype(BF16)


def _layer_spec(a, layer, **kwargs):
    return pl.BlockSpec((None,) + a.shape[1:], lambda *_: (layer,) + (0,) * (a.ndim - 1), **kwargs)


def _mod_spec(mod, layer, row, part):
    d = mod.shape[-1] // 3
    return pl.BlockSpec((None, None, 1, d), lambda bi, i: (layer, row(bi), 0, part))


def _inproj(x, mod, row, lw, layer, tabs, tm):
    b, l, d = x.shape
    cos, sina, sinb = tabs
    full = lambda a: pl.BlockSpec(a.shape, lambda bi, i: (0,) * a.ndim)
    tok = lambda width: pl.BlockSpec((None, tm, width), lambda bi, i: (bi, i, 0))
    tok_t = pl.BlockSpec((None, VT_ROWS, tm), lambda bi, i: (bi, 0, i))
    pos = pl.BlockSpec((tm, LANES), lambda bi, i: (i, 0))
    widths = (GW, GW, None, GW, 2 * GW, 2 * GW, None, GW, GW, GW, GW)
    consts = (lw["norm_g"], lw["w_in"], lw["qn_g"], lw["w_uq"], lw["kvn_g"], lw["w_ukv"])
    return pl.pallas_call(
        _inproj_kernel,
        grid=(b, l // tm),
        in_specs=[tok(d), _mod_spec(mod, layer, row, 0), _mod_spec(mod, layer, row, 1)]
                 + [_layer_spec(a, layer) for a in consts] + [pos, pos, pos]
                 + [_layer_spec(lw["ln_g"], layer), full(lw["gavg"]), _layer_spec(lw["ws"], layer),
                    _layer_spec(lw["bs"], layer)],
        out_specs=[tok_t if w is None else tok(w) for w in widths],
        out_shape=[jax.ShapeDtypeStruct((b, VT_ROWS, l) if w is None else (b, l, w), BF16) for w in widths],
        compiler_params=_cparams(2),
        name="inproj",
    )(x, mod, mod, *consts, cos, sina, sinb, lw["ln_g"], lw["gavg"], lw["ws"], lw["bs"])


def _inproj_kv(x, mod, row, lw, layer, tm):
    b, l, d = x.shape
    tok = lambda width: pl.BlockSpec((None, tm, width), lambda bi, i: (bi, i, 0))
    tok_t = pl.BlockSpec((None, VT_ROWS, tm), lambda bi, i: (bi, 0, i))
    consts = (lw["norm_g"], lw["w_in"], lw["kvn_g"], lw["w_ukv"])
    vt_shape = jax.ShapeDtypeStruct((b, VT_ROWS, l), BF16)
    return pl.pallas_call(
        _inproj_kv_kernel,
        grid=(b, l // tm),
        in_specs=[tok(d), _mod_spec(mod, layer, row, 0), _mod_spec(mod, layer, row, 1)]
                 + [_layer_spec(a, layer) for a in consts],
        out_specs=[tok(GW), tok_t, tok(2 * GW), tok_t],
        out_shape=[jax.ShapeDtypeStruct((b, l, GW), BF16), vt_shape,
                   jax.ShapeDtypeStruct((b, l, 2 * GW), BF16), vt_shape],
        compiler_params=_cparams(2),
        name="inproj_kv",
    )(x, mod, mod, *consts)


def _lane_mask(ranges, width):
    lane = lax.broadcasted_iota(jnp.int32, (1, width), 1)
    m = None
    for lo, hi in ranges:
        t = (lane >= lo) & (lane < hi)
        m = t if m is None else (m | t)
    return m


def _key_reduce(pieces, combine, reduce):
    by_height = {}
    for t in pieces:
        n = t.shape[0]
        by_height[n] = t if n not in by_height else combine(by_height[n], t)
    out = None
    for t in by_height.values():
        r = reduce(t, axis=0, keepdims=True)
        out = r if out is None else combine(out, r)
    return out


def _softmax_probs(scores, exp_dtype, sums_on_mxu):
    m = _key_reduce(scores, jnp.maximum, jnp.max)
    exps = [jnp.exp2((s - m).astype(exp_dtype)) for s in scores]
    l = None if sums_on_mxu else _key_reduce([e.astype(F32) for e in exps], jnp.add, jnp.sum)
    return [e.astype(BF16) for e in exps], l


def _weighted_values(probs, l, values_t):
    o = None
    for p, vt in zip(probs, values_t):
        pv = _dot(vt, p)
        o = pv if o is None else o + pv
    if l is None:
        l = o[MLA_V_DIM:MLA_V_DIM + 1]
    return o[:MLA_V_DIM] * (1.0 / l)


def _attend_blocks(n_blk, n_heads, scores_fn, values_fn, emit_fn, scr, exp_dtype, sums_on_mxu):
    def block(i, has_next, scores):
        units = [(i, h) for h in range(n_heads)] + ([(i + 1, 0)] if has_next else [])
        outs = []
        for u in range(n_heads):
            nxt = scores_fn(*units[u + 1]) if u + 1 < len(units) else None
            probs, l = _softmax_probs(scores, exp_dtype, sums_on_mxu)
            outs.append(_weighted_values(probs, l, values_fn(i, u)))
            scores = nxt
        emit_fn(i, outs)
        return scores

    scores = scores_fn(0, 0)
    if n_blk == 1:
        block(0, False, scores)
        return

    def stash(scores):
        for ref, s in zip(scr, scores):
            ref[...] = s

    def body(i, carry):
        stash(block(i, True, [ref[...] for ref in scr]))
        return carry

    stash(scores)
    lax.fori_loop(0, n_blk - 1, body, 0)
    block(n_blk - 1, False, [ref[...] for ref in scr])


def _pipeline_scratch(tq, piece_heights):
    return [pltpu.VMEM((n, tq), F32) for n in piece_heights]


def _head_rows(refs, hd):
    return [r[hd * V_SLAB:hd * V_SLAB + MLA_V_DIM, :] for r in refs]


def _head_scores(q_ref, rows, k_refs, head_spec):
    grp, ranges = head_spec
    q = q_ref[rows, grp * GW:(grp + 1) * GW]
    qm = jnp.where(_lane_mask(ranges, GW), q, jnp.zeros_like(q))
    return [_dot_nt(k[:, grp * GW:(grp + 1) * GW], qm) for k in k_refs]


def _attn_kernel(*refs, head_specs, n_src, tr):
    q_ref = refs[0]
    k_refs = refs[1:1 + n_src]
    vt_refs = refs[1 + n_src:1 + 2 * n_src]
    gate_ref, o_ref = refs[1 + 2 * n_src], refs[2 + 2 * n_src]
    scr = refs[3 + 2 * n_src:]
    n_blk = q_ref.shape[0] // tr

    def rows(i):
        start = i * tr
        return pl.ds(start if isinstance(start, int) else pl.multiple_of(start, tr), tr)

    def emit(i, outs):
        o = jnp.concatenate(outs, axis=0).T
        o_ref[rows(i), :] = (o * gate_ref[rows(i), :].astype(F32)).astype(BF16)

    _attend_blocks(n_blk, len(head_specs),
                   lambda i, h: _head_scores(q_ref, rows(i), k_refs, head_specs[h]),
                   lambda i, h: _head_rows(vt_refs, h), emit, scr, F32, False)


def _dense_attention(q, ks, vts, gate, head_specs, tq, tr):
    b, lq, qw = q.shape
    n_src = len(ks)
    kv_spec = lambda a: pl.BlockSpec((None,) + a.shape[1:], lambda bi, i: (bi, 0, 0))
    scratch = _pipeline_scratch(tr, [a.shape[1] for a in ks]) if tq > tr else []
    return pl.pallas_call(
        functools.partial(_attn_kernel, head_specs=head_specs, n_src=n_src, tr=tr),
        grid=(b, lq // tq),
        in_specs=[pl.BlockSpec((None, tq, qw), lambda bi, i: (bi, i, 0))]
                 + [kv_spec(a) for a in ks] + [kv_spec(a) for a in vts]
                 + [pl.BlockSpec((None, tq, GW), lambda bi, i: (bi, i, 0))],
        out_specs=pl.BlockSpec((None, tq, GW), lambda bi, i: (bi, i, 0)),
        out_shape=jax.ShapeDtypeStruct((b, lq, GW), BF16),
        scratch_shapes=scratch,
        compiler_params=_cparams(2),
        name="dense_attention",
    )(q, *ks, *vts, gate)


NA_HEAD_SPECS = tuple((0, ((h * 64, h * 64 + 64),)) for h in range(NA_HEADS))
MLA_HEAD_SPECS = tuple((h // 2, ((32 * (h % 2), 32 * (h % 2) + 32),
                                 (LANES + 64 * (h % 2), LANES + 64 * (h % 2) + 64)))
                       for h in range(MLA_HEADS))


def _na_kernel(q_ref, k0, k1, k2, vt0, vt1, vt2, kc_ref, vct_ref, bias_ref, win_ref, gate_ref,
               o_ref, bias_scr, *scr):
    qb = pl.program_id(0)
    n_b = q_ref.shape[0]
    kw = NA_KCHUNK_ROWS * GRID_W
    d0 = -jnp.where(qb == 0, 0, jnp.where(qb == pl.num_programs(0) - 1, 8, 4))
    q_row = lax.broadcasted_iota(jnp.int32, (1, q_ref.shape[1]), 1) // GRID_W
    n_dr = bias_ref.shape[1]
    for hd in range(NA_HEADS):
        for j in range(NA_KCHUNKS):
            for i in range(NA_KCHUNK_ROWS):
                blk = None
                for r in range(NA_QROWS):
                    a = jnp.clip(d0 + NA_KCHUNK_ROWS * j + i - r + NA_WIN_ROWS - 1, 0, n_dr - 1)
                    piece = bias_ref[hd, a]
                    blk = piece if blk is None else jnp.where(q_row == r, piece, blk)
                keys = slice(j * kw + i * GRID_W, j * kw + (i + 1) * GRID_W)
                bias_scr[hd, keys, :] = blk + win_ref[keys, :]

    def scores_fn(i, hd):
        q = q_ref[i]
        qm = jnp.where(_lane_mask([(hd * 64, hd * 64 + 64)], GW), q, jnp.zeros_like(q))
        scores = [_dot_nt(k[i], qm) + bias_scr[hd, j * kw:(j + 1) * kw, :]
                  for j, k in enumerate((k0, k1, k2))]
        scores.append(_dot_nt(kc_ref[i], qm))
        return scores

    def values_fn(i, hd):
        return [vt[i, hd * V_SLAB:(hd + 1) * V_SLAB, :] for vt in (vt0, vt1, vt2, vct_ref)]

    def emit(i, outs):
        o = jnp.concatenate(outs, axis=0).T
        o_ref[i] = (o * gate_ref[i].astype(F32)).astype(BF16)

    _attend_blocks(n_b, NA_HEADS, scores_fn, values_fn, emit, scr, BF16, True)


def _na_tables(rpb, rows):
    n_dc = 2 * NA_WIN_COLS - 1
    tq = NA_QROWS * GRID_W
    kc = np.arange(GRID_W)[:, None]
    qc = np.arange(GRID_W)[None, :]
    cs = np.clip(qc - NA_WIN_COLS // 2, 0, GRID_W - NA_WIN_COLS)
    col_ok = (kc >= cs) & (kc < cs + NA_WIN_COLS)
    ci = np.clip(kc - qc, -(NA_WIN_COLS - 1), NA_WIN_COLS - 1) + NA_WIN_COLS - 1
    sel_c = (ci[None] == np.arange(n_dc)[:, None, None]).astype(np.float32)
    sel_c = np.tile(sel_c, (1, 1, NA_QROWS))
    bias = jnp.einsum("...ac,ckq->...akq", rpb.astype(F32), jnp.asarray(sel_c),
                      precision=lax.Precision.HIGHEST) * LOG2E

    kr_win = min(NA_WIN_ROWS, rows)
    n_qb = rows // NA_QROWS
    n_krows = NA_KCHUNKS * NA_KCHUNK_ROWS
    rs = np.clip(np.arange(rows) - kr_win // 2, 0, rows - kr_win).reshape(n_qb, NA_QROWS)
    c0 = np.clip(np.arange(n_qb) - 1, 0, rows // NA_KCHUNK_ROWS - NA_KCHUNKS)
    kr = NA_KCHUNK_ROWS * c0[:, None] + np.arange(n_krows)[None, :]
    row_ok = (kr[:, :, None] >= rs[:, None, :]) & (kr[:, :, None] < rs[:, None, :] + kr_win)
    ok = row_ok[:, :, None, :, None] & col_ok[None, None, :, None, :]
    win = np.where(ok, 0.0, NEG_INF).astype(np.float32).reshape(n_qb, n_krows * GRID_W, tq)
    return bias, jnp.asarray(win)


def _neighbourhood_attention(q, k, vt, kc, vct, bias, layer, win, gate):
    b, l, _ = q.shape
    tq = NA_QROWS * GRID_W
    kw = NA_KCHUNK_ROWS * GRID_W
    n_qb = l // tq
    n_kblk = l // kw
    n_b = NA_BATCH_PER_STEP
    assert b % n_b == 0

    def chunk(j):
        return lambda qb: jnp.clip(qb - 1, 0, n_kblk - NA_KCHUNKS) + j

    k_specs = [pl.BlockSpec((n_b, kw, GW), lambda qb, bi, c=chunk(j): (bi, c(qb), 0))
               for j in range(NA_KCHUNKS)]
    vt_specs = [pl.BlockSpec((n_b, VT_ROWS, kw), lambda qb, bi, c=chunk(j): (bi, 0, c(qb)))
                for j in range(NA_KCHUNKS)]
    tok = pl.BlockSpec((n_b, tq, GW), lambda qb, bi: (bi, qb, 0))
    ctx_spec = lambda a: pl.BlockSpec((n_b,) + a.shape[1:], lambda qb, bi: (bi, 0, 0))
    return pl.pallas_call(
        _na_kernel,
        grid=(n_qb, b // n_b),
        in_specs=[tok] + k_specs + vt_specs + [ctx_spec(kc), ctx_spec(vct),
                  _layer_spec(bias, layer, pipeline_mode=pl.Buffered(1)),
                  pl.BlockSpec((None,) + win.shape[1:], lambda qb, bi: (qb, 0, 0)),
                  tok],
        out_specs=tok,
        out_shape=jax.ShapeDtypeStruct((b, l, GW), BF16),
        scratch_shapes=[pltpu.VMEM((NA_HEADS, NA_KCHUNKS * kw, tq), F32)]
                       + _pipeline_scratch(tq, [kw] * NA_KCHUNKS + [kc.shape[1]]),
        compiler_params=_cparams(2),
        name="neighbourhood_attention",
    )(q, *([k] * NA_KCHUNKS), *([vt] * NA_KCHUNKS), kc, vct, bias, win, gate)


def _outproj_kernel(oa_ref, ob_ref, mc_ref, p_ref, pprev_ref, pnext_ref, bgd_ref, scw_ref, w_ref,
                    g_ref, x_ref, fg_ref, o_ref, *, final_norm):
    i = pl.program_id(1)
    tm = x_ref.shape[0]
    p = p_ref[...].astype(F32)
    row = lax.broadcasted_iota(jnp.int32, (tm, 1), 0)
    prev_row = jnp.where(i == 0, 0.0, pprev_ref[HALO - 1:HALO, :].astype(F32))
    next_row = jnp.where(i == pl.num_programs(1) - 1, 0.0, pnext_ref[0:1, :].astype(F32))
    p_dn = jnp.where(row == 0, prev_row, pltpu.roll(p, 1, 0))
    p_up = jnp.where(row == tm - 1, next_row, pltpu.roll(p, tm - 1, 0))
    y = p_dn * scw_ref[0:1, :] + p * scw_ref[1:2, :] + p_up * scw_ref[2:3, :]
    md = (bgd_ref[...].astype(F32) * y).astype(BF16)
    acc = (_dot(oa_ref[...], w_ref[0:GW, :]) + _dot(ob_ref[...], w_ref[GW:2 * GW, :])
           + _dot(mc_ref[...], w_ref[2 * GW:3 * GW, :]) + _dot(md, w_ref[3 * GW:4 * GW, :]))
    xn = x_ref[...] + g_ref[...] * acc
    if final_norm:
        xn = _rms(xn) * fg_ref[...]
    o_ref[...] = xn


def _outproj(oa, ob, mc, p, bgd, sc_w, w_out, layer, mod, row, x, final_g, tm, final_norm):
    b, l, d = x.shape
    n_halo = l // HALO
    per_tile = tm // HALO
    tok = lambda width: pl.BlockSpec((None, tm, width), lambda bi, i: (bi, i, 0))
    full = lambda a: pl.BlockSpec(a.shape, lambda bi, i: (0,) * a.ndim)
    prev = pl.BlockSpec((None, HALO, GW), lambda bi, i: (bi, jnp.maximum(i * per_tile - 1, 0), 0))
    nxt = pl.BlockSpec((None, HALO, GW),
                       lambda bi, i: (bi, jnp.minimum((i + 1) * per_tile, n_halo - 1), 0))
    return pl.pallas_call(
        functools.partial(_outproj_kernel, final_norm=final_norm),
        grid=(b, l // tm),
        in_specs=[tok(GW), tok(GW), tok(GW), tok(GW), prev, nxt, tok(GW), _layer_spec(sc_w, layer),
                  _layer_spec(w_out, layer), _mod_spec(mod, layer, row, 2), tok(d), full(final_g)],
        out_specs=tok(d),
        out_shape=jax.ShapeDtypeStruct((b, l, d), F32),
        compiler_params=_cparams(2),
        name="outproj",
    )(oa, ob, mc, p, p, p, bgd, sc_w, w_out, mod, x, final_g)


def _prep_w_in_kernel(w_ref, o_ref):
    kr_end = C_KR + MLA_ROPE_DIM
    kr = w_ref[:, C_KR:kr_end].astype(BF16)
    o_ref[:, :C_KR] = w_ref[:, :C_KR].astype(BF16)
    o_ref[:, C_KR:C_BG] = jnp.concatenate([kr, kr, jnp.zeros_like(kr), jnp.zeros_like(kr)], axis=1)
    o_ref[:, C_BG:] = w_ref[:, kr_end:].astype(BF16)


def _prep_w_in(w):
    depth, d, n = w.shape
    tr = 256
    return pl.pallas_call(
        _prep_w_in_kernel,
        grid=(depth, d // tr),
        in_specs=[pl.BlockSpec((None, tr, n), lambda l, i: (l, i, 0))],
        out_specs=pl.BlockSpec((None, tr, D_IN_P), lambda l, i: (l, i, 0)),
        out_shape=jax.ShapeDtypeStruct((depth, d, D_IN_P), BF16),
        compiler_params=_cparams(2),
        name="prep_w_in",
    )(w)


def _prep_w_uq(w):
    hd = MLA_NOPE_DIM + MLA_ROPE_DIM
    z = jnp.zeros(w.shape[:-1] + (64,), w.dtype)
    cols = []
    for p in range(MLA_HEADS // 2):
        h0, h1 = 2 * p, 2 * p + 1
        cols += [w[..., hd * h0 + MLA_NOPE_DIM:hd * (h0 + 1)], w[..., hd * h1 + MLA_NOPE_DIM:hd * (h1 + 1)], z,
                 w[..., hd * h0:hd * h0 + MLA_NOPE_DIM], w[..., hd * h1:hd * h1 + MLA_NOPE_DIM]]
    return jnp.concatenate(cols, axis=-1).astype(BF16)


def _prep_w_ukv(w):
    hd = MLA_NOPE_DIM + MLA_V_DIM
    kn = [w[..., hd * h:hd * h + MLA_NOPE_DIM] for h in range(MLA_HEADS)]
    vv = [w[..., hd * h + MLA_NOPE_DIM:hd * (h + 1)] for h in range(MLA_HEADS)]
    return jnp.concatenate(kn + vv, axis=-1).astype(BF16)


def _rope_tables(length):
    nf = MLA_ROPE_DIM // 4
    t = np.arange(length)
    inv = ROPE_BASE ** (-np.arange(nf, dtype=np.float64) / nf)
    ang = np.stack([(t // GRID_W)[:, None] * inv, (t % GRID_W)[:, None] * inv], axis=1)
    cos = np.cos(ang)
    sin = np.sin(ang)
    zero = np.zeros_like(sin)
    cos32 = np.stack([cos, cos], axis=2).reshape(length, MLA_ROPE_DIM)
    sina32 = np.stack([-sin, zero], axis=2).reshape(length, MLA_ROPE_DIM)
    sinb32 = np.stack([zero, sin], axis=2).reshape(length, MLA_ROPE_DIM)
    pad1 = np.ones((length, 64))
    pad0 = np.zeros((length, 64))
    tabs = (np.concatenate([cos32, cos32, pad1], axis=1), np.concatenate([sina32, sina32, pad0], axis=1),
            np.concatenate([sinb32, sinb32, pad0], axis=1))
    return tuple(jnp.asarray(a, F32) for a in tabs)


def _identity_rope_tables(length):
    return (jnp.ones((length, LANES), F32), jnp.zeros((length, LANES), F32),
            jnp.zeros((length, LANES), F32))


def kernel(x, c, ctx, c_ctx, norm_g, w_mod, b_mod, w_in, na_rpb, mla_qn_g, mla_w_uq, mla_kvn_g,
           mla_w_ukv, cm_ln_g, cm_w_s, cm_b_s, sc_w, w_out, final_g):
    b, l, d = x.shape
    lc = ctx.shape[1]
    depth = w_in.shape[0]
    rows = l // GRID_W
    assert l % (NA_QROWS * GRID_W) == 0 and rows >= NA_KCHUNKS * NA_KCHUNK_ROWS and lc % CM_CHUNK == 0
    assert w_in.shape[2] == 3488 and d == 4 * GW

    n_mod = -(-(b + 1) // 8) * 8
    cc = jnp.concatenate([c, c_ctx[None], jnp.zeros((n_mod - b - 1, d), F32)], axis=0)
    mod = _modulation(cc, w_mod, b_mod).reshape(depth, n_mod, 1, 3 * d)
    row_x = lambda bi: bi
    row_c = lambda bi: b

    lw = {
        "norm_g": norm_g.reshape(depth, 1, d),
        "w_in": _prep_w_in(w_in),
        "qn_g": mla_qn_g.reshape(depth, 1, MLA_Q_RANK),
        "w_uq": _prep_w_uq(mla_w_uq),
        "kvn_g": mla_kvn_g.reshape(depth, 1, MLA_KV_RANK),
        "w_ukv": _prep_w_ukv(mla_w_ukv),
        "ln_g": cm_ln_g.reshape(depth, 1, GW),
        "gavg": jnp.asarray(np.kron(np.eye(CM_GROUPS), np.full((CM_GROUP_DIM, CM_GROUP_DIM),
                                                               1.0 / CM_GROUP_DIM)), BF16),
        "ws": jnp.transpose(cm_w_s, (0, 2, 1, 3)).reshape(depth, CM_CHUNK, CM_GROUPS * CM_CHUNK).astype(BF16),
        "bs": jnp.repeat(jnp.transpose(cm_b_s, (0, 2, 1)), CM_GROUP_DIM, axis=2),
    }
    w_out_b = w_out.astype(BF16)
    na_bias, win = _na_tables(na_rpb, rows)
    rope_x = _rope_tables(l)
    rope_c = _identity_rope_tables(lc)
    fg = final_g.reshape(1, d)

    for layer in range(depth):
        last = layer == depth - 1
        (aq, ak, avt, ga, bq, bk, bvt, gb, mc, pp, bgd) = _inproj(x, mod, row_x, lw, layer, rope_x, tm=1024)
        if last:
            akc, avct, bkc, bvct = _inproj_kv(ctx, mod, row_c, lw, layer, tm=lc)
        else:
            (aqc, akc, avct, gac, bqc, bkc, bvct, gbc, mcc, ppc, bgdc) = _inproj(ctx, mod, row_c, lw, layer,
                                                                                 rope_c, tm=lc)

        out_a = _neighbourhood_attention(aq, ak, avt, akc, avct, na_bias, layer, win, ga)
        out_b = _dense_attention(bq, [bk, bkc], [bvt, bvct], gb, MLA_HEAD_SPECS, tq=l, tr=512)
        x_new = _outproj(out_a, out_b, mc, pp, bgd, sc_w, w_out_b, layer, mod, row_x, x, fg, tm=1024,
                         final_norm=last)
        if not last:
            ctx_a = _dense_attention(aqc, [akc], [avct], gac, NA_HEAD_SPECS, tq=lc, tr=lc)
            ctx_b = _dense_attention(bqc, [bkc], [bvct], gbc, MLA_HEAD_SPECS, tq=lc, tr=lc)
            ctx = _outproj(ctx_a, ctx_b, mcc, ppc, bgdc, sc_w, w_out_b, layer, mod, row_c, ctx, fg, tm=lc,
                           final_norm=False)
        x = x_new
    return x
```

```python
import functools

import numpy as np
import jax
import jax.numpy as jnp
from jax import lax
from jax.experimental import pallas as pl
from jax.experimental.pallas import tpu as pltpu

F32 = jnp.float32
BF16 = jnp.bfloat16

GRID_W = 64
NA_HEADS = 4
NA_HEAD_DIM = 64
NA_WIN_ROWS = 8
NA_WIN_COLS = 16
MLA_HEADS = 4
MLA_NOPE_DIM = 64
MLA_ROPE_DIM = 32
MLA_V_DIM = 64
MLA_Q_RANK = 256
MLA_KV_RANK = 128
CM_GROUPS = 4
CM_GROUP_DIM = 64
CM_CHUNK = 128
SC_TAPS = 3
ROPE_BASE = 10000.0
NORM_EPS = 1e-6
NEG_INF = -1e9
LOG2E = 1.4426950408889634

GW = 256
LANES = 128
NA_QROWS = 4
NA_KCHUNK_ROWS = 4
NA_KCHUNKS = 3
NA_BATCH_PER_STEP = 8
HALO = 16
V_SLAB = 80
VT_ROWS = 4 * V_SLAB
V7X_VMEM_BYTES = 64 * 1024 * 1024
VMEM_LIMIT = V7X_VMEM_BYTES - 8 * 1024 * 1024

_SLAB_WIDTHS = (GW, GW, GW, GW,
                MLA_Q_RANK, MLA_KV_RANK, LANES, GW,
                GW, GW, GW,
                GW, GW, GW, GW)
(C_AQ, C_AK, C_AV, C_AG, C_CQ, C_CKV, C_KR, C_BG, C_CU, C_CV, C_CG,
 C_DB, C_DC, C_DH, C_DG, D_IN_P) = (int(v) for v in np.cumsum((0,) + _SLAB_WIDTHS))
D_IN = D_IN_P - LANES + MLA_ROPE_DIM


def _silu(x):
    return x / (1.0 + jnp.exp(-x))


def _gelu(x):
    return 0.5 * x * (1.0 + lax.erf(x * 0.7071067811865476))


def _rms(x):
    return x * lax.rsqrt(jnp.mean(x * x, axis=-1, keepdims=True) + NORM_EPS)


def _dot(a, b):
    return jnp.dot(a, b, preferred_element_type=F32)


def _dot_nt(a, b):
    return lax.dot_general(a, b, (((1,), (1,)), ((), ())), preferred_element_type=F32)


def _cparams(n_grid):
    return pltpu.CompilerParams(dimension_semantics=("parallel",) * n_grid,
                                vmem_limit_bytes=VMEM_LIMIT)


def _mod_kernel(c_ref, w_ref, b_ref, o_ref):
    s = _silu(c_ref[...]).astype(BF16)
    o_ref[...] = _dot(s, w_ref[...].astype(BF16)) + b_ref[...]


def _modulation(cc, w_mod, b_mod):
    depth, d, n = w_mod.shape
    r = cc.shape[0]
    tn = 768
    return pl.pallas_call(
        _mod_kernel,
        grid=(depth, n // tn),
        in_specs=[pl.BlockSpec((r, d), lambda l, j: (0, 0)),
                  pl.BlockSpec((None, d, tn), lambda l, j: (l, 0, j)),
                  pl.BlockSpec((None, 1, tn), lambda l, j: (l, 0, j))],
        out_specs=pl.BlockSpec((None, r, tn), lambda l, j: (l, 0, j)),
        out_shape=jax.ShapeDtypeStruct((depth, r, n), F32),
        compiler_params=_cparams(2),
        name="modulation",
    )(cc, w_mod, b_mod.reshape(depth, 1, n))


def _rope128(x, cos, sina, sinb):
    return x * cos + pltpu.roll(x, LANES - 8, 1) * sina + pltpu.roll(x, 8, 1) * sinb


def _modulated(x_ref, sh_ref, sc_ref, ng_ref):
    h = _rms(x_ref[...]) * ng_ref[...]
    return (h * (1.0 + sc_ref[...]) + sh_ref[...]).astype(BF16)


def _store_value_slabs(vt_ref, vals):
    tm = vals.shape[0]
    ones_row = (lax.broadcasted_iota(jnp.int32, (V_SLAB - MLA_V_DIM, tm), 0) == 0).astype(BF16)
    vt = vals.T.astype(BF16)
    for hd in range(NA_HEADS):
        vt_ref[hd * V_SLAB:hd * V_SLAB + MLA_V_DIM, :] = vt[hd * MLA_V_DIM:(hd + 1) * MLA_V_DIM]
        vt_ref[hd * V_SLAB + MLA_V_DIM:(hd + 1) * V_SLAB, :] = ones_row


def _store_mla_keys(bk_ref, kr, kvu):
    for p in range(MLA_HEADS // 2):
        o = p * GW
        bk_ref[:, o:o + LANES] = kr
        bk_ref[:, o + LANES:o + GW] = kvu[:, p * LANES:(p + 1) * LANES].astype(BF16)


def _inproj_kv_kernel(x_ref, sh_ref, sc_ref, ng_ref, w_ref, kvng_ref, wukv_ref,
                      ak_ref, avt_ref, bk_ref, bvt_ref):
    hb = _modulated(x_ref, sh_ref, sc_ref, ng_ref)

    def proj(off, width=GW):
        return _dot(hb, w_ref[:, off:off + width])

    kvn = (_rms(proj(C_CKV, MLA_KV_RANK)) * kvng_ref[...]).astype(BF16)
    ak_ref[...] = proj(C_AK).astype(BF16)
    _store_value_slabs(avt_ref, proj(C_AV))
    kvu = _dot(kvn, wukv_ref[...])
    _store_mla_keys(bk_ref, proj(C_KR, LANES).astype(BF16), kvu)
    _store_value_slabs(bvt_ref, kvu[:, 2 * LANES:])


def _inproj_kernel(x_ref, sh_ref, sc_ref, ng_ref, w_ref, qng_ref, wuq_ref, kvng_ref, wukv_ref,
                   cos_ref, sina_ref, sinb_ref, lng_ref, gavg_ref, ws_ref, bs_ref,
                   aq_ref, ak_ref, avt_ref, ga_ref, bq_ref, bk_ref, bvt_ref, gb_ref, mc_ref, p_ref,
                   bgd_ref):
    tm = x_ref.shape[0]
    hb = _modulated(x_ref, sh_ref, sc_ref, ng_ref)

    def proj(off, width=GW):
        return _dot(hb, w_ref[:, off:off + width])

    cos, sina, sinb = cos_ref[...], sina_ref[...], sinb_ref[...]
    qscale = (MLA_NOPE_DIM + MLA_ROPE_DIM) ** -0.5 * LOG2E
    gavg = gavg_ref[...]
    lane = lax.broadcasted_iota(jnp.int32, (1, GW), 1)

    def split_bf16(t):
        hi = t.astype(BF16)
        return hi, (t - hi.astype(F32)).astype(BF16)

    qn = (_rms(proj(C_CQ)) * qng_ref[...]).astype(BF16)
    kvn = (_rms(proj(C_CKV, MLA_KV_RANK)) * kvng_ref[...]).astype(BF16)
    v = _gelu(proj(C_CV))
    v_hi, v_lo = split_bf16(v)

    aq_ref[...] = (proj(C_AQ) * (NA_HEAD_DIM ** -0.5 * LOG2E)).astype(BF16)
    ak_ref[...] = proj(C_AK).astype(BF16)
    _store_value_slabs(avt_ref, proj(C_AV))
    ga_ref[...] = _silu(proj(C_AG)).astype(BF16)

    qu = _dot(qn, wuq_ref[...])
    kvu = _dot(kvn, wukv_ref[...])
    cen = v - (_dot(v_hi, gavg) + _dot(v_lo, gavg))
    sq_hi, sq_lo = split_bf16(cen * cen)
    for p in range(MLA_HEADS // 2):
        o = p * GW
        bq_ref[:, o:o + LANES] = (_rope128(qu[:, o:o + LANES], cos, sina, sinb) * qscale).astype(BF16)
        bq_ref[:, o + LANES:o + GW] = (qu[:, o + LANES:o + GW] * qscale).astype(BF16)
    _store_mla_keys(bk_ref, _rope128(proj(C_KR, LANES), cos, sina, sinb).astype(BF16), kvu)
    _store_value_slabs(bvt_ref, kvu[:, 2 * LANES:])

    p_ref[...] = (proj(C_DC) * proj(C_DH)).astype(BF16)
    bgd_ref[...] = (proj(C_DB) * _silu(proj(C_DG))).astype(BF16)

    var = _dot(sq_hi, gavg) + _dot(sq_lo, gavg)
    vn = (cen * lax.rsqrt(var + NORM_EPS) * lng_ref[...]).astype(BF16)
    gb_ref[...] = _silu(proj(C_BG)).astype(BF16)
    u = _gelu(proj(C_CU))
    gate_c = _silu(proj(C_CG))

    for n in range(tm // CM_CHUNK):
        rows = slice(n * CM_CHUNK, (n + 1) * CM_CHUNK)
        vc = vn[rows]
        stacked = jnp.concatenate(
            [jnp.where((lane >= g * CM_GROUP_DIM) & (lane < (g + 1) * CM_GROUP_DIM), vc, jnp.zeros_like(vc))
             for g in range(CM_GROUPS)], axis=0)
        s = bs_ref[...] + _dot(ws_ref[...], stacked)
        mc_ref[rows, :] = (u[rows] * s * gate_c[rows]).astype(BF16)


def _layer_spec(a, layer, **kwargs):
    return pl.BlockSpec((None,) + a.shape[1:], lambda *_: (layer,) + (0,) * (a.ndim - 1), **kwargs)


def _mod_spec(mod, layer, row, part):
    d = mod.shape[-1] // 3
    return pl.BlockSpec((None, None, 1, d), lambda bi, i: (layer, row(bi), 0, part))


def _inproj(x, mod, row, lw, layer, tabs, tm):
    b, l, d = x.shape
    cos, sina, sinb = tabs
    full = lambda a: pl.BlockSpec(a.shape, lambda bi, i: (0,) * a.ndim)
    tok = lambda width: pl.BlockSpec((None, tm, width), lambda bi, i: (bi, i, 0))
    tok_t = pl.BlockSpec((None, VT_ROWS, tm), lambda bi, i: (bi, 0, i))
    pos = pl.BlockSpec((tm, LANES), lambda bi, i: (i, 0))
    widths = (GW, GW, None, GW, 2 * GW, 2 * GW, None, GW, GW, GW, GW)
    consts = (lw["norm_g"], lw["w_in"], lw["qn_g"], lw["w_uq"], lw["kvn_g"], lw["w_ukv"])
    return pl.pallas_call(
        _inproj_kernel,
        grid=(b, l // tm),
        in_specs=[tok(d), _mod_spec(mod, layer, row, 0), _mod_spec(mod, layer, row, 1)]
                 + [_layer_spec(a, layer) for a in consts] + [pos, pos, pos]
                 + [_layer_spec(lw["ln_g"], layer), full(lw["gavg"]), _layer_spec(lw["ws"], layer),
                    _layer_spec(lw["bs"], layer)],
        out_specs=[tok_t if w is None else tok(w) for w in widths],
        out_shape=[jax.ShapeDtypeStruct((b, VT_ROWS, l) if w is None else (b, l, w), BF16) for w in widths],
        compiler_params=_cparams(2),
        name="inproj",
    )(x, mod, mod, *consts, cos, sina, sinb, lw["ln_g"], lw["gavg"], lw["ws"], lw["bs"])


def _inproj_kv(x, mod, row, lw, layer, tm):
    b, l, d = x.shape
    tok = lambda width: pl.BlockSpec((None, tm, width), lambda bi, i: (bi, i, 0))
    tok_t = pl.BlockSpec((None, VT_ROWS, tm), lambda bi, i: (bi, 0, i))
    consts = (lw["norm_g"], lw["w_in"], lw["kvn_g"], lw["w_ukv"])
    vt_shape = jax.ShapeDtypeStruct((b, VT_ROWS, l), BF16)
    return pl.pallas_call(
        _inproj_kv_kernel,
        grid=(b, l // tm),
        in_specs=[tok(d), _mod_spec(mod, layer, row, 0), _mod_spec(mod, layer, row, 1)]
                 + [_layer_spec(a, layer) for a in consts],
        out_specs=[tok(GW), tok_t, tok(2 * GW), tok_t],
        out_shape=[jax.ShapeDtypeStruct((b, l, GW), BF16), vt_shape,
                   jax.ShapeDtypeStruct((b, l, 2 * GW), BF16), vt_shape],
        compiler_params=_cparams(2),
        name="inproj_kv",
    )(x, mod, mod, *consts)


def _lane_mask(ranges, width):
    lane = lax.broadcasted_iota(jnp.int32, (1, width), 1)
    m = None
    for lo, hi in ranges:
        t = (lane >= lo) & (lane < hi)
        m = t if m is None else (m | t)
    return m


def _key_reduce(pieces, combine, reduce):
    by_height = {}
    for t in pieces:
        n = t.shape[0]
        by_height[n] = t if n not in by_height else combine(by_height[n], t)
    out = None
    for t in by_height.values():
        r = reduce(t, axis=0, keepdims=True)
        out = r if out is None else combine(out, r)
    return out


def _softmax_probs(scores, exp_dtype, sums_on_mxu):
    m = _key_reduce(scores, jnp.maximum, jnp.max)
    exps = [jnp.exp2((s - m).astype(exp_dtype)) for s in scores]
    l = None if sums_on_mxu else _key_reduce([e.astype(F32) for e in exps], jnp.add, jnp.sum)
    return [e.astype(BF16) for e in exps], l


def _weighted_values(probs, l, values_t):
    o = None
    for p, vt in zip(probs, values_t):
        pv = _dot(vt, p)
        o = pv if o is None else o + pv
    if l is None:
        l = o[MLA_V_DIM:MLA_V_DIM + 1]
    return o[:MLA_V_DIM] * (1.0 / l)


def _attend_blocks(n_blk, n_heads, scores_fn, values_fn, emit_fn, scr, exp_dtype, sums_on_mxu):
    def block(i, has_next, scores):
        units = [(i, h) for h in range(n_heads)] + ([(i + 1, 0)] if has_next else [])
        outs = []
        for u in range(n_heads):
            nxt = scores_fn(*units[u + 1]) if u + 1 < len(units) else None
            probs, l = _softmax_probs(scores, exp_dtype, sums_on_mxu)
            outs.append(_weighted_values(probs, l, values_fn(i, u)))
            scores = nxt
        emit_fn(i, outs)
        return scores

    scores = scores_fn(0, 0)
    if n_blk == 1:
        block(0, False, scores)
        return

    def stash(scores):
        for ref, s in zip(scr, scores):
            ref[...] = s

    def body(i, carry):
        stash(block(i, True, [ref[...] for ref in scr]))
        return carry

    stash(scores)
    lax.fori_loop(0, n_blk - 1, body, 0)
    block(n_blk - 1, False, [ref[...] for ref in scr])


def _pipeline_scratch(tq, piece_heights):
    return [pltpu.VMEM((n, tq), F32) for n in piece_heights]


def _head_rows(refs, hd):
    return [r[hd * V_SLAB:hd * V_SLAB + MLA_V_DIM, :] for r in refs]


def _head_scores(q_ref, rows, k_refs, head_spec):
    grp, ranges = head_spec
    q = q_ref[rows, grp * GW:(grp + 1) * GW]
    qm = jnp.where(_lane_mask(ranges, GW), q, jnp.zeros_like(q))
    return [_dot_nt(k[:, grp * GW:(grp + 1) * GW], qm) for k in k_refs]


def _attn_kernel(*refs, head_specs, n_src, tr):
    q_ref = refs[0]
    k_refs = refs[1:1 + n_src]
    vt_refs = refs[1 + n_src:1 + 2 * n_src]
    gate_ref, o_ref = refs[1 + 2 * n_src], refs[2 + 2 * n_src]
    scr = refs[3 + 2 * n_src:]
    n_blk = q_ref.shape[0] // tr

    def rows(i):
        start = i * tr
        return pl.ds(start if isinstance(start, int) else pl.multiple_of(start, tr), tr)

    def emit(i, outs):
        o = jnp.concatenate(outs, axis=0).T
        o_ref[rows(i), :] = (o * gate_ref[rows(i), :].astype(F32)).astype(BF16)

    _attend_blocks(n_blk, len(head_specs),
                   lambda i, h: _head_scores(q_ref, rows(i), k_refs, head_specs[h]),
                   lambda i, h: _head_rows(vt_refs, h), emit, scr, F32, False)


def _dense_attention(q, ks, vts, gate, head_specs, tq, tr):
    b, lq, qw = q.shape
    n_src = len(ks)
    kv_spec = lambda a: pl.BlockSpec((None,) + a.shape[1:], lambda bi, i: (bi, 0, 0))
    scratch = _pipeline_scratch(tr, [a.shape[1] for a in ks]) if tq > tr else []
    return pl.pallas_call(
        functools.partial(_attn_kernel, head_specs=head_specs, n_src=n_src, tr=tr),
        grid=(b, lq // tq),
        in_specs=[pl.BlockSpec((None, tq, qw), lambda bi, i: (bi, i, 0))]
                 + [kv_spec(a) for a in ks] + [kv_spec(a) for a in vts]
                 + [pl.BlockSpec((None, tq, GW), lambda bi, i: (bi, i, 0))],
        out_specs=pl.BlockSpec((None, tq, GW), lambda bi, i: (bi, i, 0)),
        out_shape=jax.ShapeDtypeStruct((b, lq, GW), BF16),
        scratch_shapes=scratch,
        compiler_params=_cparams(2),
        name="dense_attention",
    )(q, *ks, *vts, gate)


NA_HEAD_SPECS = tuple((0, ((h * 64, h * 64 + 64),)) for h in range(NA_HEADS))
MLA_HEAD_SPECS = tuple((h // 2, ((32 * (h % 2), 32 * (h % 2) + 32),
                                 (LANES + 64 * (h % 2), LANES + 64 * (h % 2) + 64)))
                       for h in range(MLA_HEADS))


def _na_kernel(q_ref, k0, k1, k2, vt0, vt1, vt2, kc_ref, vct_ref, bias_ref, win_ref, gate_ref,
               o_ref, bias_scr, *scr):
    qb = pl.program_id(0)
    n_b = q_ref.shape[0]
    kw = NA_KCHUNK_ROWS * GRID_W
    d0 = -jnp.where(qb == 0, 0, jnp.where(qb == pl.num_programs(0) - 1, 8, 4))
    q_row = lax.broadcasted_iota(jnp.int32, (1, q_ref.shape[1]), 1) // GRID_W
    n_dr = bias_ref.shape[1]
    for hd in range(NA_HEADS):
        for j in range(NA_KCHUNKS):
            for i in range(NA_KCHUNK_ROWS):
                blk = None
                for r in range(NA_QROWS):
                    a = jnp.clip(d0 + NA_KCHUNK_ROWS * j + i - r + NA_WIN_ROWS - 1, 0, n_dr - 1)
                    piece = bias_ref[hd, a]
                    blk = piece if blk is None else jnp.where(q_row == r, piece, blk)
                keys = slice(j * kw + i * GRID_W, j * kw + (i + 1) * GRID_W)
                bias_scr[hd, keys, :] = blk + win_ref[keys, :]

    def scores_fn(i, hd):
        q = q_ref[i]
        qm = jnp.where(_lane_mask([(hd * 64, hd * 64 + 64)], GW), q, jnp.zeros_like(q))
        scores = [_dot_nt(k[i], qm) + bias_scr[hd, j * kw:(j + 1) * kw, :]
                  for j, k in enumerate((k0, k1, k2))]
        scores.append(_dot_nt(kc_ref[i], qm))
        return scores

    def values_fn(i, hd):
        return [vt[i, hd * V_SLAB:(hd + 1) * V_SLAB, :] for vt in (vt0, vt1, vt2, vct_ref)]

    def emit(i, outs):
        o = jnp.concatenate(outs, axis=0).T
        o_ref[i] = (o * gate_ref[i].astype(F32)).astype(BF16)

    _attend_blocks(n_b, NA_HEADS, scores_fn, values_fn, emit, scr, BF16, True)


def _na_tables(rpb, rows):
    n_dc = 2 * NA_WIN_COLS - 1
    tq = NA_QROWS * GRID_W
    kc = np.arange(GRID_W)[:, None]
    qc = np.arange(GRID_W)[None, :]
    cs = np.clip(qc - NA_WIN_COLS // 2, 0, GRID_W - NA_WIN_COLS)
    col_ok = (kc >= cs) & (kc < cs + NA_WIN_COLS)
    ci = np.clip(kc - qc, -(NA_WIN_COLS - 1), NA_WIN_COLS - 1) + NA_WIN_COLS - 1
    sel_c = (ci[None] == np.arange(n_dc)[:, None, None]).astype(np.float32)
    sel_c = np.tile(sel_c, (1, 1, NA_QROWS))
    bias = jnp.einsum("...ac,ckq->...akq", rpb.astype(F32), jnp.asarray(sel_c),
                      precision=lax.Precision.HIGHEST) * LOG2E

    kr_win = min(NA_WIN_ROWS, rows)
    n_qb = rows // NA_QROWS
    n_krows = NA_KCHUNKS * NA_KCHUNK_ROWS
    rs = np.clip(np.arange(rows) - kr_win // 2, 0, rows - kr_win).reshape(n_qb, NA_QROWS)
    c0 = np.clip(np.arange(n_qb) - 1, 0, rows // NA_KCHUNK_ROWS - NA_KCHUNKS)
    kr = NA_KCHUNK_ROWS * c0[:, None] + np.arange(n_krows)[None, :]
    row_ok = (kr[:, :, None] >= rs[:, None, :]) & (kr[:, :, None] < rs[:, None, :] + kr_win)
    ok = row_ok[:, :, None, :, None] & col_ok[None, None, :, None, :]
    win = np.where(ok, 0.0, NEG_INF).astype(np.float32).reshape(n_qb, n_krows * GRID_W, tq)
    return bias, jnp.asarray(win)


def _neighbourhood_attention(q, k, vt, kc, vct, bias, layer, win, gate):
    b, l, _ = q.shape
    tq = NA_QROWS * GRID_W
    kw = NA_KCHUNK_ROWS * GRID_W
    n_qb = l // tq
    n_kblk = l // kw
    n_b = NA_BATCH_PER_STEP
    assert b % n_b == 0

    def chunk(j):
        return lambda qb: jnp.clip(qb - 1, 0, n_kblk - NA_KCHUNKS) + j

    k_specs = [pl.BlockSpec((n_b, kw, GW), lambda qb, bi, c=chunk(j): (bi, c(qb), 0))
               for j in range(NA_KCHUNKS)]
    vt_specs = [pl.BlockSpec((n_b, VT_ROWS, kw), lambda qb, bi, c=chunk(j): (bi, 0, c(qb)))
                for j in range(NA_KCHUNKS)]
    tok = pl.BlockSpec((n_b, tq, GW), lambda qb, bi: (bi, qb, 0))
    ctx_spec = lambda a: pl.BlockSpec((n_b,) + a.shape[1:], lambda qb, bi: (bi, 0, 0))
    return pl.pallas_call(
        _na_kernel,
        grid=(n_qb, b // n_b),
        in_specs=[tok] + k_specs + vt_specs + [ctx_spec(kc), ctx_spec(vct),
                  _layer_spec(bias, layer, pipeline_mode=pl.Buffered(1)),
                  pl.BlockSpec((None,) + win.shape[1:], lambda qb, bi: (qb, 0, 0)),
                  tok],
        out_specs=tok,
        out_shape=jax.ShapeDtypeStruct((b, l, GW), BF16),
        scratch_shapes=[pltpu.VMEM((NA_HEADS, NA_KCHUNKS * kw, tq), F32)]
                       + _pipeline_scratch(tq, [kw] * NA_KCHUNKS + [kc.shape[1]]),
        compiler_params=_cparams(2),
        name="neighbourhood_attention",
    )(q, *([k] * NA_KCHUNKS), *([vt] * NA_KCHUNKS), kc, vct, bias, win, gate)


def _outproj_kernel(oa_ref, ob_ref, mc_ref, p_ref, pprev_ref, pnext_ref, bgd_ref, scw_ref, w_ref,
                    g_ref, x_ref, fg_ref, o_ref, *, final_norm):
    i = pl.program_id(1)
    tm = x_ref.shape[0]
    p = p_ref[...].astype(F32)
    row = lax.broadcasted_iota(jnp.int32, (tm, 1), 0)
    prev_row = jnp.where(i == 0, 0.0, pprev_ref[HALO - 1:HALO, :].astype(F32))
    next_row = jnp.where(i == pl.num_programs(1) - 1, 0.0, pnext_ref[0:1, :].astype(F32))
    p_dn = jnp.where(row == 0, prev_row, pltpu.roll(p, 1, 0))
    p_up = jnp.where(row == tm - 1, next_row, pltpu.roll(p, tm - 1, 0))
    y = p_dn * scw_ref[0:1, :] + p * scw_ref[1:2, :] + p_up * scw_ref[2:3, :]
    md = (bgd_ref[...].astype(F32) * y).astype(BF16)
    acc = (_dot(oa_ref[...], w_ref[0:GW, :]) + _dot(ob_ref[...], w_ref[GW:2 * GW, :])
           + _dot(mc_ref[...], w_ref[2 * GW:3 * GW, :]) + _dot(md, w_ref[3 * GW:4 * GW, :]))
    xn = x_ref[...] + g_ref[...] * acc
    if final_norm:
        xn = _rms(xn) * fg_ref[...]
    o_ref[...] = xn


def _outproj(oa, ob, mc, p, bgd, sc_w, w_out, layer, mod, row, x, final_g, tm, final_norm):
    b, l, d = x.shape
    n_halo = l // HALO
    per_tile = tm // HALO
    tok = lambda width: pl.BlockSpec((None, tm, width), lambda bi, i: (bi, i, 0))
    full = lambda a: pl.BlockSpec(a.shape, lambda bi, i: (0,) * a.ndim)
    prev = pl.BlockSpec((None, HALO, GW), lambda bi, i: (bi, jnp.maximum(i * per_tile - 1, 0), 0))
    nxt = pl.BlockSpec((None, HALO, GW),
                       lambda bi, i: (bi, jnp.minimum((i + 1) * per_tile, n_halo - 1), 0))
    return pl.pallas_call(
        functools.partial(_outproj_kernel, final_norm=final_norm),
        grid=(b, l // tm),
        in_specs=[tok(GW), tok(GW), tok(GW), tok(GW), prev, nxt, tok(GW), _layer_spec(sc_w, layer),
                  _layer_spec(w_out, layer), _mod_spec(mod, layer, row, 2), tok(d), full(final_g)],
        out_specs=tok(d),
        out_shape=jax.ShapeDtypeStruct((b, l, d), F32),
        compiler_params=_cparams(2),
        name="outproj",
    )(oa, ob, mc, p, p, p, bgd, sc_w, w_out, mod, x, final_g)


def _prep_w_in_kernel(w_ref, o_ref):
    kr_end = C_KR + MLA_ROPE_DIM
    kr = w_ref[:, C_KR:kr_end]
    o_ref[:, :C_KR] = w_ref[:, :C_KR]
    o_ref[:, C_KR:C_BG] = jnp.concatenate([kr, kr, jnp.zeros_like(kr), jnp.zeros_like(kr)], axis=1)
    o_ref[:, C_BG:] = w_ref[:, kr_end:]


def _prep_w_in(w):
    depth, d, n = w.shape
    tr = 256
    return pl.pallas_call(
        _prep_w_in_kernel,
        grid=(depth, d // tr),
        in_specs=[pl.BlockSpec((None, tr, n), lambda l, i: (l, i, 0))],
        out_specs=pl.BlockSpec((None, tr, D_IN_P), lambda l, i: (l, i, 0)),
        out_shape=jax.ShapeDtypeStruct((depth, d, D_IN_P), BF16),
        compiler_params=_cparams(2),
        name="prep_w_in",
    )(w.astype(BF16))


def _prep_w_uq(w):
    hd = MLA_NOPE_DIM + MLA_ROPE_DIM
    z = jnp.zeros(w.shape[:-1] + (64,), w.dtype)
    cols = []
    for p in range(MLA_HEADS // 2):
        h0, h1 = 2 * p, 2 * p + 1
        cols += [w[..., hd * h0 + MLA_NOPE_DIM:hd * (h0 + 1)], w[..., hd * h1 + MLA_NOPE_DIM:hd * (h1 + 1)], z,
                 w[..., hd * h0:hd * h0 + MLA_NOPE_DIM], w[..., hd * h1:hd * h1 + MLA_NOPE_DIM]]
    return jnp.concatenate(cols, axis=-1).astype(BF16)


def _prep_w_ukv(w):
    hd = MLA_NOPE_DIM + MLA_V_DIM
    kn = [w[..., hd * h:hd * h + MLA_NOPE_DIM] for h in range(MLA_HEADS)]
    vv = [w[..., hd * h + MLA_NOPE_DIM:hd * (h + 1)] for h in range(MLA_HEADS)]
    return jnp.concatenate(kn + vv, axis=-1).astype(BF16)


def _rope_tables(length):
    nf = MLA_ROPE_DIM // 4
    t = np.arange(length)
    inv = ROPE_BASE ** (-np.arange(nf, dtype=np.float64) / nf)
    ang = np.stack([(t // GRID_W)[:, None] * inv, (t % GRID_W)[:, None] * inv], axis=1)
    cos = np.cos(ang)
    sin = np.sin(ang)
    zero = np.zeros_like(sin)
    cos32 = np.stack([cos, cos], axis=2).reshape(length, MLA_ROPE_DIM)
    sina32 = np.stack([-sin, zero], axis=2).reshape(length, MLA_ROPE_DIM)
    sinb32 = np.stack([zero, sin], axis=2).reshape(length, MLA_ROPE_DIM)
    pad1 = np.ones((length, 64))
    pad0 = np.zeros((length, 64))
    tabs = (np.concatenate([cos32, cos32, pad1], axis=1), np.concatenate([sina32, sina32, pad0], axis=1),
            np.concatenate([sinb32, sinb32, pad0], axis=1))
    return tuple(jnp.asarray(a, F32) for a in tabs)


def _identity_rope_tables(length):
    return (jnp.ones((length, LANES), F32), jnp.zeros((length, LANES), F32),
            jnp.zeros((length, LANES), F32))


def kernel(x, c, ctx, c_ctx, norm_g, w_mod, b_mod, w_in, na_rpb, mla_qn_g, mla_w_uq, mla_kvn_g,
           mla_w_ukv, cm_ln_g, cm_w_s, cm_b_s, sc_w, w_out, final_g):
    b, l, d = x.shape
    lc = ctx.shape[1]
    depth = w_in.shape[0]
    rows = l // GRID_W
    assert l % (NA_QROWS * GRID_W) == 0 and rows >= NA_KCHUNKS * NA_KCHUNK_ROWS and lc % CM_CHUNK == 0
    assert w_in.shape[2] == D_IN and d == 4 * GW

    n_mod = -(-(b + 1) // 8) * 8
    cc = jnp.concatenate([c, c_ctx[None], jnp.zeros((n_mod - b - 1, d), F32)], axis=0)
    mod = _modulation(cc, w_mod, b_mod).reshape(depth, n_mod, 1, 3 * d)
    row_x = lambda bi: bi
    row_c = lambda bi: b

    lw = {
        "norm_g": norm_g.reshape(depth, 1, d),
        "w_in": _prep_w_in(w_in),
        "qn_g": mla_qn_g.reshape(depth, 1, MLA_Q_RANK),
        "w_uq": _prep_w_uq(mla_w_uq),
        "kvn_g": mla_kvn_g.reshape(depth, 1, MLA_KV_RANK),
        "w_ukv": _prep_w_ukv(mla_w_ukv),
        "ln_g": cm_ln_g.reshape(depth, 1, GW),
        "gavg": jnp.asarray(np.kron(np.eye(CM_GROUPS), np.full((CM_GROUP_DIM, CM_GROUP_DIM),
                                                               1.0 / CM_GROUP_DIM)), BF16),
        "ws": jnp.transpose(cm_w_s, (0, 2, 1, 3)).reshape(depth, CM_CHUNK, CM_GROUPS * CM_CHUNK).astype(BF16),
        "bs": jnp.repeat(jnp.transpose(cm_b_s, (0, 2, 1)), CM_GROUP_DIM, axis=2),
    }
    w_out_b = w_out.astype(BF16)
    na_bias, win = _na_tables(na_rpb, rows)
    rope_x = _rope_tables(l)
    rope_c = _identity_rope_tables(lc)
    fg = final_g.reshape(1, d)

    for layer in range(depth):
        last = layer == depth - 1
        (aq, ak, avt, ga, bq, bk, bvt, gb, mc, pp, bgd) = _inproj(x, mod, row_x, lw, layer, rope_x, tm=1024)
        if last:
            akc, avct, bkc, bvct = _inproj_kv(ctx, mod, row_c, lw, layer, tm=lc)
        else:
            (aqc, akc, avct, gac, bqc, bkc, bvct, gbc, mcc, ppc, bgdc) = _inproj(ctx, mod, row_c, lw, layer,
                                                                                 rope_c, tm=lc)

        out_a = _neighbourhood_attention(aq, ak, avt, akc, avct, na_bias, layer, win, ga)
        out_b = _dense_attention(bq, [bk, bkc], [bvt, bvct], gb, MLA_HEAD_SPECS, tq=l, tr=512)
        x_new = _outproj(out_a, out_b, mc, pp, bgd, sc_w, w_out_b, layer, mod, row_x, x, fg, tm=1024,
                         final_norm=last)
        if not last:
            ctx_a = _dense_attention(aqc, [akc], [avct], gac, NA_HEAD_SPECS, tq=lc, tr=lc)
            ctx_b = _dense_attention(bqc, [bkc], [bvct], gbc, MLA_HEAD_SPECS, tq=lc, tr=lc)
            ctx = _outproj(ctx_a, ctx_b, mcc, ppc, bgdc, sc_w, w_out_b, layer, mod, row_c, ctx, fg, tm=lc,
                           final_norm=False)
        x = x_new
    return x
```

```python
import functools

import numpy as np
import jax
import jax.numpy as jnp
from jax import lax
from jax.experimental import pallas as pl
from jax.experimental.pallas import tpu as pltpu

F32 = jnp.float32
BF16 = jnp.bfloat16

GRID_W = 64
NA_HEADS = 4
NA_HEAD_DIM = 64
NA_WIN_ROWS = 8
NA_WIN_COLS = 16
MLA_HEADS = 4
MLA_NOPE_DIM = 64
MLA_ROPE_DIM = 32
MLA_V_DIM = 64
MLA_Q_RANK = 256
MLA_KV_RANK = 128
CM_GROUPS = 4
CM_GROUP_DIM = 64
CM_CHUNK = 128
SC_TAPS = 3
ROPE_BASE = 10000.0
NORM_EPS = 1e-6
NEG_INF = -1e9
LOG2E = 1.4426950408889634

GW = 256
LANES = 128
NA_QROWS = 4
NA_KCHUNK_ROWS = 4
NA_KCHUNKS = 3
NA_BATCH_PER_STEP = 8
HALO = 16
V_SLAB = 80
VT_ROWS = 4 * V_SLAB
V7X_VMEM_BYTES = 64 * 1024 * 1024
VMEM_LIMIT = V7X_VMEM_BYTES - 8 * 1024 * 1024

_SLAB_WIDTHS = (GW, GW, GW, GW,
                MLA_Q_RANK, MLA_KV_RANK, LANES, GW,
                GW, GW, GW,
                GW, GW, GW, GW)
(C_AQ, C_AK, C_AV, C_AG, C_CQ, C_CKV, C_KR, C_BG, C_CU, C_CV, C_CG,
 C_DB, C_DC, C_DH, C_DG, D_IN_P) = (int(v) for v in np.cumsum((0,) + _SLAB_WIDTHS))
D_IN = D_IN_P - LANES + MLA_ROPE_DIM


def _silu(x):
    return x / (1.0 + jnp.exp(-x))


def _gelu(x):
    return 0.5 * x * (1.0 + lax.erf(x * 0.7071067811865476))


def _rms(x):
    return x * lax.rsqrt(jnp.mean(x * x, axis=-1, keepdims=True) + NORM_EPS)


def _dot(a, b):
    return jnp.dot(a, b, preferred_element_type=F32)


def _dot_nt(a, b):
    return lax.dot_general(a, b, (((1,), (1,)), ((), ())), preferred_element_type=F32)


def _cparams(n_grid):
    return pltpu.CompilerParams(dimension_semantics=("parallel",) * n_grid,
                                vmem_limit_bytes=VMEM_LIMIT)


def _mod_kernel(c_ref, w_ref, b_ref, o_ref):
    s = _silu(c_ref[...]).astype(BF16)
    o_ref[...] = _dot(s, w_ref[...].astype(BF16)) + b_ref[...]


def _modulation(cc, w_mod, b_mod):
    depth, d, n = w_mod.shape
    r = cc.shape[0]
    tn = 768
    return pl.pallas_call(
        _mod_kernel,
        grid=(depth, n // tn),
        in_specs=[pl.BlockSpec((r, d), lambda l, j: (0, 0)),
                  pl.BlockSpec((None, d, tn), lambda l, j: (l, 0, j)),
                  pl.BlockSpec((None, 1, tn), lambda l, j: (l, 0, j))],
        out_specs=pl.BlockSpec((None, r, tn), lambda l, j: (l, 0, j)),
        out_shape=jax.ShapeDtypeStruct((depth, r, n), F32),
        compiler_params=_cparams(2),
        name="modulation",
    )(cc, w_mod, b_mod.reshape(depth, 1, n))


def _rope128(x, cos, sina, sinb):
    return x * cos + pltpu.roll(x, LANES - 8, 1) * sina + pltpu.roll(x, 8, 1) * sinb


def _modulated(x_ref, sh_ref, sc_ref, ng_ref):
    h = _rms(x_ref[...]) * ng_ref[...]
    return (h * (1.0 + sc_ref[...]) + sh_ref[...]).astype(BF16)


def _store_value_slabs(vt_ref, vals):
    tm = vals.shape[0]
    ones_row = (lax.broadcasted_iota(jnp.int32, (V_SLAB - MLA_V_DIM, tm), 0) == 0).astype(BF16)
    vt = vals.T.astype(BF16)
    for hd in range(NA_HEADS):
        vt_ref[hd * V_SLAB:hd * V_SLAB + MLA_V_DIM, :] = vt[hd * MLA_V_DIM:(hd + 1) * MLA_V_DIM]
        vt_ref[hd * V_SLAB + MLA_V_DIM:(hd + 1) * V_SLAB, :] = ones_row


def _store_mla_keys(bk_ref, kr, kvu):
    for p in range(MLA_HEADS // 2):
        o = p * GW
        bk_ref[:, o:o + LANES] = kr
        bk_ref[:, o + LANES:o + GW] = kvu[:, p * LANES:(p + 1) * LANES].astype(BF16)


def _inproj_kv_kernel(x_ref, sh_ref, sc_ref, ng_ref, w_ref, kvng_ref, wukv_ref,
                      ak_ref, avt_ref, bk_ref, bvt_ref):
    hb = _modulated(x_ref, sh_ref, sc_ref, ng_ref)

    def proj(off, width=GW):
        return _dot(hb, w_ref[:, off:off + width])

    kvn = (_rms(proj(C_CKV, MLA_KV_RANK)) * kvng_ref[...]).astype(BF16)
    ak_ref[...] = proj(C_AK).astype(BF16)
    _store_value_slabs(avt_ref, proj(C_AV))
    kvu = _dot(kvn, wukv_ref[...])
    _store_mla_keys(bk_ref, proj(C_KR, LANES).astype(BF16), kvu)
    _store_value_slabs(bvt_ref, kvu[:, 2 * LANES:])


def _inproj_kernel(x_ref, sh_ref, sc_ref, ng_ref, w_ref, qng_ref, wuq_ref, kvng_ref, wukv_ref,
                   cos_ref, sina_ref, sinb_ref, lng_ref, gavg_ref, ws_ref, bs_ref,
                   aq_ref, ak_ref, avt_ref, ga_ref, bq_ref, bk_ref, bvt_ref, gb_ref, mc_ref, p_ref,
                   bgd_ref):
    tm = x_ref.shape[0]
    hb = _modulated(x_ref, sh_ref, sc_ref, ng_ref)

    def proj(off, width=GW):
        return _dot(hb, w_ref[:, off:off + width])

    cos, sina, sinb = cos_ref[...], sina_ref[...], sinb_ref[...]
    qscale = (MLA_NOPE_DIM + MLA_ROPE_DIM) ** -0.5 * LOG2E
    gavg = gavg_ref[...]
    lane = lax.broadcasted_iota(jnp.int32, (1, GW), 1)

    def split_bf16(t):
        hi = t.astype(BF16)
        return hi, (t - hi.astype(F32)).astype(BF16)

    qn = (_rms(proj(C_CQ)) * qng_ref[...]).astype(BF16)
    kvn = (_rms(proj(C_CKV, MLA_KV_RANK)) * kvng_ref[...]).astype(BF16)
    v = _gelu(proj(C_CV))
    v_hi, v_lo = split_bf16(v)

    aq_ref[...] = (proj(C_AQ) * (NA_HEAD_DIM ** -0.5 * LOG2E)).astype(BF16)
    ak_ref[...] = proj(C_AK).astype(BF16)
    _store_value_slabs(avt_ref, proj(C_AV))
    ga_ref[...] = _silu(proj(C_AG)).astype(BF16)

    qu = _dot(qn, wuq_ref[...])
    kvu = _dot(kvn, wukv_ref[...])
    cen = v - (_dot(v_hi, gavg) + _dot(v_lo, gavg))
    sq_hi, sq_lo = split_bf16(cen * cen)
    for p in range(MLA_HEADS // 2):
        o = p * GW
        bq_ref[:, o:o + LANES] = (_rope128(qu[:, o:o + LANES], cos, sina, sinb) * qscale).astype(BF16)
        bq_ref[:, o + LANES:o + GW] = (qu[:, o + LANES:o + GW] * qscale).astype(BF16)
    _store_mla_keys(bk_ref, _rope128(proj(C_KR, LANES), cos, sina, sinb).astype(BF16), kvu)
    _store_value_slabs(bvt_ref, kvu[:, 2 * LANES:])

    p_ref[...] = (proj(C_DC) * proj(C_DH)).astype(BF16)
    bgd_ref[...] = (proj(C_DB) * _silu(proj(C_DG))).astype(BF16)

    var = _dot(sq_hi, gavg) + _dot(sq_lo, gavg)
    vn = (cen * lax.rsqrt(var + NORM_EPS) * lng_ref[...]).astype(BF16)
    gb_ref[...] = _silu(proj(C_BG)).astype(BF16)
    u = _gelu(proj(C_CU))
    gate_c = _silu(proj(C_CG))

    for n in range(tm // CM_CHUNK):
        rows = slice(n * CM_CHUNK, (n + 1) * CM_CHUNK)
        vc = vn[rows]
        stacked = jnp.concatenate(
            [jnp.where((lane >= g * CM_GROUP_DIM) & (lane < (g + 1) * CM_GROUP_DIM), vc, jnp.zeros_like(vc))
             for g in range(CM_GROUPS)], axis=0)
        s = bs_ref[...] + _dot(ws_ref[...], stacked)
        mc_ref[rows, :] = (u[rows] * s * gate_c[rows]).astype(BF16)


def _layer_spec(a, layer, **kwargs):
    return pl.BlockSpec((None,) + a.shape[1:], lambda *_: (layer,) + (0,) * (a.ndim - 1), **kwargs)


def _mod_spec(mod, layer, row, part):
    d = mod.shape[-1] // 3
    return pl.BlockSpec((None, None, 1, d), lambda bi, i: (layer, row(bi), 0, part))


def _inproj(x, mod, row, lw, layer, tabs, tm):
    b, l, d = x.shape
    cos, sina, sinb = tabs
    full = lambda a: pl.BlockSpec(a.shape, lambda bi, i: (0,) * a.ndim)
    tok = lambda width: pl.BlockSpec((None, tm, width), lambda bi, i: (bi, i, 0))
    tok_t = pl.BlockSpec((None, VT_ROWS, tm), lambda bi, i: (bi, 0, i))
    pos = pl.BlockSpec((tm, LANES), lambda bi, i: (i, 0))
    widths = (GW, GW, None, GW, 2 * GW, 2 * GW, None, GW, GW, GW, GW)
    consts = (lw["norm_g"], lw["w_in"], lw["qn_g"], lw["w_uq"], lw["kvn_g"], lw["w_ukv"])
    return pl.pallas_call(
        _inproj_kernel,
        grid=(b, l // tm),
        in_specs=[tok(d), _mod_spec(mod, layer, row, 0), _mod_spec(mod, layer, row, 1)]
                 + [_layer_spec(a, layer) for a in consts] + [pos, pos, pos]
                 + [_layer_spec(lw["ln_g"], layer), full(lw["gavg"]), _layer_spec(lw["ws"], layer),
                    _layer_spec(lw["bs"], layer)],
        out_specs=[tok_t if w is None else tok(w) for w in widths],
        out_shape=[jax.ShapeDtypeStruct((b, VT_ROWS, l) if w is None else (b, l, w), BF16) for w in widths],
        compiler_params=_cparams(2),
        name="inproj",
    )(x, mod, mod, *consts, cos, sina, sinb, lw["ln_g"], lw["gavg"], lw["ws"], lw["bs"])


def _inproj_kv(x, mod, row, lw, layer, tm):
    b, l, d = x.shape
    tok = lambda width: pl.BlockSpec((None, tm, width), lambda bi, i: (bi, i, 0))
    tok_t = pl.BlockSpec((None, VT_ROWS, tm), lambda bi, i: (bi, 0, i))
    consts = (lw["norm_g"], lw["w_in"], lw["kvn_g"], lw["w_ukv"])
    vt_shape = jax.ShapeDtypeStruct((b, VT_ROWS, l), BF16)
    return pl.pallas_call(
        _inproj_kv_kernel,
        grid=(b, l // tm),
        in_specs=[tok(d), _mod_spec(mod, layer, row, 0), _mod_spec(mod, layer, row, 1)]
                 + [_layer_spec(a, layer) for a in consts],
        out_specs=[tok(GW), tok_t, tok(2 * GW), tok_t],
        out_shape=[jax.ShapeDtypeStruct((b, l, GW), BF16), vt_shape,
                   jax.ShapeDtypeStruct((b, l, 2 * GW), BF16), vt_shape],
        compiler_params=_cparams(2),
        name="inproj_kv",
    )(x, mod, mod, *consts)


def _lane_mask(ranges, width):
    lane = lax.broadcasted_iota(jnp.int32, (1, width), 1)
    m = None
    for lo, hi in ranges:
        t = (lane >= lo) & (lane < hi)
        m = t if m is None else (m | t)
    return m


def _key_reduce(pieces, combine, reduce):
    by_height = {}
    for t in pieces:
        n = t.shape[0]
        by_height[n] = t if n not in by_height else combine(by_height[n], t)
    out = None
    for t in by_height.values():
        r = reduce(t, axis=0, keepdims=True)
        out = r if out is None else combine(out, r)
    return out


def _softmax_probs(scores, exp_dtype, sums_on_mxu):
    m = _key_reduce(scores, jnp.maximum, jnp.max)
    exps = [jnp.exp2((s - m).astype(exp_dtype)) for s in scores]
    l = None if sums_on_mxu else _key_reduce([e.astype(F32) for e in exps], jnp.add, jnp.sum)
    return [e.astype(BF16) for e in exps], l


def _weighted_values(probs, l, values_t):
    o = None
    for p, vt in zip(probs, values_t):
        pv = _dot(vt, p)
        o = pv if o is None else o + pv
    if l is None:
        l = o[MLA_V_DIM:MLA_V_DIM + 1]
    return o[:MLA_V_DIM] * (1.0 / l)


def _attend_blocks(n_blk, n_heads, scores_fn, values_fn, emit_fn, scr, exp_dtype, sums_on_mxu):
    def block(i, has_next, scores):
        units = [(i, h) for h in range(n_heads)] + ([(i + 1, 0)] if has_next else [])
        outs = []
        for u in range(n_heads):
            nxt = scores_fn(*units[u + 1]) if u + 1 < len(units) else None
            probs, l = _softmax_probs(scores, exp_dtype, sums_on_mxu)
            outs.append(_weighted_values(probs, l, values_fn(i, u)))
            scores = nxt
        emit_fn(i, outs)
        return scores

    scores = scores_fn(0, 0)
    if n_blk == 1:
        block(0, False, scores)
        return

    def stash(scores):
        for ref, s in zip(scr, scores):
            ref[...] = s

    def body(i, carry):
        stash(block(i, True, [ref[...] for ref in scr]))
        return carry

    stash(scores)
    lax.fori_loop(0, n_blk - 1, body, 0)
    block(n_blk - 1, False, [ref[...] for ref in scr])


def _pipeline_scratch(tq, piece_heights):
    return [pltpu.VMEM((n, tq), F32) for n in piece_heights]


def _head_rows(refs, hd):
    return [r[hd * V_SLAB:hd * V_SLAB + MLA_V_DIM, :] for r in refs]


def _head_scores(q_ref, rows, k_refs, head_spec):
    grp, ranges = head_spec
    q = q_ref[rows, grp * GW:(grp + 1) * GW]
    qm = jnp.where(_lane_mask(ranges, GW), q, jnp.zeros_like(q))
    return [_dot_nt(k[:, grp * GW:(grp + 1) * GW], qm) for k in k_refs]


def _attn_kernel(*refs, head_specs, n_src, tr):
    q_ref = refs[0]
    k_refs = refs[1:1 + n_src]
    vt_refs = refs[1 + n_src:1 + 2 * n_src]
    gate_ref, o_ref = refs[1 + 2 * n_src], refs[2 + 2 * n_src]
    scr = refs[3 + 2 * n_src:]
    n_blk = q_ref.shape[0] // tr

    def rows(i):
        start = i * tr
        return pl.ds(start if isinstance(start, int) else pl.multiple_of(start, tr), tr)

    def emit(i, outs):
        o = jnp.concatenate(outs, axis=0).T
        o_ref[rows(i), :] = (o * gate_ref[rows(i), :].astype(F32)).astype(BF16)

    _attend_blocks(n_blk, len(head_specs),
                   lambda i, h: _head_scores(q_ref, rows(i), k_refs, head_specs[h]),
                   lambda i, h: _head_rows(vt_refs, h), emit, scr, F32, False)


def _dense_attention(q, ks, vts, gate, head_specs, tq, tr):
    b, lq, qw = q.shape
    n_src = len(ks)
    kv_spec = lambda a: pl.BlockSpec((None,) + a.shape[1:], lambda bi, i: (bi, 0, 0))
    scratch = _pipeline_scratch(tr, [a.shape[1] for a in ks]) if tq > tr else []
    return pl.pallas_call(
        functools.partial(_attn_kernel, head_specs=head_specs, n_src=n_src, tr=tr),
        grid=(b, lq // tq),
        in_specs=[pl.BlockSpec((None, tq, qw), lambda bi, i: (bi, i, 0))]
                 + [kv_spec(a) for a in ks] + [kv_spec(a) for a in vts]
                 + [pl.BlockSpec((None, tq, GW), lambda bi, i: (bi, i, 0))],
        out_specs=pl.BlockSpec((None, tq, GW), lambda bi, i: (bi, i, 0)),
        out_shape=jax.ShapeDtypeStruct((b, lq, GW), BF16),
        scratch_shapes=scratch,
        compiler_params=_cparams(2),
        name="dense_attention",
    )(q, *ks, *vts, gate)


NA_HEAD_SPECS = tuple((0, ((h * 64, h * 64 + 64),)) for h in range(NA_HEADS))
MLA_HEAD_SPECS = tuple((h // 2, ((32 * (h % 2), 32 * (h % 2) + 32),
                                 (LANES + 64 * (h % 2), LANES + 64 * (h % 2) + 64)))
                       for h in range(MLA_HEADS))


def _na_kernel(q_ref, k0, k1, k2, vt0, vt1, vt2, kc_ref, vct_ref, bias_ref, win_ref, gate_ref,
               o_ref, bias_scr, *scr):
    qb = pl.program_id(0)
    n_b = q_ref.shape[0]
    kw = NA_KCHUNK_ROWS * GRID_W
    d0 = -jnp.where(qb == 0, 0, jnp.where(qb == pl.num_programs(0) - 1, 8, 4))
    q_row = lax.broadcasted_iota(jnp.int32, (1, q_ref.shape[1]), 1) // GRID_W
    n_dr = bias_ref.shape[1]
    for hd in range(NA_HEADS):
        for j in range(NA_KCHUNKS):
            for i in range(NA_KCHUNK_ROWS):
                blk = None
                for r in range(NA_QROWS):
                    a = jnp.clip(d0 + NA_KCHUNK_ROWS * j + i - r + NA_WIN_ROWS - 1, 0, n_dr - 1)
                    piece = bias_ref[hd, a]
                    blk = piece if blk is None else jnp.where(q_row == r, piece, blk)
                keys = slice(j * kw + i * GRID_W, j * kw + (i + 1) * GRID_W)
                bias_scr[hd, keys, :] = blk + win_ref[keys, :]

    def scores_fn(i, hd):
        q = q_ref[i]
        qm = jnp.where(_lane_mask([(hd * 64, hd * 64 + 64)], GW), q, jnp.zeros_like(q))
        scores = [_dot_nt(k[i], qm) + bias_scr[hd, j * kw:(j + 1) * kw, :]
                  for j, k in enumerate((k0, k1, k2))]
        scores.append(_dot_nt(kc_ref[i], qm))
        return scores

    def values_fn(i, hd):
        return [vt[i, hd * V_SLAB:(hd + 1) * V_SLAB, :] for vt in (vt0, vt1, vt2, vct_ref)]

    def emit(i, outs):
        o = jnp.concatenate(outs, axis=0).T
        o_ref[i] = (o * gate_ref[i].astype(F32)).astype(BF16)

    _attend_blocks(n_b, NA_HEADS, scores_fn, values_fn, emit, scr, BF16, True)


def _na_tables(rpb, rows):
    n_dc = 2 * NA_WIN_COLS - 1
    tq = NA_QROWS * GRID_W
    kc = np.arange(GRID_W)[:, None]
    qc = np.arange(GRID_W)[None, :]
    cs = np.clip(qc - NA_WIN_COLS // 2, 0, GRID_W - NA_WIN_COLS)
    col_ok = (kc >= cs) & (kc < cs + NA_WIN_COLS)
    ci = np.clip(kc - qc, -(NA_WIN_COLS - 1), NA_WIN_COLS - 1) + NA_WIN_COLS - 1
    sel_c = (ci[None] == np.arange(n_dc)[:, None, None]).astype(np.float32)
    sel_c = np.tile(sel_c, (1, 1, NA_QROWS))
    bias = jnp.einsum("...ac,ckq->...akq", rpb.astype(F32), jnp.asarray(sel_c),
                      precision=lax.Precision.HIGHEST) * LOG2E

    kr_win = min(NA_WIN_ROWS, rows)
    n_qb = rows // NA_QROWS
    n_krows = NA_KCHUNKS * NA_KCHUNK_ROWS
    rs = np.clip(np.arange(rows) - kr_win // 2, 0, rows - kr_win).reshape(n_qb, NA_QROWS)
    c0 = np.clip(np.arange(n_qb) - 1, 0, rows // NA_KCHUNK_ROWS - NA_KCHUNKS)
    kr = NA_KCHUNK_ROWS * c0[:, None] + np.arange(n_krows)[None, :]
    row_ok = (kr[:, :, None] >= rs[:, None, :]) & (kr[:, :, None] < rs[:, None, :] + kr_win)
    ok = row_ok[:, :, None, :, None] & col_ok[None, None, :, None, :]
    win = np.where(ok, 0.0, NEG_INF).astype(np.float32).reshape(n_qb, n_krows * GRID_W, tq)
    return bias, jnp.asarray(win)


def _neighbourhood_attention(q, k, vt, kc, vct, bias, layer, win, gate):
    b, l, _ = q.shape
    tq = NA_QROWS * GRID_W
    kw = NA_KCHUNK_ROWS * GRID_W
    n_qb = l // tq
    n_kblk = l // kw
    n_b = NA_BATCH_PER_STEP
    assert b % n_b == 0

    def chunk(j):
        return lambda qb: jnp.clip(qb - 1, 0, n_kblk - NA_KCHUNKS) + j

    k_specs = [pl.BlockSpec((n_b, kw, GW), lambda qb, bi, c=chunk(j): (bi, c(qb), 0))
               for j in range(NA_KCHUNKS)]
    vt_specs = [pl.BlockSpec((n_b, VT_ROWS, kw), lambda qb, bi, c=chunk(j): (bi, 0, c(qb)))
                for j in range(NA_KCHUNKS)]
    tok = pl.BlockSpec((n_b, tq, GW), lambda qb, bi: (bi, qb, 0))
    ctx_spec = lambda a: pl.BlockSpec((n_b,) + a.shape[1:], lambda qb, bi: (bi, 0, 0))
    return pl.pallas_call(
        _na_kernel,
        grid=(n_qb, b // n_b),
        in_specs=[tok] + k_specs + vt_specs + [ctx_spec(kc), ctx_spec(vct),
                  _layer_spec(bias, layer, pipeline_mode=pl.Buffered(1)),
                  pl.BlockSpec((None,) + win.shape[1:], lambda qb, bi: (qb, 0, 0)),
                  tok],
        out_specs=tok,
        out_shape=jax.ShapeDtypeStruct((b, l, GW), BF16),
        scratch_shapes=[pltpu.VMEM((NA_HEADS, NA_KCHUNKS * kw, tq), F32)]
                       + _pipeline_scratch(tq, [kw] * NA_KCHUNKS + [kc.shape[1]]),
        compiler_params=_cparams(2),
        name="neighbourhood_attention",
    )(q, *([k] * NA_KCHUNKS), *([vt] * NA_KCHUNKS), kc, vct, bias, win, gate)


def _outproj_kernel(oa_ref, ob_ref, mc_ref, p_ref, pprev_ref, pnext_ref, bgd_ref, scw_ref, w_ref,
                    g_ref, x_ref, fg_ref, o_ref, *, final_norm):
    i = pl.program_id(1)
    tm = x_ref.shape[0]
    p = p_ref[...].astype(F32)
    row = lax.broadcasted_iota(jnp.int32, (tm, 1), 0)
    prev_row = jnp.where(i == 0, 0.0, pprev_ref[HALO - 1:HALO, :].astype(F32))
    next_row = jnp.where(i == pl.num_programs(1) - 1, 0.0, pnext_ref[0:1, :].astype(F32))
    p_dn = jnp.where(row == 0, prev_row, pltpu.roll(p, 1, 0))
    p_up = jnp.where(row == tm - 1, next_row, pltpu.roll(p, tm - 1, 0))
    y = p_dn * scw_ref[0:1, :] + p * scw_ref[1:2, :] + p_up * scw_ref[2:3, :]
    md = (bgd_ref[...].astype(F32) * y).astype(BF16)
    acc = (_dot(oa_ref[...], w_ref[0:GW, :]) + _dot(ob_ref[...], w_ref[GW:2 * GW, :])
           + _dot(mc_ref[...], w_ref[2 * GW:3 * GW, :]) + _dot(md, w_ref[3 * GW:4 * GW, :]))
    xn = x_ref[...] + g_ref[...] * acc
    if final_norm:
        xn = _rms(xn) * fg_ref[...]
    o_ref[...] = xn


def _outproj(oa, ob, mc, p, bgd, sc_w, w_out, layer, mod, row, x, final_g, tm, final_norm):
    b, l, d = x.shape
    n_halo = l // HALO
    per_tile = tm // HALO
    tok = lambda width: pl.BlockSpec((None, tm, width), lambda bi, i: (bi, i, 0))
    full = lambda a: pl.BlockSpec(a.shape, lambda bi, i: (0,) * a.ndim)
    prev = pl.BlockSpec((None, HALO, GW), lambda bi, i: (bi, jnp.maximum(i * per_tile - 1, 0), 0))
    nxt = pl.BlockSpec((None, HALO, GW),
                       lambda bi, i: (bi, jnp.minimum((i + 1) * per_tile, n_halo - 1), 0))
    return pl.pallas_call(
        functools.partial(_outproj_kernel, final_norm=final_norm),
        grid=(b, l // tm),
        in_specs=[tok(GW), tok(GW), tok(GW), tok(GW), prev, nxt, tok(GW), _layer_spec(sc_w, layer),
                  _layer_spec(w_out, layer), _mod_spec(mod, layer, row, 2), tok(d), full(final_g)],
        out_specs=tok(d),
        out_shape=jax.ShapeDtypeStruct((b, l, d), F32),
        compiler_params=_cparams(2),
        name="outproj",
    )(oa, ob, mc, p, p, p, bgd, sc_w, w_out, mod, x, final_g)


def _prep_w_in_kernel(w_ref, o_ref):
    kr_end = C_KR + MLA_ROPE_DIM
    kr = w_ref[:, C_KR:kr_end]
    o_ref[:, :C_KR] = w_ref[:, :C_KR]
    o_ref[:, C_KR:C_BG] = jnp.concatenate([kr, kr, jnp.zeros_like(kr), jnp.zeros_like(kr)], axis=1)
    o_ref[:, C_BG:] = w_ref[:, kr_end:D_IN]


def _prep_w_in(w):
    depth, d, n = w.shape
    tr = 256
    padded = jnp.pad(w.astype(BF16), ((0, 0), (0, 0), (0, D_IN_P - n)))
    return pl.pallas_call(
        _prep_w_in_kernel,
        grid=(depth, d // tr),
        in_specs=[pl.BlockSpec((None, tr, D_IN_P), lambda l, i: (l, i, 0))],
        out_specs=pl.BlockSpec((None, tr, D_IN_P), lambda l, i: (l, i, 0)),
        out_shape=jax.ShapeDtypeStruct((depth, d, D_IN_P), BF16),
        compiler_params=_cparams(2),
        name="prep_w_in",
    )(padded)


def _prep_w_uq(w):
    hd = MLA_NOPE_DIM + MLA_ROPE_DIM
    z = jnp.zeros(w.shape[:-1] + (64,), w.dtype)
    cols = []
    for p in range(MLA_HEADS // 2):
        h0, h1 = 2 * p, 2 * p + 1
        cols += [w[..., hd * h0 + MLA_NOPE_DIM:hd * (h0 + 1)], w[..., hd * h1 + MLA_NOPE_DIM:hd * (h1 + 1)], z,
                 w[..., hd * h0:hd * h0 + MLA_NOPE_DIM], w[..., hd * h1:hd * h1 + MLA_NOPE_DIM]]
    return jnp.concatenate(cols, axis=-1).astype(BF16)


def _prep_w_ukv(w):
    hd = MLA_NOPE_DIM + MLA_V_DIM
    kn = [w[..., hd * h:hd * h + MLA_NOPE_DIM] for h in range(MLA_HEADS)]
    vv = [w[..., hd * h + MLA_NOPE_DIM:hd * (h + 1)] for h in range(MLA_HEADS)]
    return jnp.concatenate(kn + vv, axis=-1).astype(BF16)


def _rope_tables(length):
    nf = MLA_ROPE_DIM // 4
    t = np.arange(length)
    inv = ROPE_BASE ** (-np.arange(nf, dtype=np.float64) / nf)
    ang = np.stack([(t // GRID_W)[:, None] * inv, (t % GRID_W)[:, None] * inv], axis=1)
    cos = np.cos(ang)
    sin = np.sin(ang)
    zero = np.zeros_like(sin)
    cos32 = np.stack([cos, cos], axis=2).reshape(length, MLA_ROPE_DIM)
    sina32 = np.stack([-sin, zero], axis=2).reshape(length, MLA_ROPE_DIM)
    sinb32 = np.stack([zero, sin], axis=2).reshape(length, MLA_ROPE_DIM)
    pad1 = np.ones((length, 64))
    pad0 = np.zeros((length, 64))
    tabs = (np.concatenate([cos32, cos32, pad1], axis=1), np.concatenate([sina32, sina32, pad0], axis=1),
            np.concatenate([sinb32, sinb32, pad0], axis=1))
    return tuple(jnp.asarray(a, F32) for a in tabs)


def _identity_rope_tables(length):
    return (jnp.ones((length, LANES), F32), jnp.zeros((length, LANES), F32),
            jnp.zeros((length, LANES), F32))


def kernel(x, c, ctx, c_ctx, norm_g, w_mod, b_mod, w_in, na_rpb, mla_qn_g, mla_w_uq, mla_kvn_g,
           mla_w_ukv, cm_ln_g, cm_w_s, cm_b_s, sc_w, w_out, final_g):
    b, l, d = x.shape
    lc = ctx.shape[1]
    depth = w_in.shape[0]
    rows = l // GRID_W
    assert l % (NA_QROWS * GRID_W) == 0 and rows >= NA_KCHUNKS * NA_KCHUNK_ROWS and lc % CM_CHUNK == 0
    assert w_in.shape[2] == D_IN and d == 4 * GW

    n_mod = -(-(b + 1) // 8) * 8
    cc = jnp.concatenate([c, c_ctx[None], jnp.zeros((n_mod - b - 1, d), F32)], axis=0)
    mod = _modulation(cc, w_mod, b_mod).reshape(depth, n_mod, 1, 3 * d)
    row_x = lambda bi: bi
    row_c = lambda bi: b

    lw = {
        "norm_g": norm_g.reshape(depth, 1, d),
        "w_in": _prep_w_in(w_in),
        "qn_g": mla_qn_g.reshape(depth, 1, MLA_Q_RANK),
        "w_uq": _prep_w_uq(mla_w_uq),
        "kvn_g": mla_kvn_g.reshape(depth, 1, MLA_KV_RANK),
        "w_ukv": _prep_w_ukv(mla_w_ukv),
        "ln_g": cm_ln_g.reshape(depth, 1, GW),
        "gavg": jnp.asarray(np.kron(np.eye(CM_GROUPS), np.full((CM_GROUP_DIM, CM_GROUP_DIM),
                                                               1.0 / CM_GROUP_DIM)), BF16),
        "ws": jnp.transpose(cm_w_s, (0, 2, 1, 3)).reshape(depth, CM_CHUNK, CM_GROUPS * CM_CHUNK).astype(BF16),
        "bs": jnp.repeat(jnp.transpose(cm_b_s, (0, 2, 1)), CM_GROUP_DIM, axis=2),
    }
    w_out_b = w_out.astype(BF16)
    na_bias, win = _na_tables(na_rpb, rows)
    rope_x = _rope_tables(l)
    rope_c = _identity_rope_tables(lc)
    fg = final_g.reshape(1, d)

    for layer in range(depth):
        last = layer == depth - 1
        (aq, ak, avt, ga, bq, bk, bvt, gb, mc, pp, bgd) = _inproj(x, mod, row_x, lw, layer, rope_x, tm=1024)
        if last:
            akc, avct, bkc, bvct = _inproj_kv(ctx, mod, row_c, lw, layer, tm=lc)
        else:
            (aqc, akc, avct, gac, bqc, bkc, bvct, gbc, mcc, ppc, bgdc) = _inproj(ctx, mod, row_c, lw, layer,
                                                                                 rope_c, tm=lc)

        out_a = _neighbourhood_attention(aq, ak, avt, akc, avct, na_bias, layer, win, ga)
        out_b = _dense_attention(bq, [bk, bkc], [bvt, bvct], gb, MLA_HEAD_SPECS, tq=l, tr=512)
        x_new = _outproj(out_a, out_b, mc, pp, bgd, sc_w, w_out_b, layer, mod, row_x, x, fg, tm=1024,
                         final_norm=last)
        if not last:
            ctx_a = _dense_attention(aqc, [akc], [avct], gac, NA_HEAD_SPECS, tq=lc, tr=lc)
            ctx_b = _dense_attention(bqc, [bkc], [bvct], gbc, MLA_HEAD_SPECS, tq=lc, tr=lc)
            ctx = _outproj(ctx_a, ctx_b, mcc, ppc, bgdc, sc_w, w_out_b, layer, mod, row_c, ctx, fg, tm=lc,
                           final_norm=False)
        x = x_new
    return x
```

```python
import functools

import numpy as np
import jax
import jax.numpy as jnp
from jax import lax
from jax.experimental import pallas as pl
from jax.experimental.pallas import tpu as pltpu

F32 = jnp.float32
BF16 = jnp.bfloat16

GRID_W = 64
NA_HEADS = 4
NA_HEAD_DIM = 64
NA_WIN_ROWS = 8
NA_WIN_COLS = 16
MLA_HEADS = 4
MLA_NOPE_DIM = 64
MLA_ROPE_DIM = 32
MLA_V_DIM = 64
MLA_Q_RANK = 256
MLA_KV_RANK = 128
CM_GROUPS = 4
CM_GROUP_DIM = 64
CM_CHUNK = 128
SC_TAPS = 3
ROPE_BASE = 10000.0
NORM_EPS = 1e-6
NEG_INF = -1e9
LOG2E = 1.4426950408889634

GW = 256
LANES = 128
NA_QROWS = 4
NA_KCHUNK_ROWS = 4
NA_KCHUNKS = 3
NA_BATCH_PER_STEP = 8
HALO = 16
V_SLAB = 80
VT_ROWS = 4 * V_SLAB
V7X_VMEM_BYTES = 64 * 1024 * 1024
VMEM_LIMIT = V7X_VMEM_BYTES - 8 * 1024 * 1024

_SLAB_WIDTHS = (GW, GW, GW, GW,
                MLA_Q_RANK, MLA_KV_RANK, LANES, GW,
                GW, GW, GW,
                GW, GW, GW, GW)
(C_AQ, C_AK, C_AV, C_AG, C_CQ, C_CKV, C_KR, C_BG, C_CU, C_CV, C_CG,
 C_DB, C_DC, C_DH, C_DG, D_IN_P) = (int(v) for v in np.cumsum((0,) + _SLAB_WIDTHS))
D_IN = D_IN_P - LANES + MLA_ROPE_DIM


def _silu(x):
    return x / (1.0 + jnp.exp(-x))


def _gelu(x):
    return 0.5 * x * (1.0 + lax.erf(x * 0.7071067811865476))


def _rms(x):
    return x * lax.rsqrt(jnp.mean(x * x, axis=-1, keepdims=True) + NORM_EPS)


def _dot(a, b):
    return jnp.dot(a, b, preferred_element_type=F32)


def _dot_nt(a, b):
    return lax.dot_general(a, b, (((1,), (1,)), ((), ())), preferred_element_type=F32)


def _cparams(n_grid):
    return pltpu.CompilerParams(dimension_semantics=("parallel",) * n_grid,
                                vmem_limit_bytes=VMEM_LIMIT)


def _mod_kernel(c_ref, w_ref, b_ref, o_ref):
    s = _silu(c_ref[...]).astype(BF16)
    o_ref[...] = _dot(s, w_ref[...].astype(BF16)) + b_ref[...]


def _modulation(cc, w_mod, b_mod):
    depth, d, n = w_mod.shape
    r = cc.shape[0]
    tn = 768
    return pl.pallas_call(
        _mod_kernel,
        grid=(depth, n // tn),
        in_specs=[pl.BlockSpec((r, d), lambda l, j: (0, 0)),
                  pl.BlockSpec((None, d, tn), lambda l, j: (l, 0, j)),
                  pl.BlockSpec((None, 1, tn), lambda l, j: (l, 0, j))],
        out_specs=pl.BlockSpec((None, r, tn), lambda l, j: (l, 0, j)),
        out_shape=jax.ShapeDtypeStruct((depth, r, n), F32),
        compiler_params=_cparams(2),
        name="modulation",
    )(cc, w_mod, b_mod.reshape(depth, 1, n))


def _rope128(x, cos, sina, sinb):
    return x * cos + pltpu.roll(x, LANES - 8, 1) * sina + pltpu.roll(x, 8, 1) * sinb


def _modulated(x_ref, sh_ref, sc_ref, ng_ref):
    h = _rms(x_ref[...]) * ng_ref[...]
    return (h * (1.0 + sc_ref[...]) + sh_ref[...]).astype(BF16)


def _store_value_slabs(vt_ref, vals):
    tm = vals.shape[0]
    ones_row = (lax.broadcasted_iota(jnp.int32, (V_SLAB - MLA_V_DIM, tm), 0) == 0).astype(BF16)
    vt = vals.T.astype(BF16)
    for hd in range(NA_HEADS):
        vt_ref[hd * V_SLAB:hd * V_SLAB + MLA_V_DIM, :] = vt[hd * MLA_V_DIM:(hd + 1) * MLA_V_DIM]
        vt_ref[hd * V_SLAB + MLA_V_DIM:(hd + 1) * V_SLAB, :] = ones_row


def _store_mla_keys(bk_ref, kr, kvu):
    for p in range(MLA_HEADS // 2):
        o = p * GW
        bk_ref[:, o:o + LANES] = kr
        bk_ref[:, o + LANES:o + GW] = kvu[:, p * LANES:(p + 1) * LANES].astype(BF16)


def _inproj_kv_kernel(x_ref, sh_ref, sc_ref, ng_ref, w_ref, kvng_ref, wukv_ref,
                      ak_ref, avt_ref, bk_ref, bvt_ref):
    hb = _modulated(x_ref, sh_ref, sc_ref, ng_ref)

    def proj(off, width=GW):
        return _dot(hb, w_ref[:, off:off + width])

    kvn = (_rms(proj(C_CKV, MLA_KV_RANK)) * kvng_ref[...]).astype(BF16)
    ak_ref[...] = proj(C_AK).astype(BF16)
    _store_value_slabs(avt_ref, proj(C_AV))
    kvu = _dot(kvn, wukv_ref[...])
    _store_mla_keys(bk_ref, proj(C_KR, LANES).astype(BF16), kvu)
    _store_value_slabs(bvt_ref, kvu[:, 2 * LANES:])


def _inproj_kernel(x_ref, sh_ref, sc_ref, ng_ref, w_ref, qng_ref, wuq_ref, kvng_ref, wukv_ref,
                   cos_ref, sina_ref, sinb_ref, lng_ref, gavg_ref, ws_ref, bs_ref,
                   aq_ref, ak_ref, avt_ref, ga_ref, bq_ref, bk_ref, bvt_ref, gb_ref, mc_ref, p_ref,
                   bgd_ref):
    tm = x_ref.shape[0]
    hb = _modulated(x_ref, sh_ref, sc_ref, ng_ref)

    def proj(off, width=GW):
        return _dot(hb, w_ref[:, off:off + width])

    cos, sina, sinb = cos_ref[...], sina_ref[...], sinb_ref[...]
    qscale = (MLA_NOPE_DIM + MLA_ROPE_DIM) ** -0.5 * LOG2E
    gavg = gavg_ref[...]
    lane = lax.broadcasted_iota(jnp.int32, (1, GW), 1)

    qn = (_rms(proj(C_CQ)) * qng_ref[...]).astype(BF16)
    kvn = (_rms(proj(C_CKV, MLA_KV_RANK)) * kvng_ref[...]).astype(BF16)
    v = _gelu(proj(C_CV))

    aq_ref[...] = (proj(C_AQ) * (NA_HEAD_DIM ** -0.5 * LOG2E)).astype(BF16)
    ak_ref[...] = proj(C_AK).astype(BF16)
    _store_value_slabs(avt_ref, proj(C_AV))
    ga_ref[...] = _silu(proj(C_AG)).astype(BF16)

    qu = _dot(qn, wuq_ref[...])
    kvu = _dot(kvn, wukv_ref[...])
    cen = v - _dot(v.astype(BF16), gavg)
    sq = (cen * cen).astype(BF16)
    for p in range(MLA_HEADS // 2):
        o = p * GW
        bq_ref[:, o:o + LANES] = (_rope128(qu[:, o:o + LANES], cos, sina, sinb) * qscale).astype(BF16)
        bq_ref[:, o + LANES:o + GW] = (qu[:, o + LANES:o + GW] * qscale).astype(BF16)
    _store_mla_keys(bk_ref, _rope128(proj(C_KR, LANES), cos, sina, sinb).astype(BF16), kvu)
    _store_value_slabs(bvt_ref, kvu[:, 2 * LANES:])

    p_ref[...] = (proj(C_DC) * proj(C_DH)).astype(BF16)
    bgd_ref[...] = (proj(C_DB) * _silu(proj(C_DG))).astype(BF16)

    var = _dot(sq, gavg)
    vn = (cen * lax.rsqrt(var + NORM_EPS) * lng_ref[...]).astype(BF16)
    gb_ref[...] = _silu(proj(C_BG)).astype(BF16)
    u = _gelu(proj(C_CU))
    gate_c = _silu(proj(C_CG))

    for n in range(tm // CM_CHUNK):
        rows = slice(n * CM_CHUNK, (n + 1) * CM_CHUNK)
        vc = vn[rows]
        stacked = jnp.concatenate(
            [jnp.where((lane >= g * CM_GROUP_DIM) & (lane < (g + 1) * CM_GROUP_DIM), vc, jnp.zeros_like(vc))
             for g in range(CM_GROUPS)], axis=0)
        s = bs_ref[...] + _dot(ws_ref[...], stacked)
        mc_ref[rows, :] = (u[rows] * s * gate_c[rows]).astype(BF16)


def _layer_spec(a, layer, **kwargs):
    return pl.BlockSpec((None,) + a.shape[1:], lambda *_: (layer,) + (0,) * (a.ndim - 1), **kwargs)


def _mod_spec(mod, layer, row, part):
    d = mod.shape[-1] // 3
    return pl.BlockSpec((None, None, 1, d), lambda bi, i: (layer, row(bi), 0, part))


def _inproj(x, mod, row, lw, layer, tabs, tm):
    b, l, d = x.shape
    cos, sina, sinb = tabs
    full = lambda a: pl.BlockSpec(a.shape, lambda bi, i: (0,) * a.ndim)
    tok = lambda width: pl.BlockSpec((None, tm, width), lambda bi, i: (bi, i, 0))
    tok_t = pl.BlockSpec((None, VT_ROWS, tm), lambda bi, i: (bi, 0, i))
    pos = pl.BlockSpec((tm, LANES), lambda bi, i: (i, 0))
    widths = (GW, GW, None, GW, 2 * GW, 2 * GW, None, GW, GW, GW, GW)
    consts = (lw["norm_g"], lw["w_in"], lw["qn_g"], lw["w_uq"], lw["kvn_g"], lw["w_ukv"])
    return pl.pallas_call(
        _inproj_kernel,
        grid=(b, l // tm),
        in_specs=[tok(d), _mod_spec(mod, layer, row, 0), _mod_spec(mod, layer, row, 1)]
                 + [_layer_spec(a, layer) for a in consts] + [pos, pos, pos]
                 + [_layer_spec(lw["ln_g"], layer), full(lw["gavg"]), _layer_spec(lw["ws"], layer),
                    _layer_spec(lw["bs"], layer)],
        out_specs=[tok_t if w is None else tok(w) for w in widths],
        out_shape=[jax.ShapeDtypeStruct((b, VT_ROWS, l) if w is None else (b, l, w), BF16) for w in widths],
        compiler_params=_cparams(2),
        name="inproj",
    )(x, mod, mod, *consts, cos, sina, sinb, lw["ln_g"], lw["gavg"], lw["ws"], lw["bs"])


def _inproj_kv(x, mod, row, lw, layer, tm):
    b, l, d = x.shape
    tok = lambda width: pl.BlockSpec((None, tm, width), lambda bi, i: (bi, i, 0))
    tok_t = pl.BlockSpec((None, VT_ROWS, tm), lambda bi, i: (bi, 0, i))
    consts = (lw["norm_g"], lw["w_in"], lw["kvn_g"], lw["w_ukv"])
    vt_shape = jax.ShapeDtypeStruct((b, VT_ROWS, l), BF16)
    return pl.pallas_call(
        _inproj_kv_kernel,
        grid=(b, l // tm),
        in_specs=[tok(d), _mod_spec(mod, layer, row, 0), _mod_spec(mod, layer, row, 1)]
                 + [_layer_spec(a, layer) for a in consts],
        out_specs=[tok(GW), tok_t, tok(2 * GW), tok_t],
        out_shape=[jax.ShapeDtypeStruct((b, l, GW), BF16), vt_shape,
                   jax.ShapeDtypeStruct((b, l, 2 * GW), BF16), vt_shape],
        compiler_params=_cparams(2),
        name="inproj_kv",
    )(x, mod, mod, *consts)


def _lane_mask(ranges, width):
    lane = lax.broadcasted_iota(jnp.int32, (1, width), 1)
    m = None
    for lo, hi in ranges:
        t = (lane >= lo) & (lane < hi)
        m = t if m is None else (m | t)
    return m


def _key_reduce(pieces, combine, reduce):
    by_height = {}
    for t in pieces:
        n = t.shape[0]
        by_height[n] = t if n not in by_height else combine(by_height[n], t)
    out = None
    for t in by_height.values():
        r = reduce(t, axis=0, keepdims=True)
        out = r if out is None else combine(out, r)
    return out


def _softmax_probs(scores, exp_dtype, sums_on_mxu):
    m = _key_reduce(scores, jnp.maximum, jnp.max)
    exps = [jnp.exp2((s - m).astype(exp_dtype)) for s in scores]
    l = None if sums_on_mxu else _key_reduce([e.astype(F32) for e in exps], jnp.add, jnp.sum)
    return [e.astype(BF16) for e in exps], l


def _weighted_values(probs, l, values_t):
    o = None
    for p, vt in zip(probs, values_t):
        pv = _dot(vt, p)
        o = pv if o is None else o + pv
    if l is None:
        l = o[MLA_V_DIM:MLA_V_DIM + 1]
    return o[:MLA_V_DIM] * (1.0 / l)


def _attend_blocks(n_blk, n_heads, scores_fn, values_fn, emit_fn, scr, exp_dtype, sums_on_mxu):
    def block(i, has_next, scores):
        units = [(i, h) for h in range(n_heads)] + ([(i + 1, 0)] if has_next else [])
        outs = []
        for u in range(n_heads):
            nxt = scores_fn(*units[u + 1]) if u + 1 < len(units) else None
            probs, l = _softmax_probs(scores, exp_dtype, sums_on_mxu)
            outs.append(_weighted_values(probs, l, values_fn(i, u)))
            scores = nxt
        emit_fn(i, outs)
        return scores

    scores = scores_fn(0, 0)
    if n_blk == 1:
        block(0, False, scores)
        return

    def stash(scores):
        for ref, s in zip(scr, scores):
            ref[...] = s

    def body(i, carry):
        stash(block(i, True, [ref[...] for ref in scr]))
        return carry

    stash(scores)
    lax.fori_loop(0, n_blk - 1, body, 0)
    block(n_blk - 1, False, [ref[...] for ref in scr])


def _pipeline_scratch(tq, piece_heights):
    return [pltpu.VMEM((n, tq), F32) for n in piece_heights]


def _head_rows(refs, hd):
    return [r[hd * V_SLAB:hd * V_SLAB + MLA_V_DIM, :] for r in refs]


def _head_scores(q_ref, rows, k_refs, head_spec):
    grp, ranges = head_spec
    q = q_ref[rows, grp * GW:(grp + 1) * GW]
    qm = jnp.where(_lane_mask(ranges, GW), q, jnp.zeros_like(q))
    return [_dot_nt(k[:, grp * GW:(grp + 1) * GW], qm) for k in k_refs]


def _attn_kernel(*refs, head_specs, n_src, tr):
    q_ref = refs[0]
    k_refs = refs[1:1 + n_src]
    vt_refs = refs[1 + n_src:1 + 2 * n_src]
    gate_ref, o_ref = refs[1 + 2 * n_src], refs[2 + 2 * n_src]
    scr = refs[3 + 2 * n_src:]
    n_blk = q_ref.shape[0] // tr

    def rows(i):
        start = i * tr
        return pl.ds(start if isinstance(start, int) else pl.multiple_of(start, tr), tr)

    def emit(i, outs):
        o = jnp.concatenate(outs, axis=0).T
        o_ref[rows(i), :] = (o * gate_ref[rows(i), :].astype(F32)).astype(BF16)

    _attend_blocks(n_blk, len(head_specs),
                   lambda i, h: _head_scores(q_ref, rows(i), k_refs, head_specs[h]),
                   lambda i, h: _head_rows(vt_refs, h), emit, scr, F32, False)


def _dense_attention(q, ks, vts, gate, head_specs, tq, tr):
    b, lq, qw = q.shape
    n_src = len(ks)
    kv_spec = lambda a: pl.BlockSpec((None,) + a.shape[1:], lambda bi, i: (bi, 0, 0))
    scratch = _pipeline_scratch(tr, [a.shape[1] for a in ks]) if tq > tr else []
    return pl.pallas_call(
        functools.partial(_attn_kernel, head_specs=head_specs, n_src=n_src, tr=tr),
        grid=(b, lq // tq),
        in_specs=[pl.BlockSpec((None, tq, qw), lambda bi, i: (bi, i, 0))]
                 + [kv_spec(a) for a in ks] + [kv_spec(a) for a in vts]
                 + [pl.BlockSpec((None, tq, GW), lambda bi, i: (bi, i, 0))],
        out_specs=pl.BlockSpec((None, tq, GW), lambda bi, i: (bi, i, 0)),
        out_shape=jax.ShapeDtypeStruct((b, lq, GW), BF16),
        scratch_shapes=scratch,
        compiler_params=_cparams(2),
        name="dense_attention",
    )(q, *ks, *vts, gate)


NA_HEAD_SPECS = tuple((0, ((h * 64, h * 64 + 64),)) for h in range(NA_HEADS))
MLA_HEAD_SPECS = tuple((h // 2, ((32 * (h % 2), 32 * (h % 2) + 32),
                                 (LANES + 64 * (h % 2), LANES + 64 * (h % 2) + 64)))
                       for h in range(MLA_HEADS))


def _na_kernel(q_ref, k0, k1, k2, vt0, vt1, vt2, kc_ref, vct_ref, bias_ref, win_ref, gate_ref,
               o_ref, bias_scr, *scr):
    qb = pl.program_id(0)
    n_b = q_ref.shape[0]
    kw = NA_KCHUNK_ROWS * GRID_W
    d0 = -jnp.where(qb == 0, 0, jnp.where(qb == pl.num_programs(0) - 1, 8, 4))
    q_row = lax.broadcasted_iota(jnp.int32, (1, q_ref.shape[1]), 1) // GRID_W
    n_dr = bias_ref.shape[1]
    for hd in range(NA_HEADS):
        for j in range(NA_KCHUNKS):
            for i in range(NA_KCHUNK_ROWS):
                blk = None
                for r in range(NA_QROWS):
                    a = jnp.clip(d0 + NA_KCHUNK_ROWS * j + i - r + NA_WIN_ROWS - 1, 0, n_dr - 1)
                    piece = bias_ref[hd, a]
                    blk = piece if blk is None else jnp.where(q_row == r, piece, blk)
                keys = slice(j * kw + i * GRID_W, j * kw + (i + 1) * GRID_W)
                bias_scr[hd, keys, :] = blk + win_ref[keys, :]

    def scores_fn(i, hd):
        q = q_ref[i]
        qm = jnp.where(_lane_mask([(hd * 64, hd * 64 + 64)], GW), q, jnp.zeros_like(q))
        scores = [_dot_nt(k[i], qm) + bias_scr[hd, j * kw:(j + 1) * kw, :]
                  for j, k in enumerate((k0, k1, k2))]
        scores.append(_dot_nt(kc_ref[i], qm))
        return scores

    def values_fn(i, hd):
        return [vt[i, hd * V_SLAB:(hd + 1) * V_SLAB, :] for vt in (vt0, vt1, vt2, vct_ref)]

    def emit(i, outs):
        o = jnp.concatenate(outs, axis=0).T
        o_ref[i] = (o * gate_ref[i].astype(F32)).astype(BF16)

    _attend_blocks(n_b, NA_HEADS, scores_fn, values_fn, emit, scr, BF16, True)


def _na_tables(rpb, rows):
    n_dc = 2 * NA_WIN_COLS - 1
    tq = NA_QROWS * GRID_W
    kc = np.arange(GRID_W)[:, None]
    qc = np.arange(GRID_W)[None, :]
    cs = np.clip(qc - NA_WIN_COLS // 2, 0, GRID_W - NA_WIN_COLS)
    col_ok = (kc >= cs) & (kc < cs + NA_WIN_COLS)
    ci = np.clip(kc - qc, -(NA_WIN_COLS - 1), NA_WIN_COLS - 1) + NA_WIN_COLS - 1
    sel_c = (ci[None] == np.arange(n_dc)[:, None, None]).astype(np.float32)
    sel_c = np.tile(sel_c, (1, 1, NA_QROWS))
    bias = jnp.einsum("...ac,ckq->...akq", rpb.astype(F32), jnp.asarray(sel_c),
                      precision=lax.Precision.HIGHEST) * LOG2E

    kr_win = min(NA_WIN_ROWS, rows)
    n_qb = rows // NA_QROWS
    n_krows = NA_KCHUNKS * NA_KCHUNK_ROWS
    rs = np.clip(np.arange(rows) - kr_win // 2, 0, rows - kr_win).reshape(n_qb, NA_QROWS)
    c0 = np.clip(np.arange(n_qb) - 1, 0, rows // NA_KCHUNK_ROWS - NA_KCHUNKS)
    kr = NA_KCHUNK_ROWS * c0[:, None] + np.arange(n_krows)[None, :]
    row_ok = (kr[:, :, None] >= rs[:, None, :]) & (kr[:, :, None] < rs[:, None, :] + kr_win)
    ok = row_ok[:, :, None, :, None] & col_ok[None, None, :, None, :]
    win = np.where(ok, 0.0, NEG_INF).astype(np.float32).reshape(n_qb, n_krows * GRID_W, tq)
    return bias, jnp.asarray(win)


def _neighbourhood_attention(q, k, vt, kc, vct, bias, layer, win, gate):
    b, l, _ = q.shape
    tq = NA_QROWS * GRID_W
    kw = NA_KCHUNK_ROWS * GRID_W
    n_qb = l // tq
    n_kblk = l // kw
    n_b = NA_BATCH_PER_STEP
    assert b % n_b == 0

    def chunk(j):
        return lambda qb: jnp.clip(qb - 1, 0, n_kblk - NA_KCHUNKS) + j

    k_specs = [pl.BlockSpec((n_b, kw, GW), lambda qb, bi, c=chunk(j): (bi, c(qb), 0))
               for j in range(NA_KCHUNKS)]
    vt_specs = [pl.BlockSpec((n_b, VT_ROWS, kw), lambda qb, bi, c=chunk(j): (bi, 0, c(qb)))
                for j in range(NA_KCHUNKS)]
    tok = pl.BlockSpec((n_b, tq, GW), lambda qb, bi: (bi, qb, 0))
    ctx_spec = lambda a: pl.BlockSpec((n_b,) + a.shape[1:], lambda qb, bi: (bi, 0, 0))
    return pl.pallas_call(
        _na_kernel,
        grid=(n_qb, b // n_b),
        in_specs=[tok] + k_specs + vt_specs + [ctx_spec(kc), ctx_spec(vct),
                  _layer_spec(bias, layer, pipeline_mode=pl.Buffered(1)),
                  pl.BlockSpec((None,) + win.shape[1:], lambda qb, bi: (qb, 0, 0)),
                  tok],
        out_specs=tok,
        out_shape=jax.ShapeDtypeStruct((b, l, GW), BF16),
        scratch_shapes=[pltpu.VMEM((NA_HEADS, NA_KCHUNKS * kw, tq), F32)]
                       + _pipeline_scratch(tq, [kw] * NA_KCHUNKS + [kc.shape[1]]),
        compiler_params=_cparams(2),
        name="neighbourhood_attention",
    )(q, *([k] * NA_KCHUNKS), *([vt] * NA_KCHUNKS), kc, vct, bias, win, gate)


def _outproj_kernel(oa_ref, ob_ref, mc_ref, p_ref, pprev_ref, pnext_ref, bgd_ref, scw_ref, w_ref,
                    g_ref, x_ref, fg_ref, o_ref, *, final_norm):
    i = pl.program_id(1)
    tm = x_ref.shape[0]
    p = p_ref[...].astype(F32)
    row = lax.broadcasted_iota(jnp.int32, (tm, 1), 0)
    prev_row = jnp.where(i == 0, 0.0, pprev_ref[HALO - 1:HALO, :].astype(F32))
    next_row = jnp.where(i == pl.num_programs(1) - 1, 0.0, pnext_ref[0:1, :].astype(F32))
    p_dn = jnp.where(row == 0, prev_row, pltpu.roll(p, 1, 0))
    p_up = jnp.where(row == tm - 1, next_row, pltpu.roll(p, tm - 1, 0))
    y = p_dn * scw_ref[0:1, :] + p * scw_ref[1:2, :] + p_up * scw_ref[2:3, :]
    md = (bgd_ref[...].astype(F32) * y).astype(BF16)
    acc = (_dot(oa_ref[...], w_ref[0:GW, :]) + _dot(ob_ref[...], w_ref[GW:2 * GW, :])
           + _dot(mc_ref[...], w_ref[2 * GW:3 * GW, :]) + _dot(md, w_ref[3 * GW:4 * GW, :]))
    xn = x_ref[...] + g_ref[...] * acc
    if final_norm:
        xn = _rms(xn) * fg_ref[...]
    o_ref[...] = xn


def _outproj(oa, ob, mc, p, bgd, sc_w, w_out, layer, mod, row, x, final_g, tm, final_norm):
    b, l, d = x.shape
    n_halo = l // HALO
    per_tile = tm // HALO
    tok = lambda width: pl.BlockSpec((None, tm, width), lambda bi, i: (bi, i, 0))
    full = lambda a: pl.BlockSpec(a.shape, lambda bi, i: (0,) * a.ndim)
    prev = pl.BlockSpec((None, HALO, GW), lambda bi, i: (bi, jnp.maximum(i * per_tile - 1, 0), 0))
    nxt = pl.BlockSpec((None, HALO, GW),
                       lambda bi, i: (bi, jnp.minimum((i + 1) * per_tile, n_halo - 1), 0))
    return pl.pallas_call(
        functools.partial(_outproj_kernel, final_norm=final_norm),
        grid=(b, l // tm),
        in_specs=[tok(GW), tok(GW), tok(GW), tok(GW), prev, nxt, tok(GW), _layer_spec(sc_w, layer),
                  _layer_spec(w_out, layer), _mod_spec(mod, layer, row, 2), tok(d), full(final_g)],
        out_specs=tok(d),
        out_shape=jax.ShapeDtypeStruct((b, l, d), F32),
        compiler_params=_cparams(2),
        name="outproj",
    )(oa, ob, mc, p, p, p, bgd, sc_w, w_out, mod, x, final_g)


def _prep_w_in_kernel(w_ref, o_ref):
    kr_end = C_KR + MLA_ROPE_DIM
    kr = w_ref[:, C_KR:kr_end]
    o_ref[:, :C_KR] = w_ref[:, :C_KR]
    o_ref[:, C_KR:C_BG] = jnp.concatenate([kr, kr, jnp.zeros_like(kr), jnp.zeros_like(kr)], axis=1)
    o_ref[:, C_BG:] = w_ref[:, kr_end:]


def _prep_w_in(w):
    depth, d, n = w.shape
    tr = 256
    return pl.pallas_call(
        _prep_w_in_kernel,
        grid=(depth, d // tr),
        in_specs=[pl.BlockSpec((None, tr, n), lambda l, i: (l, i, 0))],
        out_specs=pl.BlockSpec((None, tr, D_IN_P), lambda l, i: (l, i, 0)),
        out_shape=jax.ShapeDtypeStruct((depth, d, D_IN_P), BF16),
        compiler_params=_cparams(2),
        name="prep_w_in",
    )(w.astype(BF16))


def _prep_w_uq(w):
    hd = MLA_NOPE_DIM + MLA_ROPE_DIM
    z = jnp.zeros(w.shape[:-1] + (64,), w.dtype)
    cols = []
    for p in range(MLA_HEADS // 2):
        h0, h1 = 2 * p, 2 * p + 1
        cols += [w[..., hd * h0 + MLA_NOPE_DIM:hd * (h0 + 1)], w[..., hd * h1 + MLA_NOPE_DIM:hd * (h1 + 1)], z,
                 w[..., hd * h0:hd * h0 + MLA_NOPE_DIM], w[..., hd * h1:hd * h1 + MLA_NOPE_DIM]]
    return jnp.concatenate(cols, axis=-1).astype(BF16)


def _prep_w_ukv(w):
    hd = MLA_NOPE_DIM + MLA_V_DIM
    kn = [w[..., hd * h:hd * h + MLA_NOPE_DIM] for h in range(MLA_HEADS)]
    vv = [w[..., hd * h + MLA_NOPE_DIM:hd * (h + 1)] for h in range(MLA_HEADS)]
    return jnp.concatenate(kn + vv, axis=-1).astype(BF16)


def _rope_tables(length):
    nf = MLA_ROPE_DIM // 4
    t = np.arange(length)
    inv = ROPE_BASE ** (-np.arange(nf, dtype=np.float64) / nf)
    ang = np.stack([(t // GRID_W)[:, None] * inv, (t % GRID_W)[:, None] * inv], axis=1)
    cos = np.cos(ang)
    sin = np.sin(ang)
    zero = np.zeros_like(sin)
    cos32 = np.stack([cos, cos], axis=2).reshape(length, MLA_ROPE_DIM)
    sina32 = np.stack([-sin, zero], axis=2).reshape(length, MLA_ROPE_DIM)
    sinb32 = np.stack([zero, sin], axis=2).reshape(length, MLA_ROPE_DIM)
    pad1 = np.ones((length, 64))
    pad0 = np.zeros((length, 64))
    tabs = (np.concatenate([cos32, cos32, pad1], axis=1), np.concatenate([sina32, sina32, pad0], axis=1),
            np.concatenate([sinb32, sinb32, pad0], axis=1))
    return tuple(jnp.asarray(a, F32) for a in tabs)


def _identity_rope_tables(length):
    return (jnp.ones((length, LANES), F32), jnp.zeros((length, LANES), F32),
            jnp.zeros((length, LANES), F32))


def kernel(x, c, ctx, c_ctx, norm_g, w_mod, b_mod, w_in, na_rpb, mla_qn_g, mla_w_uq, mla_kvn_g,
           mla_w_ukv, cm_ln_g, cm_w_s, cm_b_s, sc_w, w_out, final_g):
    b, l, d = x.shape
    lc = ctx.shape[1]
    depth = w_in.shape[0]
    rows = l // GRID_W
    assert l % (NA_QROWS * GRID_W) == 0 and rows >= NA_KCHUNKS * NA_KCHUNK_ROWS and lc % CM_CHUNK == 0
    assert w_in.shape[2] == D_IN and d == 4 * GW

    n_mod = -(-(b + 1) // 8) * 8
    cc = jnp.concatenate([c, c_ctx[None], jnp.zeros((n_mod - b - 1, d), F32)], axis=0)
    mod = _modulation(cc, w_mod, b_mod).reshape(depth, n_mod, 1, 3 * d)
    row_x = lambda bi: bi
    row_c = lambda bi: b

    lw = {
        "norm_g": norm_g.reshape(depth, 1, d),
        "w_in": _prep_w_in(w_in),
        "qn_g": mla_qn_g.reshape(depth, 1, MLA_Q_RANK),
        "w_uq": _prep_w_uq(mla_w_uq),
        "kvn_g": mla_kvn_g.reshape(depth, 1, MLA_KV_RANK),
        "w_ukv": _prep_w_ukv(mla_w_ukv),
        "ln_g": cm_ln_g.reshape(depth, 1, GW),
        "gavg": jnp.asarray(np.kron(np.eye(CM_GROUPS), np.full((CM_GROUP_DIM, CM_GROUP_DIM),
                                                               1.0 / CM_GROUP_DIM)), BF16),
        "ws": jnp.transpose(cm_w_s, (0, 2, 1, 3)).reshape(depth, CM_CHUNK, CM_GROUPS * CM_CHUNK).astype(BF16),
        "bs": jnp.repeat(jnp.transpose(cm_b_s, (0, 2, 1)), CM_GROUP_DIM, axis=2),
    }
    w_out_b = w_out.astype(BF16)
    na_bias, win = _na_tables(na_rpb, rows)
    rope_x = _rope_tables(l)
    rope_c = _identity_rope_tables(lc)
    fg = final_g.reshape(1, d)

    for layer in range(depth):
        last = layer == depth - 1
        (aq, ak, avt, ga, bq, bk, bvt, gb, mc, pp, bgd) = _inproj(x, mod, row_x, lw, layer, rope_x, tm=1024)
        if last:
            akc, avct, bkc, bvct = _inproj_kv(ctx, mod, row_c, lw, layer, tm=lc)
        else:
            (aqc, akc, avct, gac, bqc, bkc, bvct, gbc, mcc, ppc, bgdc) = _inproj(ctx, mod, row_c, lw, layer,
                                                                                 rope_c, tm=lc)

        out_a = _neighbourhood_attention(aq, ak, avt, akc, avct, na_bias, layer, win, ga)
        out_b = _dense_attention(bq, [bk, bkc], [bvt, bvct], gb, MLA_HEAD_SPECS, tq=l, tr=512)
        x_new = _outproj(out_a, out_b, mc, pp, bgd, sc_w, w_out_b, layer, mod, row_x, x, fg, tm=1024,
                         final_norm=last)
        if not last:
            ctx_a = _dense_attention(aqc, [akc], [avct], gac, NA_HEAD_SPECS, tq=lc, tr=lc)
            ctx_b = _dense_attention(bqc, [bkc], [bvct], gbc, MLA_HEAD_SPECS, tq=lc, tr=lc)
            ctx = _outproj(ctx_a, ctx_b, mcc, ppc, bgdc, sc_w, w_out_b, layer, mod, row_c, ctx, fg, tm=lc,
                           final_norm=False)
        x = x_new
    return x
```
